```python
import math
import jax, jax.numpy as jnp
from jax import lax
import numpy as np

D_MODEL = 1024
BATCH = 8
SEQ = 2048
DEPTH = 1
DEC_BATCH = 32
DEC_SEQ = 64
PAST_LEN = 4096

CHUNK = 64
Q_BLOCK = 128
ATTN_HEADS = 8
ATTN_HEAD_DIM = 64
ATTN_V_DIM = 2 * ATTN_HEAD_DIM
ATTN_QK_WIDTH = ATTN_HEADS * 2 * ATTN_HEAD_DIM
ATTN_V_WIDTH = ATTN_HEADS * ATTN_V_DIM
ROPE_THETA = 10000.0
REC_HEADS = 8
REC_K = 128
REC_V = D_MODEL // REC_HEADS
REC_K_WIDTH = REC_HEADS * REC_K
REC_V_WIDTH = REC_HEADS * REC_V
N_EXPERTS = 32
TOP_K = 4
D_FF = D_MODEL
SWIGLU_ALPHA = 1.702
SWIGLU_LIMIT = 7.0
NORM_EPS = 1e-6
IN_WIDTHS = (ATTN_QK_WIDTH, ATTN_QK_WIDTH, ATTN_V_WIDTH,
             REC_K_WIDTH, REC_K_WIDTH, REC_V_WIDTH, REC_V_WIDTH,
             D_MODEL, D_MODEL)
IN_COLS = sum(IN_WIDTHS)
IN_SPLITS = tuple(int(s) for s in np.cumsum(IN_WIDTHS)[:-1])

kernel_name = "hybrid_diffattn_hgrn2_moe_stream_step"


def _lambda_init(layer):
    return 0.8 - 0.6 * math.exp(-0.3 * layer)


def _rmsnorm(x, g):
    xf = x.astype(jnp.float32)
    y = xf * lax.rsqrt(jnp.mean(xf * xf, axis=-1, keepdims=True) + NORM_EPS)
    return (y * g.astype(jnp.float32)).astype(x.dtype)


def _rope(x, pos):
    half = ATTN_HEAD_DIM // 2
    inv = jnp.power(ROPE_THETA, -2.0 * jnp.arange(half, dtype=jnp.float32) / ATTN_HEAD_DIM)
    ang = pos.astype(jnp.float32)[:, None] * inv[None, :]
    cos = jnp.cos(ang)[:, None, None, :]
    sin = jnp.sin(ang)[:, None, None, :]
    xf = x.astype(jnp.float32)
    x1, x2 = xf[..., :half], xf[..., half:]
    return jnp.concatenate([x1 * cos - x2 * sin, x2 * cos + x1 * sin], axis=-1).astype(x.dtype)


def _mixer_inputs(h, w_in, lb, pos):
    B, L, _ = h.shape
    z = h @ w_in
    q, k, v, rf, rq, ri, rg, ga, gr = jnp.split(z, IN_SPLITS, axis=-1)
    q = _rope(q.reshape(B, L, ATTN_HEADS, 2, ATTN_HEAD_DIM), pos)
    k = _rope(k.reshape(B, L, ATTN_HEADS, 2, ATTN_HEAD_DIM), pos)
    v = v.reshape(B, L, ATTN_HEADS, ATTN_V_DIM)
    f = lb + (1.0 - lb) * jax.nn.sigmoid(rf.astype(jnp.float32))
    logf = jnp.log(f).reshape(B, L, REC_HEADS, REC_K)
    rk = (1.0 - f).reshape(B, L, REC_HEADS, REC_K)
    rq = rq.astype(jnp.float32).reshape(B, L, REC_HEADS, REC_K)
    ri = ri.astype(jnp.float32).reshape(B, L, REC_HEADS, REC_V)
    rg = rg.reshape(B, L, REC_HEADS, REC_V)
    return q, k, v, rq, rk, ri, logf, rg, ga, gr


def _diff_attend(q, k, v, q_pos, k_pos, lam):
    s = jnp.einsum('bqhcd,bkhcd->bchqk', q, k).astype(jnp.float32) * (ATTN_HEAD_DIM ** -0.5)
    mask = (k_pos[None, :] // CHUNK) <= (q_pos[:, None] // CHUNK)
    p = jax.nn.softmax(jnp.where(mask, s, -jnp.inf), axis=-1)
    w = p[:, 0] - lam * p[:, 1]
    return jnp.einsum('bhqk,bkhv->bqhv', w.astype(v.dtype), v)


def _hgrn2_chunk(S, q, k, v, logf):
    L = q.shape[1]
    b = jnp.cumsum(logf, axis=1)
    qd = q * jnp.exp(b)
    kd = k * jnp.exp(-b)
    causal = jnp.tril(jnp.ones((L, L), dtype=bool))
    a = jnp.where(causal, jnp.einsum('blhk,bshk->bhls', qd, kd), 0.0)
    o = jnp.einsum('blhk,bhkv->blhv', qd, S) + jnp.einsum('bhls,bshv->blhv', a, v)
    b_last = b[:, -1]
    S_new = (jnp.exp(b_last)[..., None] * S
             + jnp.einsum('bshk,bshv->bhkv', k * jnp.exp(b_last[:, None] - b), v))
    return S_new, o


def _hgrn2_prompt(q, k, v, logf):
    B, L, H, K = q.shape
    n_chunks = L // CHUNK

    def to_chunks(t):
        return t.reshape(B, n_chunks, CHUNK, *t.shape[2:]).swapaxes(0, 1)

    S0 = jnp.zeros((B, H, K, REC_V), jnp.float32)
    S_fin, o = lax.scan(lambda S, xs: _hgrn2_chunk(S, *xs), S0,
                        (to_chunks(q), to_chunks(k), to_chunks(v), to_chunks(logf)))
    return o.swapaxes(0, 1).reshape(B, L, H, REC_V), S_fin


def _branch_merge(ao, ro, rg, ga, gr, lam_init, g_subln, w_o_attn, g_rec_norm, w_o_rec, w_out):
    B, L = ga.shape[:2]
    dt = ga.dtype
    a = (_rmsnorm(ao, g_subln) * (1.0 - lam_init)).astype(dt).reshape(B, L, ATTN_V_WIDTH) @ w_o_attn
    r = (_rmsnorm(ro, g_rec_norm) * jax.nn.silu(rg.astype(jnp.float32))).astype(dt)
    r = r.reshape(B, L, REC_V_WIDTH) @ w_o_rec
    mixed = jax.nn.sigmoid(ga) * a + jax.nn.sigmoid(gr) * r
    return mixed @ w_out


def _moe(h, w_router, b_router, w_gate, b_gate, w_up, b_up, w_down, b_down):
    B, L, D = h.shape
    t = h.reshape(B * L, D)
    logits = (t @ w_router).astype(jnp.float32) + b_router.astype(jnp.float32)
    top_v, top_i = lax.top_k(logits, TOP_K)
    probs = jax.nn.softmax(top_v, axis=-1)
    comb = jnp.sum(jax.nn.one_hot(top_i, N_EXPERTS, dtype=jnp.float32) * probs[..., None], axis=1)
    y = jnp.zeros((B * L, D), jnp.float32)
    for e in range(N_EXPERTS):
        gate = jnp.minimum(t @ w_gate[e] + b_gate[e], SWIGLU_LIMIT)
        up = jnp.clip(t @ w_up[e] + b_up[e], -SWIGLU_LIMIT, SWIGLU_LIMIT)
        act = gate * jax.nn.sigmoid(SWIGLU_ALPHA * gate) * (up + 1.0)
        y = y + comb[:, e:e + 1] * (act @ w_down[e] + b_down[e]).astype(jnp.float32)
    return y.astype(h.dtype).reshape(B, L, D)


def setup_inputs(seed: int = 0) -> dict:
    key = jax.random.key(seed)
    ks = jax.random.split(key, 32)
    f32 = jnp.float32

    def nrm(k, shape, scale=1.0):
        return scale * jax.random.normal(k, shape, f32)

    return {
        "x_prompt": nrm(ks[0], (BATCH, SEQ, D_MODEL)),
        "x_sample": nrm(ks[1], (DEC_BATCH, DEC_SEQ, D_MODEL)),
        "cache_k": nrm(ks[2], (DEPTH, DEC_BATCH, PAST_LEN, ATTN_HEADS, 2 * ATTN_HEAD_DIM)),
        "cache_v": nrm(ks[3], (DEPTH, DEC_BATCH, PAST_LEN, ATTN_HEADS, ATTN_V_DIM)),
        "state_rec": nrm(ks[4], (DEPTH, DEC_BATCH, REC_HEADS, REC_K, REC_V), 0.5),
        "g_mix": 1.0 + nrm(ks[5], (DEPTH, D_MODEL), 0.02),
        "w_in": nrm(ks[6], (DEPTH, D_MODEL, IN_COLS), D_MODEL ** -0.5),
        "lambda_q1": nrm(ks[7], (DEPTH, ATTN_HEAD_DIM), 0.1),
        "lambda_k1": nrm(ks[8], (DEPTH, ATTN_HEAD_DIM), 0.1),
        "lambda_q2": nrm(ks[9], (DEPTH, ATTN_HEAD_DIM), 0.1),
        "lambda_k2": nrm(ks[10], (DEPTH, ATTN_HEAD_DIM), 0.1),
        "g_subln": 1.0 + nrm(ks[11], (DEPTH, ATTN_V_DIM), 0.02),
        "w_o_attn": nrm(ks[12], (DEPTH, ATTN_V_WIDTH, D_MODEL), ATTN_V_WIDTH ** -0.5),
        "lb_logits": nrm(ks[13], (DEPTH + 1, REC_K_WIDTH), 0.1),
        "g_rec_norm": 1.0 + nrm(ks[14], (DEPTH, REC_V), 0.02),
        "w_o_rec": nrm(ks[15], (DEPTH, REC_V_WIDTH, D_MODEL), REC_V_WIDTH ** -0.5),
        "w_out": nrm(ks[16], (DEPTH, D_MODEL, D_MODEL), D_MODEL ** -0.5),
        "g_ffn": 1.0 + nrm(ks[17], (DEPTH, D_MODEL), 0.02),
        "w_router": nrm(ks[18], (DEPTH, D_MODEL, N_EXPERTS), D_MODEL ** -0.5),
        "b_router": nrm(ks[19], (DEPTH, N_EXPERTS), 0.01),
        "w_gate": nrm(ks[20], (DEPTH, N_EXPERTS, D_MODEL, D_FF), D_MODEL ** -0.5),
        "b_gate": nrm(ks[21], (DEPTH, N_EXPERTS, D_FF), 0.01),
        "w_up": nrm(ks[22], (DEPTH, N_EXPERTS, D_MODEL, D_FF), D_MODEL ** -0.5),
        "b_up": nrm(ks[23], (DEPTH, N_EXPERTS, D_FF), 0.01),
        "w_down": nrm(ks[24], (DEPTH, N_EXPERTS, D_FF, D_MODEL), D_FF ** -0.5),
        "b_down": nrm(ks[25], (DEPTH, N_EXPERTS, D_MODEL), 0.01),
        "g_final": 1.0 + nrm(ks[26], (D_MODEL,), 0.02),
    }


def reference(x_prompt, x_sample, cache_k, cache_v, state_rec, g_mix, w_in, lambda_q1, lambda_k1,
              lambda_q2, lambda_k2, g_subln, w_o_attn, lb_logits, g_rec_norm, w_o_rec, w_out, g_ffn,
              w_router, b_router, w_gate, b_gate, w_up, b_up, w_down, b_down, g_final):
    B, S, _ = x_prompt.shape
    Bd, Ld, _ = x_sample.shape
    P = cache_k.shape[2]
    pos_p = jnp.arange(S, dtype=jnp.int32)
    q_pos_s = P + jnp.arange(Ld, dtype=jnp.int32)
    k_pos_s = jnp.arange(P + Ld, dtype=jnp.int32)
    n_blocks = S // Q_BLOCK
    lb_all = jnp.cumsum(jax.nn.softmax(lb_logits.astype(jnp.float32), axis=0), axis=0)

    xp, xs = x_prompt, x_sample
    kp_list, vp_list, sp_list, ks_list, vs_list, ss_list = [], [], [], [], [], []
    for l in range(DEPTH):
        lam_init = _lambda_init(l)
        lam = (jnp.exp(jnp.sum(lambda_q1[l].astype(jnp.float32) * lambda_k1[l].astype(jnp.float32)))
               - jnp.exp(jnp.sum(lambda_q2[l].astype(jnp.float32) * lambda_k2[l].astype(jnp.float32)))
               + lam_init)
        merge_w = (lam_init, g_subln[l], w_o_attn[l], g_rec_norm[l], w_o_rec[l], w_out[l])
        moe_w = (w_router[l], b_router[l], w_gate[l], b_gate[l], w_up[l], b_up[l], w_down[l], b_down[l])

        h = _rmsnorm(xp, g_mix[l])
        q, k, v, rq, rk, ri, logf, rg, ga, gr = _mixer_inputs(h, w_in[l], lb_all[l], pos_p)
        qb = q.reshape(B, n_blocks, Q_BLOCK, ATTN_HEADS, 2, ATTN_HEAD_DIM).swapaxes(0, 1)
        pb = pos_p.reshape(n_blocks, Q_BLOCK)
        ao = lax.map(lambda a: _diff_attend(a[0], k, v, a[1], pos_p, lam), (qb, pb))
        ao = ao.swapaxes(0, 1).reshape(B, S, ATTN_HEADS, ATTN_V_DIM)
        ro, S_fin = _hgrn2_prompt(rq, rk, ri, logf)
        xp = xp + _branch_merge(ao, ro, rg, ga, gr, *merge_w)
        xp = xp + _moe(_rmsnorm(xp, g_ffn[l]), *moe_w)
        kp_list.append(k.reshape(B, S, ATTN_HEADS, 2 * ATTN_HEAD_DIM))
        vp_list.append(v)
        sp_list.append(S_fin.astype(x_prompt.dtype))

        h = _rmsnorm(xs, g_mix[l])
        q, k, v, rq, rk, ri, logf, rg, ga, gr = _mixer_inputs(h, w_in[l], lb_all[l], q_pos_s)
        k_all = jnp.concatenate(
            [cache_k[l].reshape(Bd, P, ATTN_HEADS, 2, ATTN_HEAD_DIM).astype(k.dtype), k], axis=1)
        v_all = jnp.concatenate([cache_v[l].astype(v.dtype), v], axis=1)
        ao = _diff_attend(q, k_all, v_all, q_pos_s, k_pos_s, lam)
        S_new, ro = _hgrn2_chunk(state_rec[l].astype(jnp.float32), rq, rk, ri, logf)
        xs = xs + _branch_merge(ao, ro, rg, ga, gr, *merge_w)
        xs = xs + _moe(_rmsnorm(xs, g_ffn[l]), *moe_w)
        ks_list.append(k.reshape(Bd, Ld, ATTN_HEADS, 2 * ATTN_HEAD_DIM))
        vs_list.append(v)
        ss_list.append(S_new.astype(state_rec.dtype))

    y_prompt = _rmsnorm(xp, g_final)
    y_sample = _rmsnorm(xs, g_final)
    return (y_prompt, y_sample, jnp.stack(kp_list), jnp.stack(vp_list), jnp.stack(sp_list),
            jnp.stack(ks_list), jnp.stack(vs_list), jnp.stack(ss_list))
```

```python
import functools
import math

import jax
import jax.numpy as jnp
from jax import lax
from jax.experimental import pallas as pl
from jax.experimental.pallas import tpu as pltpu

F32 = jnp.float32
BF16 = jnp.bfloat16

D_MODEL = 1024
CHUNK = 64
HEADS = 8
HEAD_W = 128
ATTN_HEAD_DIM = 64
ROPE_THETA = 10000.0
N_EXPERTS = 32
TOP_K = 4
SWIGLU_ALPHA = 1.702
SWIGLU_LIMIT = 7.0
NORM_EPS = 1e-6
N_COL_BLOCKS = 9
COL_Q, COL_K, COL_V, COL_F, COL_RQ, COL_RI, COL_RG, COL_GA, COL_GR = range(9)

PROJ_TM = 1024
ATTN_TQ = 256
ATTN_TK = 1024
MERGE_TM = 512
MOE_TM = 256
ROW_TM = 256
VMEM_LIMIT = 56 * 1024 * 1024


def _cparams(*sem):
    return pltpu.CompilerParams(dimension_semantics=sem, vmem_limit_bytes=VMEM_LIMIT)


def _dot(a, b):
    return jnp.dot(a, b, preferred_element_type=F32)


def _dot_nt(a, b):
    return lax.dot_general(a, b, (((1,), (1,)), ((), ())), preferred_element_type=F32)


def _proj_kernel(x_ref, g_ref, w_ref, cos_ref, sin_ref, lbl_ref, o_ref, h_ref):
    j = pl.program_id(1)

    @pl.when(j == 0)
    def _():
        x = x_ref[...]
        ms = jnp.mean(x * x, axis=-1, keepdims=True)
        h_ref[...] = (x * lax.rsqrt(ms + NORM_EPS) * g_ref[...]).astype(BF16)

    z = _dot(h_ref[...], w_ref[...])

    @pl.when(j <= COL_K)
    def _():
        scale = jnp.where(j == COL_Q, ATTN_HEAD_DIM ** -0.5, 1.0).astype(F32)
        c = cos_ref[...]
        s = sin_ref[...]
        lane = lax.broadcasted_iota(jnp.int32, c.shape, 1)
        first = (lane % ATTN_HEAD_DIM) < (ATTN_HEAD_DIM // 2)
        for h in range(HEADS):
            sl = slice(h * HEAD_W, (h + 1) * HEAD_W)
            zh = z[:, sl]
            partner = jnp.where(first, pltpu.roll(zh, HEAD_W - ATTN_HEAD_DIM // 2, 1),
                                pltpu.roll(zh, ATTN_HEAD_DIM // 2, 1))
            o_ref[:, sl] = (zh * c + partner * s) * scale

    @pl.when(j == COL_F)
    def _():
        lbl = lbl_ref[...]
        e = jnp.exp(lbl - jnp.max(lbl, axis=0, keepdims=True))
        lb = e[0:1] / jnp.sum(e, axis=0, keepdims=True)
        o_ref[...] = lb + (1.0 - lb) * jax.nn.sigmoid(z)

    @pl.when((j == COL_V) | (j > COL_F))
    def _():
        o_ref[...] = z


def _project(x, g, w_bf16, cos_t, sin_t, lb_logits):
    t = x.shape[0]
    tm = PROJ_TM
    n_tab = cos_t.shape[0] // tm
    return pl.pallas_call(
        _proj_kernel,
        grid=(t // tm, N_COL_BLOCKS),
        in_specs=[
            pl.BlockSpec((tm, D_MODEL), lambda i, j: (i, 0)),
            pl.BlockSpec((1, D_MODEL), lambda i, j: (0, 0)),
            pl.BlockSpec((D_MODEL, D_MODEL), lambda i, j: (0, j)),
            pl.BlockSpec((tm, HEAD_W), lambda i, j: (i % n_tab, 0)),
            pl.BlockSpec((tm, HEAD_W), lambda i, j: (i % n_tab, 0)),
            pl.BlockSpec((2, D_MODEL), lambda i, j: (0, 0)),
        ],
        out_specs=pl.BlockSpec((tm, D_MODEL), lambda i, j: (i, j)),
        out_shape=jax.ShapeDtypeStruct((t, N_COL_BLOCKS * D_MODEL), F32),
        scratch_shapes=[pltpu.VMEM((tm, D_MODEL), BF16)],
        compiler_params=_cparams("arbitrary", "arbitrary"),
        name="proj",
    )(x, g, w_bf16, cos_t, sin_t, lb_logits)


def _rope_tables(pos):
    half = ATTN_HEAD_DIM // 2
    inv = jnp.power(ROPE_THETA, -2.0 * jnp.arange(half, dtype=F32) / ATTN_HEAD_DIM)
    ang = pos.astype(F32)[:, None] * inv[None, :]
    cos = jnp.tile(jnp.cos(ang), (1, HEAD_W // half))
    sin = jnp.sin(ang)
    sin = jnp.tile(jnp.concatenate([-sin, sin], axis=-1), (1, HEAD_W // ATTN_HEAD_DIM))
    return cos, sin


def _split_components(q):
    lane = lax.broadcasted_iota(jnp.int32, q.shape, 1)
    q1 = jnp.where(lane < ATTN_HEAD_DIM, q, 0.0)
    q2 = jnp.where(lane >= ATTN_HEAD_DIM, q, 0.0)
    return jnp.concatenate([q1, q2], axis=0).astype(BF16)


def _attn_prompt_kernel(lam_ref, q_ref, k_ref, v_ref, o_ref):
    lam = lam_ref[0]
    tq = ATTN_TQ
    seq = k_ref.shape[0]
    k = k_ref[...].astype(BF16)
    v = v_ref[...].astype(BF16)
    row = lax.broadcasted_iota(jnp.int32, (2 * tq, tq), 0)
    col = lax.broadcasted_iota(jnp.int32, (2 * tq, tq), 1)
    visible = (col // CHUNK) <= ((row % tq) // CHUNK)
    for qi in range(seq // tq):
        lo = qi * tq
        qq = _split_components(q_ref[lo:lo + tq, :])
        s_d = jnp.where(visible, _dot_nt(qq, k[lo:lo + tq]), -jnp.inf)
        m = jnp.max(s_d, axis=-1, keepdims=True)
        if qi > 0:
            s_m = _dot_nt(qq, k[:lo])
            m = jnp.maximum(m, jnp.max(s_m, axis=-1, keepdims=True))
            p_m = jnp.exp(s_m - m)
        p_d = jnp.exp(s_d - m)
        l = jnp.sum(p_d, axis=-1, keepdims=True)
        if qi > 0:
            l = l + jnp.sum(p_m, axis=-1, keepdims=True)
        r = 1.0 / l
        r1 = r[:tq]
        r2 = lam * r[tq:]
        w_d = (p_d[:tq] * r1 - p_d[tq:] * r2).astype(BF16)
        o = _dot(w_d, v[lo:lo + tq])
        if qi > 0:
            w_m = (p_m[:tq] * r1 - p_m[tq:] * r2).astype(BF16)
            o = o + _dot(w_m, v[:lo])
        o_ref[lo:lo + tq, :] = o


def _attn_prompt(lam, z, n_batch, seq):
    return pl.pallas_call(
        _attn_prompt_kernel,
        grid=(n_batch, HEADS),
        in_specs=[
            pl.BlockSpec(memory_space=pltpu.SMEM),
            pl.BlockSpec((seq, HEAD_W), lambda b, h: (b, COL_Q * HEADS + h)),
            pl.BlockSpec((seq, HEAD_W), lambda b, h: (b, COL_K * HEADS + h)),
            pl.BlockSpec((seq, HEAD_W), lambda b, h: (b, COL_V * HEADS + h)),
        ],
        out_specs=pl.BlockSpec((seq, HEAD_W), lambda b, h: (b, h)),
        out_shape=jax.ShapeDtypeStruct((n_batch * seq, D_MODEL), F32),
        compiler_params=_cparams("arbitrary", "arbitrary"),
        name="attn_prompt",
    )(lam, z, z, z)


def _attn_sample_kernel(lam_ref, q_ref, kn_ref, vn_ref, ck_ref, cv_ref, o_ref, m_ref, l_ref, acc_ref):
    kt = pl.program_id(1)
    lam = lam_ref[0]
    n_q = q_ref.shape[0]

    def head(h):
        return slice(h * HEAD_W, (h + 1) * HEAD_W)

    @pl.when(kt == 0)
    def _():
        for h in range(HEADS):
            qq = _split_components(q_ref[:, head(h)])
            s = _dot_nt(qq, kn_ref[:, head(h)].astype(BF16))
            m = jnp.max(s, axis=-1, keepdims=True)
            p = jnp.exp(s - m)
            m_ref[h] = m
            l_ref[h] = jnp.sum(p, axis=-1, keepdims=True)
            acc_ref[h] = _dot(p.astype(BF16), vn_ref[:, head(h)].astype(BF16))

    for h in range(HEADS):
        qq = _split_components(q_ref[:, head(h)])
        s = _dot_nt(qq, ck_ref[0, :, h, :].astype(BF16))
        m_old = m_ref[h]
        m_new = jnp.maximum(m_old, jnp.max(s, axis=-1, keepdims=True))
        alpha = jnp.exp(m_old - m_new)
        p = jnp.exp(s - m_new)
        l_ref[h] = alpha * l_ref[h] + jnp.sum(p, axis=-1, keepdims=True)
        acc_ref[h] = alpha * acc_ref[h] + _dot(p.astype(BF16), cv_ref[0, :, h, :].astype(BF16))
        m_ref[h] = m_new

    @pl.when(kt == pl.num_programs(1) - 1)
    def _():
        for h in range(HEADS):
            on = acc_ref[h] * (1.0 / l_ref[h])
            o_ref[:, head(h)] = on[:n_q] - lam * on[n_q:]


def _attn_sample(lam, z, cache_k, cache_v, n_batch, n_q):
    past = cache_k.shape[1]
    tk = ATTN_TK
    return pl.pallas_call(
        _attn_sample_kernel,
        grid=(n_batch, past // tk),
        in_specs=[
            pl.BlockSpec(memory_space=pltpu.SMEM),
            pl.BlockSpec((n_q, D_MODEL), lambda b, t: (b, COL_Q)),
            pl.BlockSpec((n_q, D_MODEL), lambda b, t: (b, COL_K)),
            pl.BlockSpec((n_q, D_MODEL), lambda b, t: (b, COL_V)),
            pl.BlockSpec((1, tk, HEADS, HEAD_W), lambda b, t: (b, t, 0, 0)),
            pl.BlockSpec((1, tk, HEADS, HEAD_W), lambda b, t: (b, t, 0, 0)),
        ],
        out_specs=pl.BlockSpec((n_q, D_MODEL), lambda b, t: (b, 0)),
        out_shape=jax.ShapeDtypeStruct((n_batch * n_q, D_MODEL), F32),
        scratch_shapes=[
            pltpu.VMEM((HEADS, 2 * n_q, 1), F32),
            pltpu.VMEM((HEADS, 2 * n_q, 1), F32),
            pltpu.VMEM((HEADS, 2 * n_q, HEAD_W), F32),
        ],
        compiler_params=_cparams("arbitrary", "arbitrary"),
        name="attn_sample",
    )(lam, z, z, z, cache_k, cache_v)


def _split3(x):
    h1 = x.astype(BF16)
    r1 = x - h1.astype(F32)
    h2 = r1.astype(BF16)
    h3 = (r1 - h2.astype(F32)).astype(BF16)
    return h1, h2, h3


def _when_step(step, n_steps, which):
    def deco(body):
        if n_steps == 1:
            body()
        else:
            pl.when(step == which)(body)
    return deco


def _hgrn_kernel(n_chunks, f_ref, q_ref, i_ref, s0_ref, o_ref, sout_ref, st_ref):
    c = pl.program_id(1)

    @_when_step(c, n_chunks, 0)
    def _():
        for h in range(HEADS):
            st_ref[h] = s0_ref[0, h].T

    f = f_ref[...]
    logf = jnp.log(f)
    row = lax.broadcasted_iota(jnp.int32, (CHUNK, CHUNK), 0)
    col = lax.broadcasted_iota(jnp.int32, (CHUNK, CHUNK), 1)
    causal = row >= col
    tril = jnp.where(causal, 1.0, 0.0).astype(BF16)
    h1, h2, h3 = _split3(logf)
    b = _dot(tril, h1) + _dot(tril, h2) + _dot(tril, h3)
    b_last = b[CHUNK - 1:CHUNK, :]
    rk = 1.0 - f
    qd = q_ref[...] * jnp.exp(b)
    kd = rk * jnp.exp(-b)
    kl = rk * jnp.exp(b_last - b)
    eb_last = jnp.exp(b_last)
    v = i_ref[...]
    for h in range(HEADS):
        sl = slice(h * HEAD_W, (h + 1) * HEAD_W)
        qh = qd[:, sl].astype(BF16)
        vh = v[:, sl]
        a = jnp.where(causal, _dot_nt(qh, kd[:, sl].astype(BF16)), 0.0)
        st = st_ref[h]
        o_ref[:, sl] = _dot_nt(qh, st.astype(BF16)) + _dot(a.astype(BF16), vh.astype(BF16))
        st_ref[h] = st * eb_last[:, sl] + _dot(vh.T.astype(BF16), kl[:, sl].astype(BF16))

    @_when_step(c, n_chunks, n_chunks - 1)
    def _():
        for h in range(HEADS):
            sout_ref[0, h] = st_ref[h].T


def _hgrn(z, s0, n_batch, n_chunks):
    t = z.shape[0]
    return pl.pallas_call(
        functools.partial(_hgrn_kernel, n_chunks),
        grid=(n_batch, n_chunks),
        in_specs=[
            pl.BlockSpec((CHUNK, D_MODEL), lambda b, c: (b * n_chunks + c, COL_F)),
            pl.BlockSpec((CHUNK, D_MODEL), lambda b, c: (b * n_chunks + c, COL_RQ)),
            pl.BlockSpec((CHUNK, D_MODEL), lambda b, c: (b * n_chunks + c, COL_RI)),
            pl.BlockSpec((1, HEADS, HEAD_W, HEAD_W), lambda b, c: (b, 0, 0, 0)),
        ],
        out_specs=[
            pl.BlockSpec((CHUNK, D_MODEL), lambda b, c: (b * n_chunks + c, 0)),
            pl.BlockSpec((1, HEADS, HEAD_W, HEAD_W), lambda b, c: (b, 0, 0, 0)),
        ],
        out_shape=[
            jax.ShapeDtypeStruct((t, D_MODEL), F32),
            jax.ShapeDtypeStruct((n_batch, HEADS, HEAD_W, HEAD_W), F32),
        ],
        scratch_shapes=[pltpu.VMEM((HEADS, HEAD_W, HEAD_W), F32)],
        compiler_params=_cparams("arbitrary", "arbitrary"),
        name="hgrn",
    )(z, z, z, s0)


def _head_rmsnorm(x, g):
    outs = []
    for h in range(HEADS):
        xh = x[:, h * HEAD_W:(h + 1) * HEAD_W]
        ms = jnp.mean(xh * xh, axis=-1, keepdims=True)
        outs.append(xh * lax.rsqrt(ms + NORM_EPS) * g)
    return jnp.concatenate(outs, axis=-1)


def _merge_kernel(lam_init, ao_ref, ro_ref, rg_ref, ga_ref, gr_ref, x_ref, gs_ref, grn_ref,
                  woa_ref, wor_ref, wout_ref, gffn_ref, wr_ref, br_ref, cnt_in_ref,
                  x1_ref, xn_ref, idx_ref, prob_ref, rank_ref, cnt_ref, run_ref):
    i = pl.program_id(0)
    tm = x_ref.shape[0]

    @pl.when(i == 0)
    def _():
        run_ref[...] = cnt_in_ref[...]

    an = (_head_rmsnorm(ao_ref[...], gs_ref[...]) * (1.0 - lam_init)).astype(BF16)
    rn = (_head_rmsnorm(ro_ref[...], grn_ref[...]) * jax.nn.silu(rg_ref[...])).astype(BF16)
    a = _dot(an, woa_ref[...])
    r = _dot(rn, wor_ref[...])
    mixed = jax.nn.sigmoid(ga_ref[...]) * a + jax.nn.sigmoid(gr_ref[...]) * r
    x1 = x_ref[...] + _dot(mixed.astype(BF16), wout_ref[...])
    x1_ref[...] = x1
    ms = jnp.mean(x1 * x1, axis=-1, keepdims=True)
    xn = x1 * lax.rsqrt(ms + NORM_EPS) * gffn_ref[...]
    xn_ref[...] = xn

    logits = _dot(xn.astype(BF16), wr_ref[...]) + br_ref[...]
    lane = lax.broadcasted_iota(jnp.int32, logits.shape, 1).astype(F32)
    sel = jnp.zeros(logits.shape, F32)
    work = logits
    tops, idxs = [], []
    for _ in range(TOP_K):
        m = jnp.max(work, axis=-1, keepdims=True)
        idx = jnp.min(jnp.where(work == m, lane, float(N_EXPERTS)), axis=-1, keepdims=True)
        hit = lane == idx
        sel = jnp.where(hit, 1.0, sel)
        work = jnp.where(hit, -jnp.inf, work)
        tops.append(m)
        idxs.append(idx)
    es = [jnp.exp(v - tops[0]) for v in tops]
    inv = 1.0 / (es[0] + es[1] + es[2] + es[3])

    trow = lax.broadcasted_iota(jnp.int32, (tm, tm), 0)
    tcol = lax.broadcasted_iota(jnp.int32, (tm, tm), 1)
    before = jnp.where(trow > tcol, 1.0, 0.0).astype(BF16)
    ranks = _dot(before, sel.astype(BF16)) + run_ref[...]
    k_lane = lax.broadcasted_iota(jnp.int32, (tm, TOP_K), 1)
    idx_out = jnp.zeros((tm, TOP_K), F32)
    prob_out = jnp.zeros((tm, TOP_K), F32)
    rank_out = jnp.zeros((tm, TOP_K), F32)
    for k in range(TOP_K):
        rk = jnp.sum(jnp.where(lane == idxs[k], ranks, 0.0), axis=-1, keepdims=True)
        idx_out = jnp.where(k_lane == k, idxs[k], idx_out)
        prob_out = jnp.where(k_lane == k, es[k] * inv, prob_out)
        rank_out = jnp.where(k_lane == k, rk, rank_out)
    idx_ref[...] = idx_out.astype(jnp.int32)
    prob_ref[...] = prob_out
    rank_ref[...] = rank_out.astype(jnp.int32)
    run = run_ref[...] + jnp.sum(sel, axis=0, keepdims=True)
    run_ref[...] = run
    cnt_ref[...] = run


def _merge(lam_init, ao, ro, z, x, g_subln, g_rec, woa, wor, wout, g_ffn, w_router, b_router, cnt_in):
    t = x.shape[0]
    tm = MERGE_TM
    row = lambda i: (i, 0)
    fixed = lambda i: (0, 0)
    tok = pl.BlockSpec((tm, D_MODEL), row)
    wspec = pl.BlockSpec((D_MODEL, D_MODEL), fixed)
    narrow = pl.BlockSpec((tm, TOP_K), row)
    return pl.pallas_call(
        functools.partial(_merge_kernel, lam_init),
        grid=(t // tm,),
        in_specs=[
            tok, tok,
            pl.BlockSpec((tm, D_MODEL), lambda i: (i, COL_RG)),
            pl.BlockSpec((tm, D_MODEL), lambda i: (i, COL_GA)),
            pl.BlockSpec((tm, D_MODEL), lambda i: (i, COL_GR)),
            tok,
            pl.BlockSpec((1, HEAD_W), fixed), pl.BlockSpec((1, HEAD_W), fixed),
            wspec, wspec, wspec,
            pl.BlockSpec((1, D_MODEL), fixed),
            pl.BlockSpec((D_MODEL, N_EXPERTS), fixed),
            pl.BlockSpec((1, N_EXPERTS), fixed),
            pl.BlockSpec((1, N_EXPERTS), fixed),
        ],
        out_specs=[tok, tok, narrow, narrow, narrow, pl.BlockSpec((1, N_EXPERTS), fixed)],
        out_shape=[
            jax.ShapeDtypeStruct((t, D_MODEL), F32),
            jax.ShapeDtypeStruct((t, D_MODEL), F32),
            jax.ShapeDtypeStruct((t, TOP_K), jnp.int32),
            jax.ShapeDtypeStruct((t, TOP_K), F32),
            jax.ShapeDtypeStruct((t, TOP_K), jnp.int32),
            jax.ShapeDtypeStruct((1, N_EXPERTS), F32),
        ],
        scratch_shapes=[pltpu.VMEM((1, N_EXPERTS), F32)],
        compiler_params=_cparams("arbitrary"),
        name="merge",
    )(ao, ro, z, z, z, x, g_subln, g_rec, woa, wor, wout, g_ffn, w_router, b_router, cnt_in)


def _row_copy(src, dst, sem):
    return pltpu.make_async_copy(src, dst, sem)


def _dispatch_kernel(n_first, pos_ref, xa_ref, xb_ref, xs_ref, sem):
    i = pl.program_id(0)

    def scatter(x_ref):
        def start(r, carry):
            for k in range(TOP_K):
                _row_copy(x_ref.at[pl.ds(r, 1)], xs_ref.at[pl.ds(pos_ref[0, 0, r * TOP_K + k], 1)], sem).start()
            return carry

        lax.fori_loop(0, x_ref.shape[0], start, 0)

        def wait(r, carry):
            for k in range(TOP_K):
                _row_copy(x_ref.at[pl.ds(0, 1)], xs_ref.at[pl.ds(0, 1)], sem).wait()
            return carry

        lax.fori_loop(0, x_ref.shape[0], wait, 0)

    @pl.when(i < n_first)
    def _():
        scatter(xa_ref)

    @pl.when(i >= n_first)
    def _():
        scatter(xb_ref)


def _dispatch(pos, xn_a, xn_b):
    tm = ROW_TM
    n_a, n_b = xn_a.shape[0] // tm, xn_b.shape[0] // tm
    pos3 = pos.reshape(n_a + n_b, 1, tm * TOP_K)
    return pl.pallas_call(
        functools.partial(_dispatch_kernel, n_a),
        grid=(n_a + n_b,),
        in_specs=[
            pl.BlockSpec((1, 1, tm * TOP_K), lambda i: (i, 0, 0), memory_space=pltpu.SMEM),
            pl.BlockSpec((tm, D_MODEL), lambda i: (jnp.minimum(i, n_a - 1), 0)),
            pl.BlockSpec((tm, D_MODEL), lambda i: (jnp.maximum(i - n_a, 0), 0)),
        ],
        out_specs=pl.BlockSpec(memory_space=pl.ANY),
        out_shape=jax.ShapeDtypeStruct((pos.size, D_MODEL), F32),
        scratch_shapes=[pltpu.SemaphoreType.DMA],
        compiler_params=_cparams("arbitrary"),
        name="dispatch",
    )(pos3, xn_a, xn_b)


def _experts_kernel(tile_ref, exp_ref, lo_ref, hi_ref, first_ref, fresh_ref,
                    xs_ref, wg_ref, bg_ref, wu_ref, bu_ref, wd_ref, bd_ref, o_ref,
                    wg_s, wu_s, wd_s):
    w = pl.program_id(0)
    lo = lo_ref[w]
    hi = hi_ref[w]

    @pl.when(fresh_ref[w] == 1)
    def _():
        wg_s[...] = wg_ref[0].astype(BF16)
        wu_s[...] = wu_ref[0].astype(BF16)
        wd_s[...] = wd_ref[0].astype(BF16)

    @pl.when(hi > lo)
    def _():
        x = xs_ref[...].astype(BF16)
        gate = jnp.minimum(_dot(x, wg_s[...]) + bg_ref[0], SWIGLU_LIMIT)
        up = jnp.clip(_dot(x, wu_s[...]) + bu_ref[0], -SWIGLU_LIMIT, SWIGLU_LIMIT)
        act = gate * jax.nn.sigmoid(SWIGLU_ALPHA * gate) * (up + 1.0)
        out = _dot(act.astype(BF16), wd_s[...]) + bd_ref[0]
        row = lax.broadcasted_iota(jnp.int32, (o_ref.shape[0], 1), 0)
        mine = (row >= lo) & (row < hi)

        @pl.when(first_ref[w] == 1)
        def _():
            o_ref[...] = jnp.where(mine, out, 0.0)

        @pl.when(first_ref[w] == 0)
        def _():
            o_ref[...] = jnp.where(mine, out, o_ref[...])


def _experts(plan, xs, w_gate, b_gate, w_up, b_up, w_down, b_down):
    p = xs.shape[0]
    tm = MOE_TM
    n_items = plan[0].shape[0]
    rows = lambda w, tile, exp, lo, hi, first, fresh: (tile[w], 0)
    wsel = lambda w, tile, exp, lo, hi, first, fresh: (exp[w], 0, 0)
    wspec = pl.BlockSpec((1, D_MODEL, D_MODEL), wsel)
    bspec = pl.BlockSpec((1, 1, D_MODEL), wsel)
    grid_spec = pltpu.PrefetchScalarGridSpec(
        num_scalar_prefetch=len(plan),
        grid=(n_items,),
        in_specs=[pl.BlockSpec((tm, D_MODEL), rows), wspec, bspec, wspec, bspec, wspec, bspec],
        out_specs=pl.BlockSpec((tm, D_MODEL), rows),
        scratch_shapes=[pltpu.VMEM((D_MODEL, D_MODEL), BF16)] * 3,
    )
    return pl.pallas_call(
        _experts_kernel,
        grid_spec=grid_spec,
        out_shape=jax.ShapeDtypeStruct((p, D_MODEL), F32),
        compiler_params=_cparams("arbitrary"),
        name="experts",
    )(*plan, xs, w_gate, b_gate.reshape(N_EXPERTS, 1, D_MODEL), w_up, b_up.reshape(N_EXPERTS, 1, D_MODEL),
      w_down, b_down.reshape(N_EXPERTS, 1, D_MODEL))


def _expert_plan(counts, n_rows):
    tm = MOE_TM
    n_tiles = n_rows // tm
    n_items = n_tiles + N_EXPERTS - 1
    ends = jnp.cumsum(counts)
    starts = ends - counts
    first_tile = starts // tm
    last_tile = jnp.maximum(ends - 1, 0) // tm
    items = jnp.where(counts > 0, last_tile - first_tile + 1, 0)
    item_end = jnp.cumsum(items)
    item_start = item_end - items
    total = item_end[-1]
    w = jnp.arange(n_items, dtype=jnp.int32)
    wc = jnp.minimum(w, total - 1)
    exp = jnp.searchsorted(item_end, wc, side="right").astype(jnp.int32)
    tile = (first_tile[exp] + wc - item_start[exp]).astype(jnp.int32)
    lo = jnp.maximum(starts[exp], tile * tm) - tile * tm
    hi = jnp.minimum(ends[exp], (tile + 1) * tm) - tile * tm
    valid = w < total
    lo = jnp.where(valid, lo, 0).astype(jnp.int32)
    hi = jnp.where(valid, hi, 0).astype(jnp.int32)
    prev_tile = jnp.concatenate([jnp.full((1,), -1, jnp.int32), tile[:-1]])
    prev_exp = jnp.concatenate([jnp.full((1,), -1, jnp.int32), exp[:-1]])
    first = (valid & (tile != prev_tile)).astype(jnp.int32)
    fresh = (valid & (exp != prev_exp)).astype(jnp.int32)
    return tile, exp, lo, hi, first, fresh


def _combine_kernel(pos_ref, x1_ref, prob_ref, gfin_ref, ys_ref, y_ref, buf_ref, sem):
    tm = x1_ref.shape[0]

    def start(r, carry):
        for k in range(TOP_K):
            _row_copy(ys_ref.at[pl.ds(pos_ref[0, 0, r * TOP_K + k], 1)], buf_ref.at[k, pl.ds(r, 1)], sem).start()
        return carry

    lax.fori_loop(0, tm, start, 0)

    def wait(r, carry):
        for k in range(TOP_K):
            _row_copy(ys_ref.at[pl.ds(0, 1)], buf_ref.at[0, pl.ds(0, 1)], sem).wait()
        return carry

    lax.fori_loop(0, tm, wait, 0)

    prob = prob_ref[...]
    moe = jnp.zeros(x1_ref.shape, F32)
    for k in range(TOP_K):
        moe = moe + prob[:, k:k + 1] * buf_ref[k]
    x2 = x1_ref[...] + moe
    ms = jnp.mean(x2 * x2, axis=-1, keepdims=True)
    y_ref[...] = x2 * lax.rsqrt(ms + NORM_EPS) * gfin_ref[...]


def _combine(pos, x1, prob, g_final, ys):
    t = x1.shape[0]
    tm = ROW_TM
    pos3 = pos.reshape(t // tm, 1, tm * TOP_K)
    return pl.pallas_call(
        _combine_kernel,
        grid=(t // tm,),
        in_specs=[
            pl.BlockSpec((1, 1, tm * TOP_K), lambda i: (i, 0, 0), memory_space=pltpu.SMEM),
            pl.BlockSpec((tm, D_MODEL), lambda i: (i, 0)),
            pl.BlockSpec((tm, TOP_K), lambda i: (i, 0)),
            pl.BlockSpec((1, D_MODEL), lambda i: (0, 0)),
            pl.BlockSpec(memory_space=pl.ANY),
        ],
        out_specs=pl.BlockSpec((tm, D_MODEL), lambda i: (i, 0)),
        out_shape=jax.ShapeDtypeStruct((t, D_MODEL), F32),
        scratch_shapes=[pltpu.VMEM((TOP_K, tm, D_MODEL), F32), pltpu.SemaphoreType.DMA],
        compiler_params=_cparams("arbitrary"),
        name="combine",
    )(pos3, x1, prob, g_final, ys)


def _lambda_init(layer):
    return 0.8 - 0.6 * math.exp(-0.3 * layer)


def kernel(x_prompt, x_sample, cache_k, cache_v, state_rec, g_mix, w_in, lambda_q1, lambda_k1, lambda_q2, lambda_k2, g_subln, w_o_attn, lb_logits, g_rec_norm, w_o_rec, w_out, g_ffn, w_router, b_router, w_gate, b_gate, w_up, b_up, w_down, b_down, g_final):
    nb, seq, _ = x_prompt.shape
    nbd, ld, _ = x_sample.shape
    past = cache_k.shape[2]
    tp, ts = nb * seq, nbd * ld
    lam_init = _lambda_init(0)
    lam = (jnp.exp(jnp.sum(lambda_q1[0].astype(F32) * lambda_k1[0].astype(F32)))
           - jnp.exp(jnp.sum(lambda_q2[0].astype(F32) * lambda_k2[0].astype(F32))) + lam_init).reshape(1)

    w_in_b = w_in[0].astype(BF16)
    woa, wor, wout = w_o_attn[0].astype(BF16), w_o_rec[0].astype(BF16), w_out[0].astype(BF16)
    wr = w_router[0].astype(BF16)
    g_mix2, g_ffn2, g_fin2 = g_mix[0].reshape(1, -1), g_ffn[0].reshape(1, -1), g_final.reshape(1, -1)
    gs2, grn2 = g_subln[0].reshape(1, -1), g_rec_norm[0].reshape(1, -1)
    br2 = b_router[0].reshape(1, -1)

    cos_p, sin_p = _rope_tables(jnp.arange(seq, dtype=jnp.int32))
    pos_s = past + (jnp.arange(PROJ_TM, dtype=jnp.int32) % ld)
    cos_s, sin_s = _rope_tables(pos_s)

    xp = x_prompt.reshape(tp, D_MODEL)
    xs_in = x_sample.reshape(ts, D_MODEL)
    zp = _project(xp, g_mix2, w_in_b, cos_p, sin_p, lb_logits)
    zs = _project(xs_in, g_mix2, w_in_b, cos_s, sin_s, lb_logits)

    ao_p = _attn_prompt(lam, zp, nb, seq)
    ao_s = _attn_sample(lam, zs, cache_k[0], cache_v[0], nbd, ld)
    ro_p, st_p = _hgrn(zp, jnp.zeros((nb, HEADS, HEAD_W, HEAD_W), F32), nb, seq // CHUNK)
    ro_s, st_s = _hgrn(zs, state_rec[0], nbd, ld // CHUNK)

    merge_w = (gs2, grn2, woa, wor, wout, g_ffn2, wr, br2)
    zero_cnt = jnp.zeros((1, N_EXPERTS), F32)
    x1_p, xn_p, idx_p, prob_p, rank_p, cnt_p = _merge(lam_init, ao_p, ro_p, zp, xp, *merge_w, zero_cnt)
    x1_s, xn_s, idx_s, prob_s, rank_s, cnt = _merge(lam_init, ao_s, ro_s, zs, xs_in, *merge_w, cnt_p)

    counts = cnt[0].astype(jnp.int32)
    starts = jnp.cumsum(counts) - counts
    pos_p = starts[idx_p] + rank_p
    pos_s2 = starts[idx_s] + rank_s
    n_rows = (tp + ts) * TOP_K
    plan = _expert_plan(counts, n_rows)

    xs_sorted = _dispatch(jnp.concatenate([pos_p, pos_s2], axis=0), xn_p, xn_s)
    ys = _experts(plan, xs_sorted, w_gate[0], b_gate[0], w_up[0], b_up[0], w_down[0], b_down[0])
    y_p = _combine(pos_p, x1_p, prob_p, g_fin2, ys)
    y_s = _combine(pos_s2, x1_s, prob_s, g_fin2, ys)

    def col(z, c):
        return z[:, c * D_MODEL:(c + 1) * D_MODEL]

    return (y_p.reshape(nb, seq, D_MODEL), y_s.reshape(nbd, ld, D_MODEL),
            col(zp, COL_K).reshape(1, nb, seq, HEADS, HEAD_W), col(zp, COL_V).reshape(1, nb, seq, HEADS, HEAD_W),
            st_p[None],
            col(zs, COL_K).reshape(1, nbd, ld, HEADS, HEAD_W), col(zs, COL_V).reshape(1, nbd, ld, HEADS, HEAD_W),
            st_s[None])
```

```python
import functools
import math

import jax
import jax.numpy as jnp
from jax import lax
from jax.experimental import pallas as pl
from jax.experimental.pallas import tpu as pltpu

F32 = jnp.float32
BF16 = jnp.bfloat16

D_MODEL = 1024
CHUNK = 64
HEADS = 8
HEAD_W = 128
ATTN_HEAD_DIM = 64
ROPE_THETA = 10000.0
N_EXPERTS = 32
TOP_K = 4
SWIGLU_ALPHA = 1.702
SWIGLU_LIMIT = 7.0
NORM_EPS = 1e-6
N_COL_BLOCKS = 9
COL_Q, COL_K, COL_V, COL_F = range(4)
N_Z_BLOCKS = 7
ZC_Q, ZC_F, ZC_RQ, ZC_RI, ZC_RG, ZC_GA, ZC_GR = range(7)
PROJ_TM = 512

ATTN_TQ = 256
ATTN_TK = 1024
MERGE_TM = 512
MOE_TM = 256
ROW_TM = 256
VMEM_LIMIT = 56 * 1024 * 1024


def _cparams(*sem):
    return pltpu.CompilerParams(dimension_semantics=sem, vmem_limit_bytes=VMEM_LIMIT)


def _dot(a, b):
    return jnp.dot(a, b, preferred_element_type=F32)


def _dot_nt(a, b):
    return lax.dot_general(a, b, (((1,), (1,)), ((), ())), preferred_element_type=F32)


def _proj_kernel(x_ref, g_ref, w_ref, cos_ref, sin_ref, lbl_ref,
                 z_ref, kout_ref, vout_ref, kb_ref, vb_ref, h_ref):
    j = pl.program_id(1)
    tm = x_ref.shape[0]

    @pl.when(j == 0)
    def _():
        x = x_ref[...]
        ms = jnp.mean(x * x, axis=-1, keepdims=True)
        h_ref[...] = (x * lax.rsqrt(ms + NORM_EPS) * g_ref[...]).astype(BF16)

    z = _dot(h_ref[...], w_ref[...])

    def head(h):
        return slice(h * HEAD_W, (h + 1) * HEAD_W)

    def head_rows(h):
        return pl.ds(h, tm, stride=HEADS)

    def rope(zh):
        lane = lax.broadcasted_iota(jnp.int32, zh.shape, 1)
        first = (lane % ATTN_HEAD_DIM) < (ATTN_HEAD_DIM // 2)
        partner = jnp.where(first, pltpu.roll(zh, HEAD_W - ATTN_HEAD_DIM // 2, 1),
                            pltpu.roll(zh, ATTN_HEAD_DIM // 2, 1))
        return zh * cos_ref[...] + partner * sin_ref[...]

    @pl.when(j == COL_Q)
    def _():
        for h in range(HEADS):
            z_ref[:, head(h)] = rope(z[:, head(h)]) * (ATTN_HEAD_DIM ** -0.5)

    @pl.when(j == COL_K)
    def _():
        for h in range(HEADS):
            kh = rope(z[:, head(h)])
            kout_ref[head_rows(h), :] = kh
            kb_ref[:, head(h)] = kh.astype(BF16)

    @pl.when(j == COL_V)
    def _():
        for h in range(HEADS):
            vout_ref[head_rows(h), :] = z[:, head(h)]
        vb_ref[...] = z.astype(BF16)

    @pl.when(j == COL_F)
    def _():
        lbl = lbl_ref[...]
        e = jnp.exp(lbl - jnp.max(lbl, axis=0, keepdims=True))
        lb = e[0:1] / jnp.sum(e, axis=0, keepdims=True)
        z_ref[...] = lb + (1.0 - lb) * jax.nn.sigmoid(z)

    @pl.when(j > COL_F)
    def _():
        z_ref[...] = z


def _project(x, g, w_bf16, cos_t, sin_t, lb_logits):
    t = x.shape[0]
    tm = PROJ_TM
    n_tab = cos_t.shape[0] // tm
    tok = lambda i, j: (i, 0)
    return pl.pallas_call(
        _proj_kernel,
        grid=(t // tm, N_COL_BLOCKS),
        in_specs=[
            pl.BlockSpec((tm, D_MODEL), tok),
            pl.BlockSpec((1, D_MODEL), lambda i, j: (0, 0)),
            pl.BlockSpec((D_MODEL, D_MODEL), lambda i, j: (0, j)),
            pl.BlockSpec((tm, HEAD_W), lambda i, j: (i % n_tab, 0)),
            pl.BlockSpec((tm, HEAD_W), lambda i, j: (i % n_tab, 0)),
            pl.BlockSpec((2, D_MODEL), lambda i, j: (0, 0)),
        ],
        out_specs=[
            pl.BlockSpec((tm, D_MODEL), lambda i, j: (i, jnp.maximum(j - COL_V, 0))),
            pl.BlockSpec((tm * HEADS, HEAD_W), tok),
            pl.BlockSpec((tm * HEADS, HEAD_W), tok),
            pl.BlockSpec((tm, D_MODEL), tok),
            pl.BlockSpec((tm, D_MODEL), tok),
        ],
        out_shape=[
            jax.ShapeDtypeStruct((t, N_Z_BLOCKS * D_MODEL), F32),
            jax.ShapeDtypeStruct((t * HEADS, HEAD_W), F32),
            jax.ShapeDtypeStruct((t * HEADS, HEAD_W), F32),
            jax.ShapeDtypeStruct((t, D_MODEL), BF16),
            jax.ShapeDtypeStruct((t, D_MODEL), BF16),
        ],
        scratch_shapes=[pltpu.VMEM((tm, D_MODEL), BF16)],
        compiler_params=_cparams("arbitrary", "arbitrary"),
        name="proj",
    )(x, g, w_bf16, cos_t, sin_t, lb_logits)


def _rope_tables(pos):
    half = ATTN_HEAD_DIM // 2
    inv = jnp.power(ROPE_THETA, -2.0 * jnp.arange(half, dtype=F32) / ATTN_HEAD_DIM)
    ang = pos.astype(F32)[:, None] * inv[None, :]
    cos = jnp.tile(jnp.cos(ang), (1, HEAD_W // half))
    sin = jnp.sin(ang)
    sin = jnp.tile(jnp.concatenate([-sin, sin], axis=-1), (1, HEAD_W // ATTN_HEAD_DIM))
    return cos, sin


def _split_components(q):
    lane = lax.broadcasted_iota(jnp.int32, q.shape, 1)
    q1 = jnp.where(lane < ATTN_HEAD_DIM, q, 0.0)
    q2 = jnp.where(lane >= ATTN_HEAD_DIM, q, 0.0)
    return jnp.concatenate([q1, q2], axis=0).astype(BF16)


def _attn_prompt_kernel(lam_ref, q_ref, k_ref, v_ref, o_ref):
    lam = lam_ref[0]
    tq = ATTN_TQ
    seq = k_ref.shape[0]
    k = k_ref[...]
    v = v_ref[...]
    row = lax.broadcasted_iota(jnp.int32, (2 * tq, tq), 0)
    col = lax.broadcasted_iota(jnp.int32, (2 * tq, tq), 1)
    visible = (col // CHUNK) <= ((row % tq) // CHUNK)
    for qi in range(seq // tq):
        lo = qi * tq
        qq = _split_components(q_ref[lo:lo + tq, :])
        s_d = jnp.where(visible, _dot_nt(qq, k[lo:lo + tq]), -jnp.inf)
        m = jnp.max(s_d, axis=-1, keepdims=True)
        if qi > 0:
            s_m = _dot_nt(qq, k[:lo])
            m = jnp.maximum(m, jnp.max(s_m, axis=-1, keepdims=True))
            p_m = jnp.exp(s_m - m)
        p_d = jnp.exp(s_d - m)
        l = jnp.sum(p_d, axis=-1, keepdims=True)
        if qi > 0:
            l = l + jnp.sum(p_m, axis=-1, keepdims=True)
        r = 1.0 / l
        r1 = r[:tq]
        r2 = lam * r[tq:]
        w_d = (p_d[:tq] * r1 - p_d[tq:] * r2).astype(BF16)
        o = _dot(w_d, v[lo:lo + tq])
        if qi > 0:
            w_m = (p_m[:tq] * r1 - p_m[tq:] * r2).astype(BF16)
            o = o + _dot(w_m, v[:lo])
        o_ref[lo:lo + tq, :] = o


def _attn_prompt(lam, z, kb, vb, n_batch, seq):
    return pl.pallas_call(
        _attn_prompt_kernel,
        grid=(n_batch, HEADS),
        in_specs=[
            pl.BlockSpec(memory_space=pltpu.SMEM),
            pl.BlockSpec((seq, HEAD_W), lambda b, h: (b, ZC_Q * HEADS + h)),
            pl.BlockSpec((seq, HEAD_W), lambda b, h: (b, h)),
            pl.BlockSpec((seq, HEAD_W), lambda b, h: (b, h)),
        ],
        out_specs=pl.BlockSpec((seq, HEAD_W), lambda b, h: (b, h)),
        out_shape=jax.ShapeDtypeStruct((n_batch * seq, D_MODEL), F32),
        compiler_params=_cparams("arbitrary", "arbitrary"),
        name="attn_prompt",
    )(lam, z, kb, vb)


def _attn_sample_kernel(lam_ref, q_ref, kn_ref, vn_ref, ck_hbm, cv_hbm, o_ref, kbuf, vbuf, sem):
    b = pl.program_id(0)
    h = pl.program_id(1)
    step = b * HEADS + h
    slot = step % 2

    def cache_copies(bb, hh, sl):
        return (pltpu.make_async_copy(ck_hbm.at[bb, :, hh, :], kbuf.at[sl], sem.at[0, sl]),
                pltpu.make_async_copy(cv_hbm.at[bb, :, hh, :], vbuf.at[sl], sem.at[1, sl]))

    @pl.when(step == 0)
    def _():
        for c in cache_copies(b, h, slot):
            c.start()

    @pl.when(step + 1 < pl.num_programs(0) * HEADS)
    def _():
        nxt = step + 1
        for c in cache_copies(nxt // HEADS, nxt % HEADS, 1 - slot):
            c.start()

    for c in cache_copies(b, h, slot):
        c.wait()
    ck_ref = kbuf.at[slot]
    cv_ref = vbuf.at[slot]

    lam = lam_ref[0]
    n_q = q_ref.shape[0]
    qq = _split_components(q_ref[...])
    s_c = _dot_nt(qq, ck_ref[...].astype(BF16))
    s_n = _dot_nt(qq, kn_ref[...])
    m = jnp.maximum(jnp.max(s_c, axis=-1, keepdims=True), jnp.max(s_n, axis=-1, keepdims=True))
    p_c = jnp.exp(s_c - m)
    p_n = jnp.exp(s_n - m)
    r = 1.0 / (jnp.sum(p_c, axis=-1, keepdims=True) + jnp.sum(p_n, axis=-1, keepdims=True))
    r1 = r[:n_q]
    r2 = lam * r[n_q:]
    w_c = (p_c[:n_q] * r1 - p_c[n_q:] * r2).astype(BF16)
    w_n = (p_n[:n_q] * r1 - p_n[n_q:] * r2).astype(BF16)
    o_ref[...] = _dot(w_c, cv_ref[...].astype(BF16)) + _dot(w_n, vn_ref[...])


def _attn_sample(lam, z, kb, vb, cache_k, cache_v, n_batch, n_q):
    past = cache_k.shape[1]
    cache_spec = pl.BlockSpec(memory_space=pl.ANY)
    return pl.pallas_call(
        _attn_sample_kernel,
        grid=(n_batch, HEADS),
        in_specs=[
            pl.BlockSpec(memory_space=pltpu.SMEM),
            pl.BlockSpec((n_q, HEAD_W), lambda b, h: (b, ZC_Q * HEADS + h)),
            pl.BlockSpec((n_q, HEAD_W), lambda b, h: (b, h)),
            pl.BlockSpec((n_q, HEAD_W), lambda b, h: (b, h)),
            cache_spec, cache_spec,
        ],
        out_specs=pl.BlockSpec((n_q, HEAD_W), lambda b, h: (b, h)),
        out_shape=jax.ShapeDtypeStruct((n_batch * n_q, D_MODEL), F32),
        scratch_shapes=[
            pltpu.VMEM((2, past, HEAD_W), F32),
            pltpu.VMEM((2, past, HEAD_W), F32),
            pltpu.SemaphoreType.DMA((2, 2)),
        ],
        compiler_params=_cparams("arbitrary", "arbitrary"),
        name="attn_sample",
    )(lam, z, kb, vb, cache_k, cache_v)


def _split3(x):
    h1 = x.astype(BF16)
    r1 = x - h1.astype(F32)
    h2 = r1.astype(BF16)
    h3 = (r1 - h2.astype(F32)).astype(BF16)
    return h1, h2, h3


def _when_step(step, n_steps, which):
    def deco(body):
        if n_steps == 1:
            body()
        else:
            pl.when(step == which)(body)
    return deco


def _hgrn_kernel(n_chunks, f_ref, q_ref, i_ref, s0_ref, o_ref, sout_ref, st_ref):
    c = pl.program_id(1)

    @_when_step(c, n_chunks, 0)
    def _():
        for h in range(HEADS):
            st_ref[h] = s0_ref[0, h].T

    f = f_ref[...]
    logf = jnp.log(f)
    row = lax.broadcasted_iota(jnp.int32, (CHUNK, CHUNK), 0)
    col = lax.broadcasted_iota(jnp.int32, (CHUNK, CHUNK), 1)
    causal = row >= col
    tril = jnp.where(causal, 1.0, 0.0).astype(BF16)
    h1, h2, h3 = _split3(logf)
    b = _dot(tril, h1) + _dot(tril, h2) + _dot(tril, h3)
    b_last = b[CHUNK - 1:CHUNK, :]
    rk = 1.0 - f
    qd = q_ref[...] * jnp.exp(b)
    kd = rk * jnp.exp(-b)
    kl = rk * jnp.exp(b_last - b)
    eb_last = jnp.exp(b_last)
    v = i_ref[...]
    for h in range(HEADS):
        sl = slice(h * HEAD_W, (h + 1) * HEAD_W)
        qh = qd[:, sl].astype(BF16)
        vh = v[:, sl]
        a = jnp.where(causal, _dot_nt(qh, kd[:, sl].astype(BF16)), 0.0)
        st = st_ref[h]
        o_ref[:, sl] = _dot_nt(qh, st.astype(BF16)) + _dot(a.astype(BF16), vh.astype(BF16))
        st_ref[h] = st * eb_last[:, sl] + _dot(vh.T.astype(BF16), kl[:, sl].astype(BF16))

    @_when_step(c, n_chunks, n_chunks - 1)
    def _():
        for h in range(HEADS):
            sout_ref[0, h] = st_ref[h].T


def _hgrn(z, s0, n_batch, n_chunks):
    t = z.shape[0]
    return pl.pallas_call(
        functools.partial(_hgrn_kernel, n_chunks),
        grid=(n_batch, n_chunks),
        in_specs=[
            pl.BlockSpec((CHUNK, D_MODEL), lambda b, c: (b * n_chunks + c, ZC_F)),
            pl.BlockSpec((CHUNK, D_MODEL), lambda b, c: (b * n_chunks + c, ZC_RQ)),
            pl.BlockSpec((CHUNK, D_MODEL), lambda b, c: (b * n_chunks + c, ZC_RI)),
            pl.BlockSpec((1, HEADS, HEAD_W, HEAD_W), lambda b, c: (b, 0, 0, 0)),
        ],
        out_specs=[
            pl.BlockSpec((CHUNK, D_MODEL), lambda b, c: (b * n_chunks + c, 0)),
            pl.BlockSpec((1, HEADS, HEAD_W, HEAD_W), lambda b, c: (b, 0, 0, 0)),
        ],
        out_shape=[
            jax.ShapeDtypeStruct((t, D_MODEL), F32),
            jax.ShapeDtypeStruct((n_batch, HEADS, HEAD_W, HEAD_W), F32),
        ],
        scratch_shapes=[pltpu.VMEM((HEADS, HEAD_W, HEAD_W), F32)],
        compiler_params=_cparams("arbitrary", "arbitrary"),
        name="hgrn",
    )(z, z, z, s0)


def _head_rmsnorm(x, g):
    outs = []
    for h in range(HEADS):
        xh = x[:, h * HEAD_W:(h + 1) * HEAD_W]
        ms = jnp.mean(xh * xh, axis=-1, keepdims=True)
        outs.append(xh * lax.rsqrt(ms + NORM_EPS) * g)
    return jnp.concatenate(outs, axis=-1)


def _merge_kernel(lam_init, ao_ref, ro_ref, rg_ref, ga_ref, gr_ref, x_ref, gs_ref, grn_ref,
                  woa_ref, wor_ref, wout_ref, gffn_ref, wr_ref, br_ref, cnt_in_ref,
                  x1_ref, xn_ref, idx_ref, prob_ref, rank_ref, cnt_ref, run_ref):
    i = pl.program_id(0)
    tm = x_ref.shape[0]

    @pl.when(i == 0)
    def _():
        run_ref[...] = cnt_in_ref[...]

    an = (_head_rmsnorm(ao_ref[...], gs_ref[...]) * (1.0 - lam_init)).astype(BF16)
    rn = (_head_rmsnorm(ro_ref[...], grn_ref[...]) * jax.nn.silu(rg_ref[...])).astype(BF16)
    a = _dot(an, woa_ref[...])
    r = _dot(rn, wor_ref[...])
    mixed = jax.nn.sigmoid(ga_ref[...]) * a + jax.nn.sigmoid(gr_ref[...]) * r
    x1 = x_ref[...] + _dot(mixed.astype(BF16), wout_ref[...])
    x1_ref[...] = x1
    ms = jnp.mean(x1 * x1, axis=-1, keepdims=True)
    xn = x1 * lax.rsqrt(ms + NORM_EPS) * gffn_ref[...]
    xn_ref[...] = xn

    logits = _dot(xn.astype(BF16), wr_ref[...]) + br_ref[...]
    lane = lax.broadcasted_iota(jnp.int32, logits.shape, 1).astype(F32)
    sel = jnp.zeros(logits.shape, F32)
    work = logits
    tops, idxs = [], []
    for _ in range(TOP_K):
        m = jnp.max(work, axis=-1, keepdims=True)
        idx = jnp.min(jnp.where(work == m, lane, float(N_EXPERTS)), axis=-1, keepdims=True)
        hit = lane == idx
        sel = jnp.where(hit, 1.0, sel)
        work = jnp.where(hit, -jnp.inf, work)
        tops.append(m)
        idxs.append(idx)
    es = [jnp.exp(v - tops[0]) for v in tops]
    inv = 1.0 / (es[0] + es[1] + es[2] + es[3])

    trow = lax.broadcasted_iota(jnp.int32, (tm, tm), 0)
    tcol = lax.broadcasted_iota(jnp.int32, (tm, tm), 1)
    before = jnp.where(trow > tcol, 1.0, 0.0).astype(BF16)
    ranks = _dot(before, sel.astype(BF16)) + run_ref[...]
    k_lane = lax.broadcasted_iota(jnp.int32, (tm, TOP_K), 1)
    idx_out = jnp.zeros((tm, TOP_K), F32)
    prob_out = jnp.zeros((tm, TOP_K), F32)
    rank_out = jnp.zeros((tm, TOP_K), F32)
    for k in range(TOP_K):
        rk = jnp.sum(jnp.where(lane == idxs[k], ranks, 0.0), axis=-1, keepdims=True)
        idx_out = jnp.where(k_lane == k, idxs[k], idx_out)
        prob_out = jnp.where(k_lane == k, es[k] * inv, prob_out)
        rank_out = jnp.where(k_lane == k, rk, rank_out)
    idx_ref[...] = idx_out.astype(jnp.int32)
    prob_ref[...] = prob_out
    rank_ref[...] = rank_out.astype(jnp.int32)
    run = run_ref[...] + jnp.sum(sel, axis=0, keepdims=True)
    run_ref[...] = run
    cnt_ref[...] = run


def _merge(lam_init, ao, ro, z, x, g_subln, g_rec, woa, wor, wout, g_ffn, w_router, b_router, cnt_in):
    t = x.shape[0]
    tm = MERGE_TM
    row = lambda i: (i, 0)
    fixed = lambda i: (0, 0)
    tok = pl.BlockSpec((tm, D_MODEL), row)
    wspec = pl.BlockSpec((D_MODEL, D_MODEL), fixed)
    narrow = pl.BlockSpec((tm, TOP_K), row)
    return pl.pallas_call(
        functools.partial(_merge_kernel, lam_init),
        grid=(t // tm,),
        in_specs=[
            tok, tok,
            pl.BlockSpec((tm, D_MODEL), lambda i: (i, ZC_RG)),
            pl.BlockSpec((tm, D_MODEL), lambda i: (i, ZC_GA)),
            pl.BlockSpec((tm, D_MODEL), lambda i: (i, ZC_GR)),
            tok,
            pl.BlockSpec((1, HEAD_W), fixed), pl.BlockSpec((1, HEAD_W), fixed),
            wspec, wspec, wspec,
            pl.BlockSpec((1, D_MODEL), fixed),
            pl.BlockSpec((D_MODEL, N_EXPERTS), fixed),
            pl.BlockSpec((1, N_EXPERTS), fixed),
            pl.BlockSpec((1, N_EXPERTS), fixed),
        ],
        out_specs=[tok, tok, narrow, narrow, narrow, pl.BlockSpec((1, N_EXPERTS), fixed)],
        out_shape=[
            jax.ShapeDtypeStruct((t, D_MODEL), F32),
            jax.ShapeDtypeStruct((t, D_MODEL), F32),
            jax.ShapeDtypeStruct((t, TOP_K), jnp.int32),
            jax.ShapeDtypeStruct((t, TOP_K), F32),
            jax.ShapeDtypeStruct((t, TOP_K), jnp.int32),
            jax.ShapeDtypeStruct((1, N_EXPERTS), F32),
        ],
        scratch_shapes=[pltpu.VMEM((1, N_EXPERTS), F32)],
        compiler_params=_cparams("arbitrary"),
        name="merge",
    )(ao, ro, z, z, z, x, g_subln, g_rec, woa, wor, wout, g_ffn, w_router, b_router, cnt_in)


def _row_copy(src, dst, sem):
    return pltpu.make_async_copy(src, dst, sem)


def _dispatch_kernel(n_first, pos_ref, xa_ref, xb_ref, xs_ref, sem):
    i = pl.program_id(0)

    def scatter(x_ref):
        def start(r, carry):
            for k in range(TOP_K):
                _row_copy(x_ref.at[pl.ds(r, 1)], xs_ref.at[pl.ds(pos_ref[0, 0, r * TOP_K + k], 1)], sem).start()
            return carry

        lax.fori_loop(0, x_ref.shape[0], start, 0)

        def wait(r, carry):
            for k in range(TOP_K):
                _row_copy(x_ref.at[pl.ds(0, 1)], xs_ref.at[pl.ds(0, 1)], sem).wait()
            return carry

        lax.fori_loop(0, x_ref.shape[0], wait, 0)

    @pl.when(i < n_first)
    def _():
        scatter(xa_ref)

    @pl.when(i >= n_first)
    def _():
        scatter(xb_ref)


def _dispatch(pos, xn_a, xn_b):
    tm = ROW_TM
    n_a, n_b = xn_a.shape[0] // tm, xn_b.shape[0] // tm
    pos3 = pos.reshape(n_a + n_b, 1, tm * TOP_K)
    return pl.pallas_call(
        functools.partial(_dispatch_kernel, n_a),
        grid=(n_a + n_b,),
        in_specs=[
            pl.BlockSpec((1, 1, tm * TOP_K), lambda i: (i, 0, 0), memory_space=pltpu.SMEM),
            pl.BlockSpec((tm, D_MODEL), lambda i: (jnp.minimum(i, n_a - 1), 0)),
            pl.BlockSpec((tm, D_MODEL), lambda i: (jnp.maximum(i - n_a, 0), 0)),
        ],
        out_specs=pl.BlockSpec(memory_space=pl.ANY),
        out_shape=jax.ShapeDtypeStruct((pos.size, D_MODEL), F32),
        scratch_shapes=[pltpu.SemaphoreType.DMA],
        compiler_params=_cparams("arbitrary"),
        name="dispatch",
    )(pos3, xn_a, xn_b)


def _experts_kernel(tile_ref, exp_ref, lo_ref, hi_ref, first_ref, fresh_ref,
                    xs_ref, wg_ref, bg_ref, wu_ref, bu_ref, wd_ref, bd_ref, o_ref,
                    wg_s, wu_s, wd_s):
    w = pl.program_id(0)
    lo = lo_ref[w]
    hi = hi_ref[w]

    @pl.when(fresh_ref[w] == 1)
    def _():
        wg_s[...] = wg_ref[0].astype(BF16)
        wu_s[...] = wu_ref[0].astype(BF16)
        wd_s[...] = wd_ref[0].astype(BF16)

    @pl.when(hi > lo)
    def _():
        x = xs_ref[...].astype(BF16)
        gate = jnp.minimum(_dot(x, wg_s[...]) + bg_ref[0], SWIGLU_LIMIT)
        up = jnp.clip(_dot(x, wu_s[...]) + bu_ref[0], -SWIGLU_LIMIT, SWIGLU_LIMIT)
        act = gate * jax.nn.sigmoid(SWIGLU_ALPHA * gate) * (up + 1.0)
        out = _dot(act.astype(BF16), wd_s[...]) + bd_ref[0]
        row = lax.broadcasted_iota(jnp.int32, (o_ref.shape[0], 1), 0)
        mine = (row >= lo) & (row < hi)

        @pl.when(first_ref[w] == 1)
        def _():
            o_ref[...] = jnp.where(mine, out, 0.0)

        @pl.when(first_ref[w] == 0)
        def _():
            o_ref[...] = jnp.where(mine, out, o_ref[...])


def _experts(plan, xs, w_gate, b_gate, w_up, b_up, w_down, b_down):
    p = xs.shape[0]
    tm = MOE_TM
    n_items = plan[0].shape[0]
    rows = lambda w, tile, exp, lo, hi, first, fresh: (tile[w], 0)
    wsel = lambda w, tile, exp, lo, hi, first, fresh: (exp[w], 0, 0)
    wspec = pl.BlockSpec((1, D_MODEL, D_MODEL), wsel)
    bspec = pl.BlockSpec((1, 1, D_MODEL), wsel)
    grid_spec = pltpu.PrefetchScalarGridSpec(
        num_scalar_prefetch=len(plan),
        grid=(n_items,),
        in_specs=[pl.BlockSpec((tm, D_MODEL), rows), wspec, bspec, wspec, bspec, wspec, bspec],
        out_specs=pl.BlockSpec((tm, D_MODEL), rows),
        scratch_shapes=[pltpu.VMEM((D_MODEL, D_MODEL), BF16)] * 3,
    )
    return pl.pallas_call(
        _experts_kernel,
        grid_spec=grid_spec,
        out_shape=jax.ShapeDtypeStruct((p, D_MODEL), F32),
        compiler_params=_cparams("arbitrary"),
        name="experts",
    )(*plan, xs, w_gate, b_gate.reshape(N_EXPERTS, 1, D_MODEL), w_up, b_up.reshape(N_EXPERTS, 1, D_MODEL),
      w_down, b_down.reshape(N_EXPERTS, 1, D_MODEL))


def _expert_plan(counts, n_rows):
    tm = MOE_TM
    n_tiles = n_rows // tm
    n_items = n_tiles + N_EXPERTS - 1
    ends = jnp.cumsum(counts)
    starts = ends - counts
    first_tile = starts // tm
    last_tile = jnp.maximum(ends - 1, 0) // tm
    items = jnp.where(counts > 0, last_tile - first_tile + 1, 0)
    item_end = jnp.cumsum(items)
    item_start = item_end - items
    total = item_end[-1]
    w = jnp.arange(n_items, dtype=jnp.int32)
    wc = jnp.minimum(w, total - 1)
    exp = jnp.sum((item_end[None, :] <= wc[:, None]).astype(jnp.int32), axis=1)
    tile = (first_tile[exp] + wc - item_start[exp]).astype(jnp.int32)
    lo = jnp.maximum(starts[exp], tile * tm) - tile * tm
    hi = jnp.minimum(ends[exp], (tile + 1) * tm) - tile * tm
    valid = w < total
    lo = jnp.where(valid, lo, 0).astype(jnp.int32)
    hi = jnp.where(valid, hi, 0).astype(jnp.int32)
    prev_tile = jnp.concatenate([jnp.full((1,), -1, jnp.int32), tile[:-1]])
    prev_exp = jnp.concatenate([jnp.full((1,), -1, jnp.int32), exp[:-1]])
    first = (valid & (tile != prev_tile)).astype(jnp.int32)
    fresh = (valid & (exp != prev_exp)).astype(jnp.int32)
    return tile, exp, lo, hi, first, fresh


def _combine_kernel(pos_ref, x1_ref, prob_ref, gfin_ref, ys_ref, y_ref, buf_ref, sem):
    tm = x1_ref.shape[0]

    def start(r, carry):
        for k in range(TOP_K):
            _row_copy(ys_ref.at[pl.ds(pos_ref[0, 0, r * TOP_K + k], 1)], buf_ref.at[k, pl.ds(r, 1)], sem).start()
        return carry

    lax.fori_loop(0, tm, start, 0)

    def wait(r, carry):
        for k in range(TOP_K):
            _row_copy(ys_ref.at[pl.ds(0, 1)], buf_ref.at[0, pl.ds(0, 1)], sem).wait()
        return carry

    lax.fori_loop(0, tm, wait, 0)

    prob = prob_ref[...]
    moe = jnp.zeros(x1_ref.shape, F32)
    for k in range(TOP_K):
        moe = moe + prob[:, k:k + 1] * buf_ref[k]
    x2 = x1_ref[...] + moe
    ms = jnp.mean(x2 * x2, axis=-1, keepdims=True)
    y_ref[...] = x2 * lax.rsqrt(ms + NORM_EPS) * gfin_ref[...]


def _combine(pos, x1, prob, g_final, ys):
    t = x1.shape[0]
    tm = ROW_TM
    pos3 = pos.reshape(t // tm, 1, tm * TOP_K)
    return pl.pallas_call(
        _combine_kernel,
        grid=(t // tm,),
        in_specs=[
            pl.BlockSpec((1, 1, tm * TOP_K), lambda i: (i, 0, 0), memory_space=pltpu.SMEM),
            pl.BlockSpec((tm, D_MODEL), lambda i: (i, 0)),
            pl.BlockSpec((tm, TOP_K), lambda i: (i, 0)),
            pl.BlockSpec((1, D_MODEL), lambda i: (0, 0)),
            pl.BlockSpec(memory_space=pl.ANY),
        ],
        out_specs=pl.BlockSpec((tm, D_MODEL), lambda i: (i, 0)),
        out_shape=jax.ShapeDtypeStruct((t, D_MODEL), F32),
        scratch_shapes=[pltpu.VMEM((TOP_K, tm, D_MODEL), F32), pltpu.SemaphoreType.DMA],
        compiler_params=_cparams("arbitrary"),
        name="combine",
    )(pos3, x1, prob, g_final, ys)


def _lambda_init(layer):
    return 0.8 - 0.6 * math.exp(-0.3 * layer)


def kernel(x_prompt, x_sample, cache_k, cache_v, state_rec, g_mix, w_in, lambda_q1, lambda_k1, lambda_q2, lambda_k2, g_subln, w_o_attn, lb_logits, g_rec_norm, w_o_rec, w_out, g_ffn, w_router, b_router, w_gate, b_gate, w_up, b_up, w_down, b_down, g_final):
    nb, seq, _ = x_prompt.shape
    nbd, ld, _ = x_sample.shape
    past = cache_k.shape[2]
    tp, ts = nb * seq, nbd * ld
    lam_init = _lambda_init(0)
    lam = (jnp.exp(jnp.sum(lambda_q1[0].astype(F32) * lambda_k1[0].astype(F32)))
           - jnp.exp(jnp.sum(lambda_q2[0].astype(F32) * lambda_k2[0].astype(F32))) + lam_init).reshape(1)

    w_in_b = w_in[0].astype(BF16)
    woa, wor, wout = w_o_attn[0].astype(BF16), w_o_rec[0].astype(BF16), w_out[0].astype(BF16)
    wr = w_router[0].astype(BF16)
    g_mix2, g_ffn2, g_fin2 = g_mix[0].reshape(1, -1), g_ffn[0].reshape(1, -1), g_final.reshape(1, -1)
    gs2, grn2 = g_subln[0].reshape(1, -1), g_rec_norm[0].reshape(1, -1)
    br2 = b_router[0].reshape(1, -1)

    cos_p, sin_p = _rope_tables(jnp.arange(seq, dtype=jnp.int32))
    pos_s = past + (jnp.arange(PROJ_TM, dtype=jnp.int32) % ld)
    cos_s, sin_s = _rope_tables(pos_s)

    xp = x_prompt.reshape(tp, D_MODEL)
    xs_in = x_sample.reshape(ts, D_MODEL)
    zp, k_p, v_p, kb_p, vb_p = _project(xp, g_mix2, w_in_b, cos_p, sin_p, lb_logits)
    zs, k_s, v_s, kb_s, vb_s = _project(xs_in, g_mix2, w_in_b, cos_s, sin_s, lb_logits)

    ao_p = _attn_prompt(lam, zp, kb_p, vb_p, nb, seq)
    ao_s = _attn_sample(lam, zs, kb_s, vb_s, cache_k[0], cache_v[0], nbd, ld)
    ro_p, st_p = _hgrn(zp, jnp.zeros((nb, HEADS, HEAD_W, HEAD_W), F32), nb, seq // CHUNK)
    ro_s, st_s = _hgrn(zs, state_rec[0], nbd, ld // CHUNK)

    merge_w = (gs2, grn2, woa, wor, wout, g_ffn2, wr, br2)
    zero_cnt = jnp.zeros((1, N_EXPERTS), F32)
    x1_p, xn_p, idx_p, prob_p, rank_p, cnt_p = _merge(lam_init, ao_p, ro_p, zp, xp, *merge_w, zero_cnt)
    x1_s, xn_s, idx_s, prob_s, rank_s, cnt = _merge(lam_init, ao_s, ro_s, zs, xs_in, *merge_w, cnt_p)

    counts = cnt[0].astype(jnp.int32)
    starts = jnp.cumsum(counts) - counts
    experts = jnp.arange(N_EXPERTS, dtype=jnp.int32)

    def positions(idx, rank):
        return jnp.sum(jnp.where(idx[..., None] == experts, starts, 0), axis=-1) + rank

    pos_p = positions(idx_p, rank_p)
    pos_s2 = positions(idx_s, rank_s)
    n_rows = (tp + ts) * TOP_K
    plan = _expert_plan(counts, n_rows)

    xs_sorted = _dispatch(jnp.concatenate([pos_p, pos_s2], axis=0), xn_p, xn_s)
    ys = _experts(plan, xs_sorted, w_gate[0], b_gate[0], w_up[0], b_up[0], w_down[0], b_down[0])
    y_p = _combine(pos_p, x1_p, prob_p, g_fin2, ys)
    y_s = _combine(pos_s2, x1_s, prob_s, g_fin2, ys)

    return (y_p.reshape(nb, seq, D_MODEL), y_s.reshape(nbd, ld, D_MODEL),
            k_p.reshape(1, nb, seq, HEADS, HEAD_W), v_p.reshape(1, nb, seq, HEADS, HEAD_W), st_p[None],
            k_s.reshape(1, nbd, ld, HEADS, HEAD_W), v_s.reshape(1, nbd, ld, HEADS, HEAD_W), st_s[None])
```

```python
import functools
import math

import jax
import jax.numpy as jnp
from jax import lax
from jax.experimental import pallas as pl
from jax.experimental.pallas import tpu as pltpu

F32 = jnp.float32
BF16 = jnp.bfloat16

D_MODEL = 1024
CHUNK = 64
HEADS = 8
HEAD_W = 128
ATTN_HEAD_DIM = 64
ROPE_THETA = 10000.0
N_EXPERTS = 32
TOP_K = 4
SWIGLU_ALPHA = 1.702
SWIGLU_LIMIT = 7.0
NORM_EPS = 1e-6
N_COL_BLOCKS = 9
COL_Q, COL_K, COL_V, COL_F = range(4)
N_Z_BLOCKS = 7
ZC_Q, ZC_F, ZC_RQ, ZC_RI, ZC_RG, ZC_GA, ZC_GR = range(7)
PROJ_TM = 256

ATTN_TQ = 256
ATTN_TK = 1024
HGRN_GROUP_PROMPT = 2
HGRN_GROUP_SAMPLE = 4
MERGE_TM = 512
MOE_TM = 256
MOE_FF_SLICE = 512
ROW_TM = 256
ROW_COPY_UNROLL = 4
ROW_SUB = D_MODEL // HEAD_W
VMEM_LIMIT = 56 * 1024 * 1024


def _cparams(*sem):
    return pltpu.CompilerParams(dimension_semantics=sem, vmem_limit_bytes=VMEM_LIMIT)


def _dot(a, b):
    return jnp.dot(a, b, preferred_element_type=F32)


def _dot_nt(a, b):
    return lax.dot_general(a, b, (((1,), (1,)), ((), ())), preferred_element_type=F32)


def _proj_kernel(x_ref, g_ref, w_ref, cos_ref, sin_ref, lbl_ref,
                 z_ref, kout_ref, vout_ref, kb_ref, vb_ref):
    tm = x_ref.shape[0]
    x = x_ref[...]
    ms = jnp.mean(x * x, axis=-1, keepdims=True)
    hn = (x * lax.rsqrt(ms + NORM_EPS) * g_ref[...]).astype(BF16)

    def col(j):
        return _dot(hn, w_ref[:, j * D_MODEL:(j + 1) * D_MODEL])

    def zcols(c):
        return slice(c * D_MODEL, (c + 1) * D_MODEL)

    def head(h):
        return slice(h * HEAD_W, (h + 1) * HEAD_W)

    def head_rows(h):
        return pl.ds(h, tm, stride=HEADS)

    def rope(zh):
        lane = lax.broadcasted_iota(jnp.int32, zh.shape, 1)
        first = (lane % ATTN_HEAD_DIM) < (ATTN_HEAD_DIM // 2)
        partner = jnp.where(first, pltpu.roll(zh, HEAD_W - ATTN_HEAD_DIM // 2, 1),
                            pltpu.roll(zh, ATTN_HEAD_DIM // 2, 1))
        return zh * cos_ref[...] + partner * sin_ref[...]

    zq = col(COL_Q)
    for h in range(HEADS):
        z_ref[:, head(h)] = rope(zq[:, head(h)]) * (ATTN_HEAD_DIM ** -0.5)

    zk = col(COL_K)
    for h in range(HEADS):
        kh = rope(zk[:, head(h)])
        kout_ref[head_rows(h), :] = kh
        kb_ref[:, head(h)] = kh.astype(BF16)

    zv = col(COL_V)
    for h in range(HEADS):
        vout_ref[head_rows(h), :] = zv[:, head(h)]
    vb_ref[...] = zv.astype(BF16)

    lbl = lbl_ref[...]
    e = jnp.exp(lbl - jnp.max(lbl, axis=0, keepdims=True))
    lb = e[0:1] / jnp.sum(e, axis=0, keepdims=True)
    z_ref[:, zcols(ZC_F)] = lb + (1.0 - lb) * jax.nn.sigmoid(col(COL_F))

    for j in range(COL_F + 1, N_COL_BLOCKS):
        z_ref[:, zcols(j - COL_V)] = col(j)


def _project(x, g, w_bf16, cos_t, sin_t, lb_logits):
    t = x.shape[0]
    tm = PROJ_TM
    n_tab = cos_t.shape[0] // tm
    tok = lambda i: (i, 0)
    fixed = lambda i: (0, 0)
    return pl.pallas_call(
        _proj_kernel,
        grid=(t // tm,),
        in_specs=[
            pl.BlockSpec((tm, D_MODEL), tok),
            pl.BlockSpec((1, D_MODEL), fixed),
            pl.BlockSpec((D_MODEL, N_COL_BLOCKS * D_MODEL), fixed, pipeline_mode=pl.Buffered(1)),
            pl.BlockSpec((tm, HEAD_W), lambda i: (i % n_tab, 0)),
            pl.BlockSpec((tm, HEAD_W), lambda i: (i % n_tab, 0)),
            pl.BlockSpec((2, D_MODEL), fixed),
        ],
        out_specs=[
            pl.BlockSpec((tm, N_Z_BLOCKS * D_MODEL), tok),
            pl.BlockSpec((tm * HEADS, HEAD_W), tok),
            pl.BlockSpec((tm * HEADS, HEAD_W), tok),
            pl.BlockSpec((tm, D_MODEL), tok),
            pl.BlockSpec((tm, D_MODEL), tok),
        ],
        out_shape=[
            jax.ShapeDtypeStruct((t, N_Z_BLOCKS * D_MODEL), F32),
            jax.ShapeDtypeStruct((t * HEADS, HEAD_W), F32),
            jax.ShapeDtypeStruct((t * HEADS, HEAD_W), F32),
            jax.ShapeDtypeStruct((t, D_MODEL), BF16),
            jax.ShapeDtypeStruct((t, D_MODEL), BF16),
        ],
        compiler_params=_cparams("arbitrary"),
        name="proj",
    )(x, g, w_bf16, cos_t, sin_t, lb_logits)


def _rope_tables(pos):
    half = ATTN_HEAD_DIM // 2
    inv = jnp.power(ROPE_THETA, -2.0 * jnp.arange(half, dtype=F32) / ATTN_HEAD_DIM)
    ang = pos.astype(F32)[:, None] * inv[None, :]
    cos = jnp.tile(jnp.cos(ang), (1, HEAD_W // half))
    sin = jnp.sin(ang)
    sin = jnp.tile(jnp.concatenate([-sin, sin], axis=-1), (1, HEAD_W // ATTN_HEAD_DIM))
    return cos, sin


def _split_components(q):
    lane = lax.broadcasted_iota(jnp.int32, q.shape, 1)
    q1 = jnp.where(lane < ATTN_HEAD_DIM, q, 0.0)
    q2 = jnp.where(lane >= ATTN_HEAD_DIM, q, 0.0)
    return jnp.concatenate([q1, q2], axis=0).astype(BF16)


def _attn_prompt_kernel(lam_ref, q_ref, k_ref, v_ref, o_ref):
    lam = lam_ref[0]
    tq = ATTN_TQ
    seq = k_ref.shape[0]
    k = k_ref[...]
    v = v_ref[...]
    row = lax.broadcasted_iota(jnp.int32, (2 * tq, tq), 0)
    col = lax.broadcasted_iota(jnp.int32, (2 * tq, tq), 1)
    visible = (col // CHUNK) <= ((row % tq) // CHUNK)
    for qi in range(seq // tq):
        lo = qi * tq
        qq = _split_components(q_ref[lo:lo + tq, :])
        s_d = jnp.where(visible, _dot_nt(qq, k[lo:lo + tq]), -jnp.inf)
        m = jnp.max(s_d, axis=-1, keepdims=True)
        if qi > 0:
            s_m = _dot_nt(qq, k[:lo])
            m = jnp.maximum(m, jnp.max(s_m, axis=-1, keepdims=True))
            p_m = jnp.exp(s_m - m)
        p_d = jnp.exp(s_d - m)
        l = jnp.sum(p_d, axis=-1, keepdims=True)
        if qi > 0:
            l = l + jnp.sum(p_m, axis=-1, keepdims=True)
        r = 1.0 / l
        r1 = r[:tq]
        r2 = lam * r[tq:]
        w_d = (p_d[:tq] * r1 - p_d[tq:] * r2).astype(BF16)
        o = _dot(w_d, v[lo:lo + tq])
        if qi > 0:
            w_m = (p_m[:tq] * r1 - p_m[tq:] * r2).astype(BF16)
            o = o + _dot(w_m, v[:lo])
        o_ref[lo:lo + tq, :] = o


def _attn_prompt(lam, z, kb, vb, n_batch, seq):
    return pl.pallas_call(
        _attn_prompt_kernel,
        grid=(n_batch, HEADS),
        in_specs=[
            pl.BlockSpec(memory_space=pltpu.SMEM),
            pl.BlockSpec((seq, HEAD_W), lambda b, h: (b, ZC_Q * HEADS + h)),
            pl.BlockSpec((seq, HEAD_W), lambda b, h: (b, h)),
            pl.BlockSpec((seq, HEAD_W), lambda b, h: (b, h)),
        ],
        out_specs=pl.BlockSpec((seq, HEAD_W), lambda b, h: (b, h)),
        out_shape=jax.ShapeDtypeStruct((n_batch * seq, D_MODEL), F32),
        compiler_params=_cparams("arbitrary", "arbitrary"),
        name="attn_prompt",
    )(lam, z, kb, vb)


def _attn_sample_kernel(n_steps, lam_ref, q_ref, kn_ref, vn_ref, ck_hbm, cv_hbm, o_ref, kbuf, vbuf, sem):
    b = pl.program_id(0)
    h = pl.program_id(1)
    step = b * HEADS + h
    slot = step % 2

    def cache_copies(bb, hh, sl):
        return (pltpu.make_async_copy(ck_hbm.at[bb, :, hh, :], kbuf.at[sl], sem.at[0, sl]),
                pltpu.make_async_copy(cv_hbm.at[bb, :, hh, :], vbuf.at[sl], sem.at[1, sl]))

    @pl.when(step == 0)
    def _():
        for c in cache_copies(b, h, slot):
            c.start()

    @pl.when(step + 1 < n_steps)
    def _():
        nxt = step + 1
        for c in cache_copies(nxt // HEADS, nxt % HEADS, 1 - slot):
            c.start()

    for c in cache_copies(b, h, slot):
        c.wait()
    ck_ref = kbuf.at[slot]
    cv_ref = vbuf.at[slot]

    lam = lam_ref[0]
    n_q = q_ref.shape[0]
    qq = _split_components(q_ref[...])
    s_c = _dot_nt(qq, ck_ref[...].astype(BF16))
    s_n = _dot_nt(qq, kn_ref[...])
    m = jnp.maximum(jnp.max(s_c, axis=-1, keepdims=True), jnp.max(s_n, axis=-1, keepdims=True))
    p_c = jnp.exp(s_c - m)
    p_n = jnp.exp(s_n - m)
    r = 1.0 / (jnp.sum(p_c, axis=-1, keepdims=True) + jnp.sum(p_n, axis=-1, keepdims=True))
    r1 = r[:n_q]
    r2 = lam * r[n_q:]
    w_c = (p_c[:n_q] * r1 - p_c[n_q:] * r2).astype(BF16)
    w_n = (p_n[:n_q] * r1 - p_n[n_q:] * r2).astype(BF16)
    o_ref[...] = _dot(w_c, cv_ref[...].astype(BF16)) + _dot(w_n, vn_ref[...])


def _attn_sample(lam, z, kb, vb, cache_k, cache_v, n_batch, n_q):
    past = cache_k.shape[1]
    cache_spec = pl.BlockSpec(memory_space=pl.ANY)
    return pl.pallas_call(
        functools.partial(_attn_sample_kernel, n_batch * HEADS),
        grid=(n_batch, HEADS),
        in_specs=[
            pl.BlockSpec(memory_space=pltpu.SMEM),
            pl.BlockSpec((n_q, HEAD_W), lambda b, h: (b, ZC_Q * HEADS + h)),
            pl.BlockSpec((n_q, HEAD_W), lambda b, h: (b, h)),
            pl.BlockSpec((n_q, HEAD_W), lambda b, h: (b, h)),
            cache_spec, cache_spec,
        ],
        out_specs=pl.BlockSpec((n_q, HEAD_W), lambda b, h: (b, h)),
        out_shape=jax.ShapeDtypeStruct((n_batch * n_q, D_MODEL), F32),
        scratch_shapes=[
            pltpu.VMEM((2, past, HEAD_W), F32),
            pltpu.VMEM((2, past, HEAD_W), F32),
            pltpu.SemaphoreType.DMA((2, 2)),
        ],
        compiler_params=_cparams("arbitrary", "arbitrary"),
        name="attn_sample",
    )(lam, z, kb, vb, cache_k, cache_v)


def _split3(x):
    h1 = x.astype(BF16)
    r1 = x - h1.astype(F32)
    h2 = r1.astype(BF16)
    h3 = (r1 - h2.astype(F32)).astype(BF16)
    return h1, h2, h3


def _when_step(step, n_steps, which):
    def deco(body):
        if n_steps == 1:
            body()
        else:
            pl.when(step == which)(body)
    return deco


def _hgrn_kernel(n_chunks, f_ref, q_ref, i_ref, s0_ref, o_ref, sout_ref, st_ref):
    c = pl.program_id(1)
    n_group = f_ref.shape[0]

    @_when_step(c, n_chunks, 0)
    def _():
        for g in range(n_group):
            for h in range(HEADS):
                st_ref[g, h] = s0_ref[g, h].T

    row = lax.broadcasted_iota(jnp.int32, (CHUNK, CHUNK), 0)
    col = lax.broadcasted_iota(jnp.int32, (CHUNK, CHUNK), 1)
    causal = row >= col
    tril = jnp.where(causal, 1.0, 0.0).astype(BF16)
    for g in range(n_group):
        f = f_ref[g]
        h1, h2, h3 = _split3(jnp.log(f))
        b = _dot(tril, h1) + _dot(tril, h2) + _dot(tril, h3)
        b_last = b[CHUNK - 1:CHUNK, :]
        rk = 1.0 - f
        qd = q_ref[g] * jnp.exp(b)
        kd = rk * jnp.exp(-b)
        kl = rk * jnp.exp(b_last - b)
        eb_last = jnp.exp(b_last)
        v = i_ref[g]
        for h in range(HEADS):
            sl = slice(h * HEAD_W, (h + 1) * HEAD_W)
            qh = qd[:, sl].astype(BF16)
            vh = v[:, sl]
            a = jnp.where(causal, _dot_nt(qh, kd[:, sl].astype(BF16)), 0.0)
            st = st_ref[g, h]
            o_ref[g, :, sl] = _dot_nt(qh, st.astype(BF16)) + _dot(a.astype(BF16), vh.astype(BF16))
            st_ref[g, h] = st * eb_last[:, sl] + _dot(vh.T.astype(BF16), kl[:, sl].astype(BF16))

    @_when_step(c, n_chunks, n_chunks - 1)
    def _():
        for g in range(n_group):
            for h in range(HEADS):
                sout_ref[g, h] = st_ref[g, h].T


def _hgrn(z, s0, n_batch, n_chunks, n_group):
    z3 = z.reshape(n_batch, n_chunks * CHUNK, N_Z_BLOCKS * D_MODEL)

    def zblock(zc):
        return pl.BlockSpec((n_group, CHUNK, D_MODEL), lambda b, c: (b, c, zc))

    state = pl.BlockSpec((n_group, HEADS, HEAD_W, HEAD_W), lambda b, c: (b, 0, 0, 0))
    ro, st = pl.pallas_call(
        functools.partial(_hgrn_kernel, n_chunks),
        grid=(n_batch // n_group, n_chunks),
        in_specs=[zblock(ZC_F), zblock(ZC_RQ), zblock(ZC_RI), state],
        out_specs=[pl.BlockSpec((n_group, CHUNK, D_MODEL), lambda b, c: (b, c, 0)), state],
        out_shape=[
            jax.ShapeDtypeStruct((n_batch, n_chunks * CHUNK, D_MODEL), F32),
            jax.ShapeDtypeStruct((n_batch, HEADS, HEAD_W, HEAD_W), F32),
        ],
        scratch_shapes=[pltpu.VMEM((n_group, HEADS, HEAD_W, HEAD_W), F32)],
        compiler_params=_cparams("arbitrary", "arbitrary"),
        name="hgrn",
    )(z3, z3, z3, s0)
    return ro.reshape(n_batch * n_chunks * CHUNK, D_MODEL), st


def _head_rmsnorm(x, g):
    outs = []
    for h in range(HEADS):
        xh = x[:, h * HEAD_W:(h + 1) * HEAD_W]
        ms = jnp.mean(xh * xh, axis=-1, keepdims=True)
        outs.append(xh * lax.rsqrt(ms + NORM_EPS) * g)
    return jnp.concatenate(outs, axis=-1)


def _merge_kernel(lam_init, ao_ref, ro_ref, rg_ref, ga_ref, gr_ref, x_ref, gs_ref, grn_ref,
                  woa_ref, wor_ref, wout_ref, gffn_ref, wr_ref, br_ref, cnt_in_ref,
                  x1_ref, xn_ref, idx_ref, prob_ref, rank_ref, cnt_ref, run_ref):
    i = pl.program_id(0)
    tm = x_ref.shape[0]

    @pl.when(i == 0)
    def _():
        run_ref[...] = cnt_in_ref[...]

    an = (_head_rmsnorm(ao_ref[...], gs_ref[...]) * (1.0 - lam_init)).astype(BF16)
    rn = (_head_rmsnorm(ro_ref[...], grn_ref[...]) * jax.nn.silu(rg_ref[...])).astype(BF16)
    a = _dot(an, woa_ref[...])
    r = _dot(rn, wor_ref[...])
    mixed = jax.nn.sigmoid(ga_ref[...]) * a + jax.nn.sigmoid(gr_ref[...]) * r
    x1 = x_ref[...] + _dot(mixed.astype(BF16), wout_ref[...])
    x1_ref[...] = x1
    ms = jnp.mean(x1 * x1, axis=-1, keepdims=True)
    xn = x1 * lax.rsqrt(ms + NORM_EPS) * gffn_ref[...]
    for s in range(ROW_SUB):
        xn_ref[pl.ds(s, tm, stride=ROW_SUB), :] = xn[:, s * HEAD_W:(s + 1) * HEAD_W]

    logits = _dot(xn.astype(BF16), wr_ref[...]) + br_ref[...]
    lane = lax.broadcasted_iota(jnp.int32, logits.shape, 1).astype(F32)
    sel = jnp.zeros(logits.shape, F32)
    work = logits
    tops, idxs = [], []
    for _ in range(TOP_K):
        m = jnp.max(work, axis=-1, keepdims=True)
        idx = jnp.min(jnp.where(work == m, lane, float(N_EXPERTS)), axis=-1, keepdims=True)
        hit = lane == idx
        sel = jnp.where(hit, 1.0, sel)
        work = jnp.where(hit, -jnp.inf, work)
        tops.append(m)
        idxs.append(idx)
    es = [jnp.exp(v - tops[0]) for v in tops]
    inv = 1.0 / (es[0] + es[1] + es[2] + es[3])

    trow = lax.broadcasted_iota(jnp.int32, (tm, tm), 0)
    tcol = lax.broadcasted_iota(jnp.int32, (tm, tm), 1)
    before = jnp.where(trow > tcol, 1.0, 0.0).astype(BF16)
    ranks = _dot(before, sel.astype(BF16)) + run_ref[...]
    k_lane = lax.broadcasted_iota(jnp.int32, (tm, TOP_K), 1)
    idx_out = jnp.zeros((tm, TOP_K), F32)
    prob_out = jnp.zeros((tm, TOP_K), F32)
    rank_out = jnp.zeros((tm, TOP_K), F32)
    for k in range(TOP_K):
        rk = jnp.sum(jnp.where(lane == idxs[k], ranks, 0.0), axis=-1, keepdims=True)
        idx_out = jnp.where(k_lane == k, idxs[k], idx_out)
        prob_out = jnp.where(k_lane == k, es[k] * inv, prob_out)
        rank_out = jnp.where(k_lane == k, rk, rank_out)
    idx_ref[...] = idx_out.astype(jnp.int32)
    prob_ref[...] = prob_out
    rank_ref[...] = rank_out.astype(jnp.int32)
    run = run_ref[...] + jnp.sum(sel, axis=0, keepdims=True)
    run_ref[...] = run
    cnt_ref[...] = run


def _merge(lam_init, ao, ro, z, x, g_subln, g_rec, woa, wor, wout, g_ffn, w_router, b_router, cnt_in):
    t = x.shape[0]
    tm = MERGE_TM
    row = lambda i: (i, 0)
    fixed = lambda i: (0, 0)
    tok = pl.BlockSpec((tm, D_MODEL), row)
    wspec = pl.BlockSpec((D_MODEL, D_MODEL), fixed)
    narrow = pl.BlockSpec((tm, TOP_K), row)
    return pl.pallas_call(
        functools.partial(_merge_kernel, lam_init),
        grid=(t // tm,),
        in_specs=[
            tok, tok,
            pl.BlockSpec((tm, D_MODEL), lambda i: (i, ZC_RG)),
            pl.BlockSpec((tm, D_MODEL), lambda i: (i, ZC_GA)),
            pl.BlockSpec((tm, D_MODEL), lambda i: (i, ZC_GR)),
            tok,
            pl.BlockSpec((1, HEAD_W), fixed), pl.BlockSpec((1, HEAD_W), fixed),
            wspec, wspec, wspec,
            pl.BlockSpec((1, D_MODEL), fixed),
            pl.BlockSpec((D_MODEL, N_EXPERTS), fixed),
            pl.BlockSpec((1, N_EXPERTS), fixed),
            pl.BlockSpec((1, N_EXPERTS), fixed),
        ],
        out_specs=[tok, pl.BlockSpec((tm * ROW_SUB, HEAD_W), row), narrow, narrow, narrow,
                   pl.BlockSpec((1, N_EXPERTS), fixed)],
        out_shape=[
            jax.ShapeDtypeStruct((t, D_MODEL), F32),
            jax.ShapeDtypeStruct((t * ROW_SUB, HEAD_W), F32),
            jax.ShapeDtypeStruct((t, TOP_K), jnp.int32),
            jax.ShapeDtypeStruct((t, TOP_K), F32),
            jax.ShapeDtypeStruct((t, TOP_K), jnp.int32),
            jax.ShapeDtypeStruct((1, N_EXPERTS), F32),
        ],
        scratch_shapes=[pltpu.VMEM((1, N_EXPERTS), F32)],
        compiler_params=_cparams("arbitrary"),
        name="merge",
    )(ao, ro, z, z, z, x, g_subln, g_rec, woa, wor, wout, g_ffn, w_router, b_router, cnt_in)


def _row_copy(src, dst, sem):
    return pltpu.make_async_copy(src, dst, sem)


def _dispatch_kernel(n_first, n_steps, pos_ref, xa_hbm, xb_hbm, xs_hbm, sem):
    i = pl.program_id(0)
    tm = ROW_TM

    def row_tile(off):
        return pl.ds(pl.multiple_of(off, ROW_SUB), ROW_SUB)

    def scatter(x_hbm, base):
        def start(g, carry):
            r0 = g * ROW_COPY_UNROLL
            n = ROW_COPY_UNROLL * TOP_K
            dst = [pos_ref[0, 0, r0 * TOP_K + j] for j in range(n)]
            for j in range(n):
                _row_copy(x_hbm.at[row_tile((base + r0 + j // TOP_K) * ROW_SUB)],
                          xs_hbm.at[row_tile(dst[j])], sem).start()
            return carry

        lax.fori_loop(0, tm // ROW_COPY_UNROLL, start, 0)

    def drain():
        for k in range(TOP_K):
            _row_copy(xa_hbm.at[pl.ds(0, tm * ROW_SUB)], xs_hbm.at[pl.ds(0, tm * ROW_SUB)], sem).wait()

    @pl.when(i < n_first)
    def _():
        scatter(xa_hbm, i * tm)

    @pl.when(i >= n_first)
    def _():
        scatter(xb_hbm, (i - n_first) * tm)

    @pl.when(i > 0)
    def _():
        drain()

    @pl.when(i == n_steps - 1)
    def _():
        drain()


def _dispatch(pos, xn_a, xn_b):
    tm = ROW_TM
    n_a, n_b = xn_a.shape[0] // (tm * ROW_SUB), xn_b.shape[0] // (tm * ROW_SUB)
    pos3 = pos.reshape(n_a + n_b, 1, tm * TOP_K)
    return pl.pallas_call(
        functools.partial(_dispatch_kernel, n_a, n_a + n_b),
        grid=(n_a + n_b,),
        in_specs=[
            pl.BlockSpec((1, 1, tm * TOP_K), lambda i: (i, 0, 0), memory_space=pltpu.SMEM),
            pl.BlockSpec(memory_space=pl.ANY),
            pl.BlockSpec(memory_space=pl.ANY),
        ],
        out_specs=pl.BlockSpec(memory_space=pl.ANY),
        out_shape=jax.ShapeDtypeStruct((pos.size * ROW_SUB, HEAD_W), F32),
        scratch_shapes=[pltpu.SemaphoreType.DMA],
        compiler_params=_cparams("arbitrary"),
        name="dispatch",
    )(pos3, xn_a, xn_b)


def _experts_kernel(tile_ref, exp_ref, lo_ref, hi_ref, first_ref, fresh_ref,
                    xs_ref, wg_ref, bg_ref, wu_ref, bu_ref, wd_ref, bd_ref, o_ref,
                    wg_s, wu_s, wd_s, x_s):
    w = pl.program_id(0)
    lo = lo_ref[w]
    hi = hi_ref[w]
    tm = x_s.shape[0]

    def lanes(s):
        return slice(s * HEAD_W, (s + 1) * HEAD_W)

    def sub_rows(s):
        return pl.ds(s, tm, stride=ROW_SUB)

    @pl.when(fresh_ref[w] == 1)
    def _():
        wg_s[...] = wg_ref[0].astype(BF16)
        wu_s[...] = wu_ref[0].astype(BF16)
        wd_s[...] = wd_ref[0].astype(BF16)

    @pl.when(hi > lo)
    def _():
        for s in range(ROW_SUB):
            x_s[:, lanes(s)] = xs_ref[sub_rows(s), :].astype(BF16)
        x = x_s[...]
        out = bd_ref[0]
        for c in range(0, D_MODEL, MOE_FF_SLICE):
            ff = slice(c, c + MOE_FF_SLICE)
            gate = jnp.minimum(_dot(x, wg_s[:, ff]) + bg_ref[0, :, ff], SWIGLU_LIMIT)
            up = jnp.clip(_dot(x, wu_s[:, ff]) + bu_ref[0, :, ff], -SWIGLU_LIMIT, SWIGLU_LIMIT)
            act = gate * jax.nn.sigmoid(SWIGLU_ALPHA * gate) * (up + 1.0)
            out = out + _dot(act.astype(BF16), wd_s[ff, :])
        row = lax.broadcasted_iota(jnp.int32, (tm, 1), 0)
        mine = (row >= lo) & (row < hi)

        @pl.when(first_ref[w] == 1)
        def _():
            for s in range(ROW_SUB):
                o_ref[sub_rows(s), :] = jnp.where(mine, out[:, lanes(s)], 0.0)

        @pl.when(first_ref[w] == 0)
        def _():
            for s in range(ROW_SUB):
                o_ref[sub_rows(s), :] = jnp.where(mine, out[:, lanes(s)], o_ref[sub_rows(s), :])


def _experts(plan, xs, w_gate, b_gate, w_up, b_up, w_down, b_down):
    tm = MOE_TM
    n_items = plan[0].shape[0]
    rows = lambda w, tile, exp, lo, hi, first, fresh: (tile[w], 0)
    wsel = lambda w, tile, exp, lo, hi, first, fresh: (exp[w], 0, 0)
    wspec = pl.BlockSpec((1, D_MODEL, D_MODEL), wsel)
    bspec = pl.BlockSpec((1, 1, D_MODEL), wsel)
    grid_spec = pltpu.PrefetchScalarGridSpec(
        num_scalar_prefetch=len(plan),
        grid=(n_items,),
        in_specs=[pl.BlockSpec((tm * ROW_SUB, HEAD_W), rows), wspec, bspec, wspec, bspec, wspec, bspec],
        out_specs=pl.BlockSpec((tm * ROW_SUB, HEAD_W), rows),
        scratch_shapes=[pltpu.VMEM((D_MODEL, D_MODEL), BF16)] * 3 + [pltpu.VMEM((tm, D_MODEL), BF16)],
    )
    return pl.pallas_call(
        _experts_kernel,
        grid_spec=grid_spec,
        out_shape=jax.ShapeDtypeStruct(xs.shape, F32),
        compiler_params=_cparams("arbitrary"),
        name="experts",
    )(*plan, xs, w_gate, b_gate.reshape(N_EXPERTS, 1, D_MODEL), w_up, b_up.reshape(N_EXPERTS, 1, D_MODEL),
      w_down, b_down.reshape(N_EXPERTS, 1, D_MODEL))


def _expert_plan(counts, n_rows):
    tm = MOE_TM
    n_tiles = n_rows // tm
    n_items = n_tiles + N_EXPERTS - 1
    ends = jnp.cumsum(counts)
    starts = ends - counts
    first_tile = starts // tm
    last_tile = jnp.maximum(ends - 1, 0) // tm
    items = jnp.where(counts > 0, last_tile - first_tile + 1, 0)
    item_end = jnp.cumsum(items)
    item_start = item_end - items
    total = item_end[-1]
    w = jnp.arange(n_items, dtype=jnp.int32)
    wc = jnp.minimum(w, total - 1)
    exp = jnp.sum((item_end[None, :] <= wc[:, None]).astype(jnp.int32), axis=1)
    tile = (first_tile[exp] + wc - item_start[exp]).astype(jnp.int32)
    lo = jnp.maximum(starts[exp], tile * tm) - tile * tm
    hi = jnp.minimum(ends[exp], (tile + 1) * tm) - tile * tm
    valid = w < total
    lo = jnp.where(valid, lo, 0).astype(jnp.int32)
    hi = jnp.where(valid, hi, 0).astype(jnp.int32)
    prev_tile = jnp.concatenate([jnp.full((1,), -1, jnp.int32), tile[:-1]])
    prev_exp = jnp.concatenate([jnp.full((1,), -1, jnp.int32), exp[:-1]])
    first = (valid & (tile != prev_tile)).astype(jnp.int32)
    fresh = (valid & (exp != prev_exp)).astype(jnp.int32)
    return tile, exp, lo, hi, first, fresh


def _combine_kernel(n_steps, pos_ref, pos_next_ref, x1_ref, prob_ref, gfin_ref, ys_ref, y_ref, buf_ref, sem):
    i = pl.program_id(0)
    tm = x1_ref.shape[0]
    slot = i % 2

    def row_tile(off):
        return pl.ds(pl.multiple_of(off, ROW_SUB), ROW_SUB)

    def gather(p_ref, sl):
        def start(g, carry):
            r0 = g * ROW_COPY_UNROLL
            n = ROW_COPY_UNROLL * TOP_K
            src = [p_ref[0, 0, r0 * TOP_K + j] for j in range(n)]
            for j in range(n):
                _row_copy(ys_ref.at[row_tile(src[j])],
                          buf_ref.at[sl, j % TOP_K, row_tile((r0 + j // TOP_K) * ROW_SUB)], sem.at[sl]).start()
            return carry

        lax.fori_loop(0, tm // ROW_COPY_UNROLL, start, 0)

    @pl.when(i == 0)
    def _():
        gather(pos_ref, slot)

    @pl.when(i + 1 < n_steps)
    def _():
        gather(pos_next_ref, 1 - slot)

    for k in range(TOP_K):
        _row_copy(ys_ref.at[pl.ds(0, tm * ROW_SUB)], buf_ref.at[slot, k], sem.at[slot]).wait()

    prob = prob_ref[...]
    parts = []
    for s in range(ROW_SUB):
        part = x1_ref[:, s * HEAD_W:(s + 1) * HEAD_W]
        for k in range(TOP_K):
            part = part + prob[:, k:k + 1] * buf_ref[slot, k, pl.ds(s, tm, stride=ROW_SUB), :]
        parts.append(part)
    x2 = jnp.concatenate(parts, axis=-1)
    ms = jnp.mean(x2 * x2, axis=-1, keepdims=True)
    y_ref[...] = x2 * lax.rsqrt(ms + NORM_EPS) * gfin_ref[...]


def _combine(pos, x1, prob, g_final, ys):
    t = x1.shape[0]
    tm = ROW_TM
    n = t // tm
    pos3 = pos.reshape(n, 1, tm * TOP_K)
    return pl.pallas_call(
        functools.partial(_combine_kernel, n),
        grid=(n,),
        in_specs=[
            pl.BlockSpec((1, 1, tm * TOP_K), lambda i: (i, 0, 0), memory_space=pltpu.SMEM),
            pl.BlockSpec((1, 1, tm * TOP_K), lambda i: (jnp.minimum(i + 1, n - 1), 0, 0), memory_space=pltpu.SMEM),
            pl.BlockSpec((tm, D_MODEL), lambda i: (i, 0)),
            pl.BlockSpec((tm, TOP_K), lambda i: (i, 0)),
            pl.BlockSpec((1, D_MODEL), lambda i: (0, 0)),
            pl.BlockSpec(memory_space=pl.ANY),
        ],
        out_specs=pl.BlockSpec((tm, D_MODEL), lambda i: (i, 0)),
        out_shape=jax.ShapeDtypeStruct((t, D_MODEL), F32),
        scratch_shapes=[pltpu.VMEM((2, TOP_K, tm * ROW_SUB, HEAD_W), F32), pltpu.SemaphoreType.DMA((2,))],
        compiler_params=_cparams("arbitrary"),
        name="combine",
    )(pos3, pos3, x1, prob, g_final, ys)


def _lambda_init(layer):
    return 0.8 - 0.6 * math.exp(-0.3 * layer)


def kernel(x_prompt, x_sample, cache_k, cache_v, state_rec, g_mix, w_in, lambda_q1, lambda_k1, lambda_q2, lambda_k2, g_subln, w_o_attn, lb_logits, g_rec_norm, w_o_rec, w_out, g_ffn, w_router, b_router, w_gate, b_gate, w_up, b_up, w_down, b_down, g_final):
    nb, seq, _ = x_prompt.shape
    nbd, ld, _ = x_sample.shape
    past = cache_k.shape[2]
    tp, ts = nb * seq, nbd * ld
    lam_init = _lambda_init(0)
    lam = (jnp.exp(jnp.sum(lambda_q1[0].astype(F32) * lambda_k1[0].astype(F32)))
           - jnp.exp(jnp.sum(lambda_q2[0].astype(F32) * lambda_k2[0].astype(F32))) + lam_init).reshape(1)

    w_in_b = w_in[0].astype(BF16)
    woa, wor, wout = w_o_attn[0].astype(BF16), w_o_rec[0].astype(BF16), w_out[0].astype(BF16)
    wr = w_router[0].astype(BF16)
    g_mix2, g_ffn2, g_fin2 = g_mix[0].reshape(1, -1), g_ffn[0].reshape(1, -1), g_final.reshape(1, -1)
    gs2, grn2 = g_subln[0].reshape(1, -1), g_rec_norm[0].reshape(1, -1)
    br2 = b_router[0].reshape(1, -1)

    cos_p, sin_p = _rope_tables(jnp.arange(seq, dtype=jnp.int32))
    pos_s = past + (jnp.arange(PROJ_TM, dtype=jnp.int32) % ld)
    cos_s, sin_s = _rope_tables(pos_s)

    xp = x_prompt.reshape(tp, D_MODEL)
    xs_in = x_sample.reshape(ts, D_MODEL)
    zp, k_p, v_p, kb_p, vb_p = _project(xp, g_mix2, w_in_b, cos_p, sin_p, lb_logits)
    zs, k_s, v_s, kb_s, vb_s = _project(xs_in, g_mix2, w_in_b, cos_s, sin_s, lb_logits)

    ao_p = _attn_prompt(lam, zp, kb_p, vb_p, nb, seq)
    ao_s = _attn_sample(lam, zs, kb_s, vb_s, cache_k[0], cache_v[0], nbd, ld)
    ro_p, st_p = _hgrn(zp, jnp.zeros((nb, HEADS, HEAD_W, HEAD_W), F32), nb, seq // CHUNK, HGRN_GROUP_PROMPT)
    ro_s, st_s = _hgrn(zs, state_rec[0], nbd, ld // CHUNK, HGRN_GROUP_SAMPLE)

    merge_w = (gs2, grn2, woa, wor, wout, g_ffn2, wr, br2)
    zero_cnt = jnp.zeros((1, N_EXPERTS), F32)
    x1_p, xn_p, idx_p, prob_p, rank_p, cnt_p = _merge(lam_init, ao_p, ro_p, zp, xp, *merge_w, zero_cnt)
    x1_s, xn_s, idx_s, prob_s, rank_s, cnt = _merge(lam_init, ao_s, ro_s, zs, xs_in, *merge_w, cnt_p)

    counts = cnt[0].astype(jnp.int32)
    starts = jnp.cumsum(counts) - counts
    experts = jnp.arange(N_EXPERTS, dtype=jnp.int32)

    def positions(idx, rank):
        return (jnp.sum(jnp.where(idx[..., None] == experts, starts, 0), axis=-1) + rank) * ROW_SUB

    pos_p = positions(idx_p, rank_p)
    pos_s2 = positions(idx_s, rank_s)
    n_rows = (tp + ts) * TOP_K
    plan = _expert_plan(counts, n_rows)

    xs_sorted = _dispatch(jnp.concatenate([pos_p, pos_s2], axis=0), xn_p, xn_s)
    ys = _experts(plan, xs_sorted, w_gate[0], b_gate[0], w_up[0], b_up[0], w_down[0], b_down[0])
    y_p = _combine(pos_p, x1_p, prob_p, g_fin2, ys)
    y_s = _combine(pos_s2, x1_s, prob_s, g_fin2, ys)

    return (y_p.reshape(nb, seq, D_MODEL), y_s.reshape(nbd, ld, D_MODEL),
            k_p.reshape(1, nb, seq, HEADS, HEAD_W), v_p.reshape(1, nb, seq, HEADS, HEAD_W), st_p[None],
            k_s.reshape(1, nbd, ld, HEADS, HEAD_W), v_s.reshape(1, nbd, ld, HEADS, HEAD_W), st_s[None])
```

```python
import functools
import math

import jax
import jax.numpy as jnp
from jax import lax
from jax.experimental import pallas as pl
from jax.experimental.pallas import tpu as pltpu

F32 = jnp.float32
BF16 = jnp.bfloat16

D_MODEL = 1024
CHUNK = 64
HEADS = 8
HEAD_W = 128
ATTN_HEAD_DIM = 64
ROPE_THETA = 10000.0
N_EXPERTS = 32
TOP_K = 4
SWIGLU_ALPHA = 1.702
SWIGLU_LIMIT = 7.0
NORM_EPS = 1e-6
N_COL_BLOCKS = 9
COL_Q, COL_K, COL_V, COL_F = range(4)
N_Z_BLOCKS = 7
ZC_Q, ZC_F, ZC_RQ, ZC_RI, ZC_RG, ZC_GA, ZC_GR = range(7)
PROJ_TM = 256

ATTN_TQ = 256
ATTN_TK = 1024
HGRN_GROUP_PROMPT = 2
HGRN_GROUP_SAMPLE = 4
MERGE_TM = 512
MOE_TM = 256
MOE_FF_SLICE = 256
ROW_TM = 256
ROW_COPY_UNROLL = 4
DISPATCH_SLOTS = 3
ROW_SUB = D_MODEL // HEAD_W
VMEM_LIMIT = 56 * 1024 * 1024


def _cparams(*sem):
    return pltpu.CompilerParams(dimension_semantics=sem, vmem_limit_bytes=VMEM_LIMIT)


def _dot(a, b):
    return jnp.dot(a, b, preferred_element_type=F32)


def _dot_nt(a, b):
    return lax.dot_general(a, b, (((1,), (1,)), ((), ())), preferred_element_type=F32)


def _proj_kernel(x_ref, g_ref, w_ref, cos_ref, sin_ref, lbl_ref,
                 z_ref, kout_ref, vout_ref, kb_ref, vb_ref):
    tm = x_ref.shape[0]
    x = x_ref[...]
    ms = jnp.mean(x * x, axis=-1, keepdims=True)
    hn = (x * lax.rsqrt(ms + NORM_EPS) * g_ref[...]).astype(BF16)

    def col(j):
        return _dot(hn, w_ref[:, j * D_MODEL:(j + 1) * D_MODEL])

    def zcols(c):
        return slice(c * D_MODEL, (c + 1) * D_MODEL)

    def head(h):
        return slice(h * HEAD_W, (h + 1) * HEAD_W)

    def head_rows(h):
        return pl.ds(h, tm, stride=HEADS)

    def rope(zh):
        lane = lax.broadcasted_iota(jnp.int32, zh.shape, 1)
        first = (lane % ATTN_HEAD_DIM) < (ATTN_HEAD_DIM // 2)
        partner = jnp.where(first, pltpu.roll(zh, HEAD_W - ATTN_HEAD_DIM // 2, 1),
                            pltpu.roll(zh, ATTN_HEAD_DIM // 2, 1))
        return zh * cos_ref[...] + partner * sin_ref[...]

    zq = col(COL_Q)
    for h in range(HEADS):
        z_ref[:, head(h)] = rope(zq[:, head(h)]) * (ATTN_HEAD_DIM ** -0.5)

    zk = col(COL_K)
    for h in range(HEADS):
        kh = rope(zk[:, head(h)])
        kout_ref[head_rows(h), :] = kh
        kb_ref[:, head(h)] = kh.astype(BF16)

    zv = col(COL_V)
    for h in range(HEADS):
        vout_ref[head_rows(h), :] = zv[:, head(h)]
    vb_ref[...] = zv.astype(BF16)

    lbl = lbl_ref[...]
    e = jnp.exp(lbl - jnp.max(lbl, axis=0, keepdims=True))
    lb = e[0:1] / jnp.sum(e, axis=0, keepdims=True)
    z_ref[:, zcols(ZC_F)] = lb + (1.0 - lb) * jax.nn.sigmoid(col(COL_F))

    for j in range(COL_F + 1, N_COL_BLOCKS):
        z_ref[:, zcols(j - COL_V)] = col(j)


def _project(x, g, w_bf16, cos_t, sin_t, lb_logits):
    t = x.shape[0]
    tm = PROJ_TM
    n_tab = cos_t.shape[0] // tm
    tok = lambda i: (i, 0)
    fixed = lambda i: (0, 0)
    return pl.pallas_call(
        _proj_kernel,
        grid=(t // tm,),
        in_specs=[
            pl.BlockSpec((tm, D_MODEL), tok),
            pl.BlockSpec((1, D_MODEL), fixed),
            pl.BlockSpec((D_MODEL, N_COL_BLOCKS * D_MODEL), fixed, pipeline_mode=pl.Buffered(1)),
            pl.BlockSpec((tm, HEAD_W), lambda i: (i % n_tab, 0)),
            pl.BlockSpec((tm, HEAD_W), lambda i: (i % n_tab, 0)),
            pl.BlockSpec((2, D_MODEL), fixed),
        ],
        out_specs=[
            pl.BlockSpec((tm, N_Z_BLOCKS * D_MODEL), tok),
            pl.BlockSpec((tm * HEADS, HEAD_W), tok),
            pl.BlockSpec((tm * HEADS, HEAD_W), tok),
            pl.BlockSpec((tm, D_MODEL), tok),
            pl.BlockSpec((tm, D_MODEL), tok),
        ],
        out_shape=[
            jax.ShapeDtypeStruct((t, N_Z_BLOCKS * D_MODEL), F32),
            jax.ShapeDtypeStruct((t * HEADS, HEAD_W), F32),
            jax.ShapeDtypeStruct((t * HEADS, HEAD_W), F32),
            jax.ShapeDtypeStruct((t, D_MODEL), BF16),
            jax.ShapeDtypeStruct((t, D_MODEL), BF16),
        ],
        compiler_params=_cparams("arbitrary"),
        name="proj",
    )(x, g, w_bf16, cos_t, sin_t, lb_logits)


def _rope_tables(pos):
    half = ATTN_HEAD_DIM // 2
    inv = jnp.power(ROPE_THETA, -2.0 * jnp.arange(half, dtype=F32) / ATTN_HEAD_DIM)
    ang = pos.astype(F32)[:, None] * inv[None, :]
    cos = jnp.tile(jnp.cos(ang), (1, HEAD_W // half))
    sin = jnp.sin(ang)
    sin = jnp.tile(jnp.concatenate([-sin, sin], axis=-1), (1, HEAD_W // ATTN_HEAD_DIM))
    return cos, sin


def _split_components(q):
    lane = lax.broadcasted_iota(jnp.int32, q.shape, 1)
    q1 = jnp.where(lane < ATTN_HEAD_DIM, q, 0.0)
    q2 = jnp.where(lane >= ATTN_HEAD_DIM, q, 0.0)
    return jnp.concatenate([q1, q2], axis=0).astype(BF16)


def _attn_prompt_kernel(lam_ref, q_ref, k_ref, v_ref, o_ref):
    lam = lam_ref[0]
    tq = ATTN_TQ
    seq = k_ref.shape[0]
    k = k_ref[...]
    v = v_ref[...]
    row = lax.broadcasted_iota(jnp.int32, (2 * tq, tq), 0)
    col = lax.broadcasted_iota(jnp.int32, (2 * tq, tq), 1)
    visible = (col // CHUNK) <= ((row % tq) // CHUNK)
    n_tiles = seq // tq

    def scores(qi):
        lo = qi * tq
        qq = _split_components(q_ref[lo:lo + tq, :])
        s_d = jnp.where(visible, _dot_nt(qq, k[lo:lo + tq]), -jnp.inf)
        return s_d, (_dot_nt(qq, k[:lo]) if qi > 0 else None)

    def weighted_values(qi, w_d, w_m):
        lo = qi * tq
        o = _dot(w_d, v[lo:lo + tq])
        if qi > 0:
            o = o + _dot(w_m, v[:lo])
        o_ref[lo:lo + tq, :] = o

    ahead = scores(0)
    behind = None
    for qi in range(n_tiles):
        s_d, s_m = ahead
        if qi + 1 < n_tiles:
            ahead = scores(qi + 1)
        if behind is not None:
            weighted_values(*behind)
        m = jnp.max(s_d, axis=-1, keepdims=True)
        if qi > 0:
            m = jnp.maximum(m, jnp.max(s_m, axis=-1, keepdims=True))
            p_m = jnp.exp(s_m - m)
        p_d = jnp.exp(s_d - m)
        l = jnp.sum(p_d, axis=-1, keepdims=True)
        if qi > 0:
            l = l + jnp.sum(p_m, axis=-1, keepdims=True)
        r = 1.0 / l
        r1 = r[:tq]
        r2 = lam * r[tq:]
        w_d = (p_d[:tq] * r1 - p_d[tq:] * r2).astype(BF16)
        w_m = (p_m[:tq] * r1 - p_m[tq:] * r2).astype(BF16) if qi > 0 else None
        behind = (qi, w_d, w_m)
    weighted_values(*behind)


def _attn_prompt(lam, z, kb, vb, n_batch, seq):
    return pl.pallas_call(
        _attn_prompt_kernel,
        grid=(n_batch, HEADS),
        in_specs=[
            pl.BlockSpec(memory_space=pltpu.SMEM),
            pl.BlockSpec((seq, HEAD_W), lambda b, h: (b, ZC_Q * HEADS + h)),
            pl.BlockSpec((seq, HEAD_W), lambda b, h: (b, h)),
            pl.BlockSpec((seq, HEAD_W), lambda b, h: (b, h)),
        ],
        out_specs=pl.BlockSpec((seq, HEAD_W), lambda b, h: (b, h)),
        out_shape=jax.ShapeDtypeStruct((n_batch * seq, D_MODEL), F32),
        compiler_params=_cparams("arbitrary", "arbitrary"),
        name="attn_prompt",
    )(lam, z, kb, vb)


def _attn_sample_kernel(n_steps, lam_ref, q_ref, kn_ref, vn_ref, ck_hbm, cv_hbm, o_ref, kbuf, vbuf, sem):
    b = pl.program_id(0)
    h = pl.program_id(1)
    step = b * HEADS + h
    slot = step % 2

    def cache_copies(bb, hh, sl):
        return (pltpu.make_async_copy(ck_hbm.at[bb, :, hh, :], kbuf.at[sl], sem.at[0, sl]),
                pltpu.make_async_copy(cv_hbm.at[bb, :, hh, :], vbuf.at[sl], sem.at[1, sl]))

    @pl.when(step == 0)
    def _():
        for c in cache_copies(b, h, slot):
            c.start()

    @pl.when(step + 1 < n_steps)
    def _():
        nxt = step + 1
        for c in cache_copies(nxt // HEADS, nxt % HEADS, 1 - slot):
            c.start()

    for c in cache_copies(b, h, slot):
        c.wait()
    ck_ref = kbuf.at[slot]
    cv_ref = vbuf.at[slot]

    lam = lam_ref[0]
    n_q = q_ref.shape[0]
    qq = _split_components(q_ref[...])
    s_c = _dot_nt(qq, ck_ref[...].astype(BF16))
    s_n = _dot_nt(qq, kn_ref[...])
    m = jnp.maximum(jnp.max(s_c, axis=-1, keepdims=True), jnp.max(s_n, axis=-1, keepdims=True))
    p_c = jnp.exp(s_c - m)
    p_n = jnp.exp(s_n - m)
    r = 1.0 / (jnp.sum(p_c, axis=-1, keepdims=True) + jnp.sum(p_n, axis=-1, keepdims=True))
    r1 = r[:n_q]
    r2 = lam * r[n_q:]
    w_c = (p_c[:n_q] * r1 - p_c[n_q:] * r2).astype(BF16)
    w_n = (p_n[:n_q] * r1 - p_n[n_q:] * r2).astype(BF16)
    o_ref[...] = _dot(w_c, cv_ref[...].astype(BF16)) + _dot(w_n, vn_ref[...])


def _attn_sample(lam, z, kb, vb, cache_k, cache_v, n_batch, n_q):
    past = cache_k.shape[1]
    cache_spec = pl.BlockSpec(memory_space=pl.ANY)
    return pl.pallas_call(
        functools.partial(_attn_sample_kernel, n_batch * HEADS),
        grid=(n_batch, HEADS),
        in_specs=[
            pl.BlockSpec(memory_space=pltpu.SMEM),
            pl.BlockSpec((n_q, HEAD_W), lambda b, h: (b, ZC_Q * HEADS + h)),
            pl.BlockSpec((n_q, HEAD_W), lambda b, h: (b, h)),
            pl.BlockSpec((n_q, HEAD_W), lambda b, h: (b, h)),
            cache_spec, cache_spec,
        ],
        out_specs=pl.BlockSpec((n_q, HEAD_W), lambda b, h: (b, h)),
        out_shape=jax.ShapeDtypeStruct((n_batch * n_q, D_MODEL), F32),
        scratch_shapes=[
            pltpu.VMEM((2, past, HEAD_W), F32),
            pltpu.VMEM((2, past, HEAD_W), F32),
            pltpu.SemaphoreType.DMA((2, 2)),
        ],
        compiler_params=_cparams("arbitrary", "arbitrary"),
        name="attn_sample",
    )(lam, z, kb, vb, cache_k, cache_v)


def _split3(x):
    h1 = x.astype(BF16)
    r1 = x - h1.astype(F32)
    h2 = r1.astype(BF16)
    h3 = (r1 - h2.astype(F32)).astype(BF16)
    return h1, h2, h3


def _when_step(step, n_steps, which):
    def deco(body):
        if n_steps == 1:
            body()
        else:
            pl.when(step == which)(body)
    return deco


def _hgrn_kernel(n_chunks, f_ref, q_ref, i_ref, s0_ref, o_ref, sout_ref, st_ref):
    c = pl.program_id(1)
    n_group = f_ref.shape[0]

    @_when_step(c, n_chunks, 0)
    def _():
        for g in range(n_group):
            for h in range(HEADS):
                st_ref[g, h] = s0_ref[g, h].T

    row = lax.broadcasted_iota(jnp.int32, (CHUNK, CHUNK), 0)
    col = lax.broadcasted_iota(jnp.int32, (CHUNK, CHUNK), 1)
    causal = row >= col
    tril = jnp.where(causal, 1.0, 0.0).astype(BF16)
    for g in range(n_group):
        f = f_ref[g]
        h1, h2, h3 = _split3(jnp.log(f))
        b = _dot(tril, h1) + _dot(tril, h2) + _dot(tril, h3)
        b_last = b[CHUNK - 1:CHUNK, :]
        rk = 1.0 - f
        qd = q_ref[g] * jnp.exp(b)
        kd = rk * jnp.exp(-b)
        kl = rk * jnp.exp(b_last - b)
        eb_last = jnp.exp(b_last)
        v = i_ref[g]
        for h in range(HEADS):
            sl = slice(h * HEAD_W, (h + 1) * HEAD_W)
            qh = qd[:, sl].astype(BF16)
            vh = v[:, sl]
            a = jnp.where(causal, _dot_nt(qh, kd[:, sl].astype(BF16)), 0.0)
            st = st_ref[g, h]
            o_ref[g, :, sl] = _dot_nt(qh, st.astype(BF16)) + _dot(a.astype(BF16), vh.astype(BF16))
            st_ref[g, h] = st * eb_last[:, sl] + _dot(vh.T.astype(BF16), kl[:, sl].astype(BF16))

    @_when_step(c, n_chunks, n_chunks - 1)
    def _():
        for g in range(n_group):
            for h in range(HEADS):
                sout_ref[g, h] = st_ref[g, h].T


def _hgrn(z, s0, n_batch, n_chunks, n_group):
    z3 = z.reshape(n_batch, n_chunks * CHUNK, N_Z_BLOCKS * D_MODEL)

    def zblock(zc):
        return pl.BlockSpec((n_group, CHUNK, D_MODEL), lambda b, c: (b, c, zc))

    state = pl.BlockSpec((n_group, HEADS, HEAD_W, HEAD_W), lambda b, c: (b, 0, 0, 0))
    ro, st = pl.pallas_call(
        functools.partial(_hgrn_kernel, n_chunks),
        grid=(n_batch // n_group, n_chunks),
        in_specs=[zblock(ZC_F), zblock(ZC_RQ), zblock(ZC_RI), state],
        out_specs=[pl.BlockSpec((n_group, CHUNK, D_MODEL), lambda b, c: (b, c, 0)), state],
        out_shape=[
            jax.ShapeDtypeStruct((n_batch, n_chunks * CHUNK, D_MODEL), F32),
            jax.ShapeDtypeStruct((n_batch, HEADS, HEAD_W, HEAD_W), F32),
        ],
        scratch_shapes=[pltpu.VMEM((n_group, HEADS, HEAD_W, HEAD_W), F32)],
        compiler_params=_cparams("arbitrary", "arbitrary"),
        name="hgrn",
    )(z3, z3, z3, s0)
    return ro.reshape(n_batch * n_chunks * CHUNK, D_MODEL), st


def _head_rmsnorm(x, g):
    outs = []
    for h in range(HEADS):
        xh = x[:, h * HEAD_W:(h + 1) * HEAD_W]
        ms = jnp.mean(xh * xh, axis=-1, keepdims=True)
        outs.append(xh * lax.rsqrt(ms + NORM_EPS) * g)
    return jnp.concatenate(outs, axis=-1)


def _merge_kernel(lam_init, ao_ref, ro_ref, rg_ref, ga_ref, gr_ref, x_ref, gs_ref, grn_ref,
                  woa_ref, wor_ref, wout_ref, gffn_ref, wr_ref, br_ref, cnt_in_ref,
                  x1_ref, xn_ref, idx_ref, prob_ref, rank_ref, cnt_ref, run_ref):
    i = pl.program_id(0)
    tm = x_ref.shape[0]

    @pl.when(i == 0)
    def _():
        run_ref[...] = cnt_in_ref[...]

    an = (_head_rmsnorm(ao_ref[...], gs_ref[...]) * (1.0 - lam_init)).astype(BF16)
    rn = (_head_rmsnorm(ro_ref[...], grn_ref[...]) * jax.nn.silu(rg_ref[...])).astype(BF16)
    a = _dot(an, woa_ref[...])
    r = _dot(rn, wor_ref[...])
    mixed = jax.nn.sigmoid(ga_ref[...]) * a + jax.nn.sigmoid(gr_ref[...]) * r
    x1 = x_ref[...] + _dot(mixed.astype(BF16), wout_ref[...])
    x1_ref[...] = x1
    ms = jnp.mean(x1 * x1, axis=-1, keepdims=True)
    xn = x1 * lax.rsqrt(ms + NORM_EPS) * gffn_ref[...]
    for s in range(ROW_SUB):
        xn_ref[pl.ds(s, tm, stride=ROW_SUB), :] = xn[:, s * HEAD_W:(s + 1) * HEAD_W]

    logits = _dot(xn.astype(BF16), wr_ref[...]) + br_ref[...]
    lane = lax.broadcasted_iota(jnp.int32, logits.shape, 1).astype(F32)
    sel = jnp.zeros(logits.shape, F32)
    work = logits
    tops, idxs = [], []
    for _ in range(TOP_K):
        m = jnp.max(work, axis=-1, keepdims=True)
        idx = jnp.min(jnp.where(work == m, lane, float(N_EXPERTS)), axis=-1, keepdims=True)
        hit = lane == idx
        sel = jnp.where(hit, 1.0, sel)
        work = jnp.where(hit, -jnp.inf, work)
        tops.append(m)
        idxs.append(idx)
    es = [jnp.exp(v - tops[0]) for v in tops]
    inv = 1.0 / (es[0] + es[1] + es[2] + es[3])

    trow = lax.broadcasted_iota(jnp.int32, (tm, tm), 0)
    tcol = lax.broadcasted_iota(jnp.int32, (tm, tm), 1)
    before = jnp.where(trow > tcol, 1.0, 0.0).astype(BF16)
    ranks = _dot(before, sel.astype(BF16)) + run_ref[...]
    k_lane = lax.broadcasted_iota(jnp.int32, (tm, TOP_K), 1)
    idx_out = jnp.zeros((tm, TOP_K), F32)
    prob_out = jnp.zeros((tm, TOP_K), F32)
    rank_out = jnp.zeros((tm, TOP_K), F32)
    for k in range(TOP_K):
        rk = jnp.sum(jnp.where(lane == idxs[k], ranks, 0.0), axis=-1, keepdims=True)
        idx_out = jnp.where(k_lane == k, idxs[k], idx_out)
        prob_out = jnp.where(k_lane == k, es[k] * inv, prob_out)
        rank_out = jnp.where(k_lane == k, rk, rank_out)
    idx_ref[...] = idx_out.astype(jnp.int32)
    prob_ref[...] = prob_out
    rank_ref[...] = rank_out.astype(jnp.int32)
    run = run_ref[...] + jnp.sum(sel, axis=0, keepdims=True)
    run_ref[...] = run
    cnt_ref[...] = run


def _merge(lam_init, ao, ro, z, x, g_subln, g_rec, woa, wor, wout, g_ffn, w_router, b_router, cnt_in):
    t = x.shape[0]
    tm = MERGE_TM
    row = lambda i: (i, 0)
    fixed = lambda i: (0, 0)
    tok = pl.BlockSpec((tm, D_MODEL), row)
    wspec = pl.BlockSpec((D_MODEL, D_MODEL), fixed)
    narrow = pl.BlockSpec((tm, TOP_K), row)
    return pl.pallas_call(
        functools.partial(_merge_kernel, lam_init),
        grid=(t // tm,),
        in_specs=[
            tok, tok,
            pl.BlockSpec((tm, D_MODEL), lambda i: (i, ZC_RG)),
            pl.BlockSpec((tm, D_MODEL), lambda i: (i, ZC_GA)),
            pl.BlockSpec((tm, D_MODEL), lambda i: (i, ZC_GR)),
            tok,
            pl.BlockSpec((1, HEAD_W), fixed), pl.BlockSpec((1, HEAD_W), fixed),
            wspec, wspec, wspec,
            pl.BlockSpec((1, D_MODEL), fixed),
            pl.BlockSpec((D_MODEL, N_EXPERTS), fixed),
            pl.BlockSpec((1, N_EXPERTS), fixed),
            pl.BlockSpec((1, N_EXPERTS), fixed),
        ],
        out_specs=[tok, pl.BlockSpec((tm * ROW_SUB, HEAD_W), row), narrow, narrow, narrow,
                   pl.BlockSpec((1, N_EXPERTS), fixed)],
        out_shape=[
            jax.ShapeDtypeStruct((t, D_MODEL), F32),
            jax.ShapeDtypeStruct((t * ROW_SUB, HEAD_W), F32),
            jax.ShapeDtypeStruct((t, TOP_K), jnp.int32),
            jax.ShapeDtypeStruct((t, TOP_K), F32),
            jax.ShapeDtypeStruct((t, TOP_K), jnp.int32),
            jax.ShapeDtypeStruct((1, N_EXPERTS), F32),
        ],
        scratch_shapes=[pltpu.VMEM((1, N_EXPERTS), F32)],
        compiler_params=_cparams("arbitrary"),
        name="merge",
    )(ao, ro, z, z, z, x, g_subln, g_rec, woa, wor, wout, g_ffn, w_router, b_router, cnt_in)


def _row_copy(src, dst, sem):
    return pltpu.make_async_copy(src, dst, sem)


def _dispatch_kernel(n_first, n_steps, pos_ref, xa_hbm, xb_hbm, xs_hbm, stage, load_sem, copy_sem):
    i = pl.program_id(0)
    tm = ROW_TM
    rows = tm * ROW_SUB
    slot = i % DISPATCH_SLOTS

    def row_tile(off):
        return pl.ds(pl.multiple_of(off, ROW_SUB), ROW_SUB)

    def load(t, sl):
        @pl.when(t < n_first)
        def _():
            _row_copy(xa_hbm.at[pl.ds(pl.multiple_of(t * rows, rows), rows)], stage.at[sl], load_sem.at[sl]).start()

        @pl.when(t >= n_first)
        def _():
            _row_copy(xb_hbm.at[pl.ds(pl.multiple_of((t - n_first) * rows, rows), rows)], stage.at[sl],
                      load_sem.at[sl]).start()

    def drain(parity):
        for k in range(TOP_K):
            _row_copy(stage.at[0], xs_hbm.at[pl.ds(0, rows)], copy_sem.at[parity]).wait()

    @pl.when(i == 0)
    def _():
        load(i, slot)

    @pl.when(i + 1 < n_steps)
    def _():
        load(i + 1, (i + 1) % DISPATCH_SLOTS)

    _row_copy(xa_hbm.at[pl.ds(0, rows)], stage.at[slot], load_sem.at[slot]).wait()

    def start(g, carry):
        r0 = g * ROW_COPY_UNROLL
        n = ROW_COPY_UNROLL * TOP_K
        dst = [pos_ref[0, 0, r0 * TOP_K + j] for j in range(n)]
        for j in range(n):
            _row_copy(stage.at[slot, row_tile((r0 + j // TOP_K) * ROW_SUB)],
                      xs_hbm.at[row_tile(dst[j])], copy_sem.at[i % 2]).start()
        return carry

    lax.fori_loop(0, tm // ROW_COPY_UNROLL, start, 0)

    @pl.when(i > 0)
    def _():
        drain((i - 1) % 2)

    @pl.when(i == n_steps - 1)
    def _():
        drain(i % 2)


def _dispatch(pos, xn_a, xn_b):
    tm = ROW_TM
    n_a, n_b = xn_a.shape[0] // (tm * ROW_SUB), xn_b.shape[0] // (tm * ROW_SUB)
    pos3 = pos.reshape(n_a + n_b, 1, tm * TOP_K)
    return pl.pallas_call(
        functools.partial(_dispatch_kernel, n_a, n_a + n_b),
        grid=(n_a + n_b,),
        in_specs=[
            pl.BlockSpec((1, 1, tm * TOP_K), lambda i: (i, 0, 0), memory_space=pltpu.SMEM),
            pl.BlockSpec(memory_space=pl.ANY),
            pl.BlockSpec(memory_space=pl.ANY),
        ],
        out_specs=pl.BlockSpec(memory_space=pl.ANY),
        out_shape=jax.ShapeDtypeStruct((pos.size * ROW_SUB, HEAD_W), F32),
        scratch_shapes=[
            pltpu.VMEM((DISPATCH_SLOTS, tm * ROW_SUB, HEAD_W), F32),
            pltpu.SemaphoreType.DMA((DISPATCH_SLOTS,)),
            pltpu.SemaphoreType.DMA((2,)),
        ],
        compiler_params=_cparams("arbitrary"),
        name="dispatch",
    )(pos3, xn_a, xn_b)


def _experts_kernel(tile_ref, exp_ref, lo_ref, hi_ref, first_ref, fresh_ref, par_ref, nxt_ref, more_ref,
                    xs_ref, wg_hbm, bg_ref, wu_hbm, bu_ref, wd_hbm, bd_ref, o_ref,
                    wg_s, wu_s, wd_s, x_s, wf_s, wsem):
    w = pl.program_id(0)
    lo = lo_ref[w]
    hi = hi_ref[w]
    tm = x_s.shape[0]

    def lanes(s):
        return slice(s * HEAD_W, (s + 1) * HEAD_W)

    def sub_rows(s):
        return pl.ds(s, tm, stride=ROW_SUB)

    def weight_copies(e, sl):
        return [pltpu.make_async_copy(src.at[e], wf_s.at[sl, j], wsem.at[sl])
                for j, src in enumerate((wg_hbm, wu_hbm, wd_hbm))]

    @pl.when(w == 0)
    def _():
        for c in weight_copies(exp_ref[0], par_ref[0]):
            c.start()

    @pl.when(fresh_ref[w] == 1)
    def _():
        sl = par_ref[w]
        for c in weight_copies(exp_ref[w], sl):
            c.wait()

        @pl.when(more_ref[w] == 1)
        def _():
            for c in weight_copies(nxt_ref[w], 1 - sl):
                c.start()

        wg_s[...] = wf_s[sl, 0].astype(BF16)
        wu_s[...] = wf_s[sl, 1].astype(BF16)
        wd_s[...] = wf_s[sl, 2].astype(BF16)

    @pl.when(hi > lo)
    def _():
        for s in range(ROW_SUB):
            x_s[:, lanes(s)] = xs_ref[sub_rows(s), :].astype(BF16)
        x = x_s[...]
        out = bd_ref[0]
        def gate_up(c):
            ff = slice(c, c + MOE_FF_SLICE)
            return _dot(x, wg_s[:, ff]) + bg_ref[0, :, ff], _dot(x, wu_s[:, ff]) + bu_ref[0, :, ff]

        ahead = gate_up(0)
        for c in range(0, D_MODEL, MOE_FF_SLICE):
            gate, up = ahead
            if c + MOE_FF_SLICE < D_MODEL:
                ahead = gate_up(c + MOE_FF_SLICE)
            gate = jnp.minimum(gate, SWIGLU_LIMIT)
            up = jnp.clip(up, -SWIGLU_LIMIT, SWIGLU_LIMIT)
            act = gate * jax.nn.sigmoid(SWIGLU_ALPHA * gate) * (up + 1.0)
            out = out + _dot(act.astype(BF16), wd_s[c:c + MOE_FF_SLICE, :])
        row = lax.broadcasted_iota(jnp.int32, (tm, 1), 0)
        mine = (row >= lo) & (row < hi)

        @pl.when(first_ref[w] == 1)
        def _():
            for s in range(ROW_SUB):
                o_ref[sub_rows(s), :] = jnp.where(mine, out[:, lanes(s)], 0.0)

        @pl.when(first_ref[w] == 0)
        def _():
            for s in range(ROW_SUB):
                o_ref[sub_rows(s), :] = jnp.where(mine, out[:, lanes(s)], o_ref[sub_rows(s), :])


def _experts(plan, xs, w_gate, b_gate, w_up, b_up, w_down, b_down):
    tm = MOE_TM
    n_items = plan[0].shape[0]
    rows = lambda w, tile, *_: (tile[w], 0)
    wspec = pl.BlockSpec(memory_space=pl.ANY)
    bspec = pl.BlockSpec((1, 1, D_MODEL), lambda w, tile, exp, *_: (exp[w], 0, 0))
    grid_spec = pltpu.PrefetchScalarGridSpec(
        num_scalar_prefetch=len(plan),
        grid=(n_items,),
        in_specs=[pl.BlockSpec((tm * ROW_SUB, HEAD_W), rows), wspec, bspec, wspec, bspec, wspec, bspec],
        out_specs=pl.BlockSpec((tm * ROW_SUB, HEAD_W), rows),
        scratch_shapes=[pltpu.VMEM((D_MODEL, D_MODEL), BF16)] * 3 + [
            pltpu.VMEM((tm, D_MODEL), BF16),
            pltpu.VMEM((2, 3, D_MODEL, D_MODEL), F32),
            pltpu.SemaphoreType.DMA((2,)),
        ],
    )
    return pl.pallas_call(
        _experts_kernel,
        grid_spec=grid_spec,
        out_shape=jax.ShapeDtypeStruct(xs.shape, F32),
        compiler_params=_cparams("arbitrary"),
        name="experts",
    )(*plan, xs, w_gate, b_gate.reshape(N_EXPERTS, 1, D_MODEL), w_up, b_up.reshape(N_EXPERTS, 1, D_MODEL),
      w_down, b_down.reshape(N_EXPERTS, 1, D_MODEL))


def _expert_plan(counts, n_rows):
    tm = MOE_TM
    n_tiles = n_rows // tm
    n_items = n_tiles + N_EXPERTS - 1
    ends = jnp.cumsum(counts)
    starts = ends - counts
    first_tile = starts // tm
    last_tile = jnp.maximum(ends - 1, 0) // tm
    items = jnp.where(counts > 0, last_tile - first_tile + 1, 0)
    item_end = jnp.cumsum(items)
    item_start = item_end - items
    total = item_end[-1]
    w = jnp.arange(n_items, dtype=jnp.int32)
    wc = jnp.minimum(w, total - 1)
    exp = jnp.sum((item_end[None, :] <= wc[:, None]).astype(jnp.int32), axis=1)
    tile = (first_tile[exp] + wc - item_start[exp]).astype(jnp.int32)
    lo = jnp.maximum(starts[exp], tile * tm) - tile * tm
    hi = jnp.minimum(ends[exp], (tile + 1) * tm) - tile * tm
    valid = w < total
    lo = jnp.where(valid, lo, 0).astype(jnp.int32)
    hi = jnp.where(valid, hi, 0).astype(jnp.int32)
    prev_tile = jnp.concatenate([jnp.full((1,), -1, jnp.int32), tile[:-1]])
    prev_exp = jnp.concatenate([jnp.full((1,), -1, jnp.int32), exp[:-1]])
    first = (valid & (tile != prev_tile)).astype(jnp.int32)
    fresh = (valid & (exp != prev_exp)).astype(jnp.int32)
    ids = jnp.arange(N_EXPERTS, dtype=jnp.int32)
    later = jnp.where((counts > 0)[None, :] & (ids[None, :] > ids[:, None]), ids[None, :], N_EXPERTS)
    next_exp = jnp.min(later, axis=1)
    parity = ((jnp.cumsum((counts > 0).astype(jnp.int32)) - 1) % 2).astype(jnp.int32)
    more = (next_exp[exp] < N_EXPERTS).astype(jnp.int32)
    nxt = jnp.where(more == 1, next_exp[exp], exp).astype(jnp.int32)
    return tile, exp, lo, hi, first, fresh, parity[exp], nxt, more


def _combine_kernel(n_steps, pos_ref, pos_next_ref, x1_ref, prob_ref, gfin_ref, ys_ref, y_ref, buf_ref, sem):
    i = pl.program_id(0)
    tm = x1_ref.shape[0]
    slot = i % 2

    def row_tile(off):
        return pl.ds(pl.multiple_of(off, ROW_SUB), ROW_SUB)

    def gather(p_ref, sl):
        def start(g, carry):
            r0 = g * ROW_COPY_UNROLL
            n = ROW_COPY_UNROLL * TOP_K
            src = [p_ref[0, 0, r0 * TOP_K + j] for j in range(n)]
            for j in range(n):
                _row_copy(ys_ref.at[row_tile(src[j])],
                          buf_ref.at[sl, j % TOP_K, row_tile((r0 + j // TOP_K) * ROW_SUB)], sem.at[sl]).start()
            return carry

        lax.fori_loop(0, tm // ROW_COPY_UNROLL, start, 0)

    @pl.when(i == 0)
    def _():
        gather(pos_ref, slot)

    @pl.when(i + 1 < n_steps)
    def _():
        gather(pos_next_ref, 1 - slot)

    for k in range(TOP_K):
        _row_copy(ys_ref.at[pl.ds(0, tm * ROW_SUB)], buf_ref.at[slot, k], sem.at[slot]).wait()

    prob = prob_ref[...]
    parts = []
    for s in range(ROW_SUB):
        part = x1_ref[:, s * HEAD_W:(s + 1) * HEAD_W]
        for k in range(TOP_K):
            part = part + prob[:, k:k + 1] * buf_ref[slot, k, pl.ds(s, tm, stride=ROW_SUB), :]
        parts.append(part)
    x2 = jnp.concatenate(parts, axis=-1)
    ms = jnp.mean(x2 * x2, axis=-1, keepdims=True)
    y_ref[...] = x2 * lax.rsqrt(ms + NORM_EPS) * gfin_ref[...]


def _combine(pos, x1, prob, g_final, ys):
    t = x1.shape[0]
    tm = ROW_TM
    n = t // tm
    pos3 = pos.reshape(n, 1, tm * TOP_K)
    return pl.pallas_call(
        functools.partial(_combine_kernel, n),
        grid=(n,),
        in_specs=[
            pl.BlockSpec((1, 1, tm * TOP_K), lambda i: (i, 0, 0), memory_space=pltpu.SMEM),
            pl.BlockSpec((1, 1, tm * TOP_K), lambda i: (jnp.minimum(i + 1, n - 1), 0, 0), memory_space=pltpu.SMEM),
            pl.BlockSpec((tm, D_MODEL), lambda i: (i, 0)),
            pl.BlockSpec((tm, TOP_K), lambda i: (i, 0)),
            pl.BlockSpec((1, D_MODEL), lambda i: (0, 0)),
            pl.BlockSpec(memory_space=pl.ANY),
        ],
        out_specs=pl.BlockSpec((tm, D_MODEL), lambda i: (i, 0)),
        out_shape=jax.ShapeDtypeStruct((t, D_MODEL), F32),
        scratch_shapes=[pltpu.VMEM((2, TOP_K, tm * ROW_SUB, HEAD_W), F32), pltpu.SemaphoreType.DMA((2,))],
        compiler_params=_cparams("arbitrary"),
        name="combine",
    )(pos3, pos3, x1, prob, g_final, ys)


def _lambda_init(layer):
    return 0.8 - 0.6 * math.exp(-0.3 * layer)


def kernel(x_prompt, x_sample, cache_k, cache_v, state_rec, g_mix, w_in, lambda_q1, lambda_k1, lambda_q2, lambda_k2, g_subln, w_o_attn, lb_logits, g_rec_norm, w_o_rec, w_out, g_ffn, w_router, b_router, w_gate, b_gate, w_up, b_up, w_down, b_down, g_final):
    nb, seq, _ = x_prompt.shape
    nbd, ld, _ = x_sample.shape
    past = cache_k.shape[2]
    tp, ts = nb * seq, nbd * ld
    lam_init = _lambda_init(0)
    lam = (jnp.exp(jnp.sum(lambda_q1[0].astype(F32) * lambda_k1[0].astype(F32)))
           - jnp.exp(jnp.sum(lambda_q2[0].astype(F32) * lambda_k2[0].astype(F32))) + lam_init).reshape(1)

    w_in_b = w_in[0].astype(BF16)
    woa, wor, wout = w_o_attn[0].astype(BF16), w_o_rec[0].astype(BF16), w_out[0].astype(BF16)
    wr = w_router[0].astype(BF16)
    g_mix2, g_ffn2, g_fin2 = g_mix[0].reshape(1, -1), g_ffn[0].reshape(1, -1), g_final.reshape(1, -1)
    gs2, grn2 = g_subln[0].reshape(1, -1), g_rec_norm[0].reshape(1, -1)
    br2 = b_router[0].reshape(1, -1)

    cos_p, sin_p = _rope_tables(jnp.arange(seq, dtype=jnp.int32))
    pos_s = past + (jnp.arange(PROJ_TM, dtype=jnp.int32) % ld)
    cos_s, sin_s = _rope_tables(pos_s)

    xp = x_prompt.reshape(tp, D_MODEL)
    xs_in = x_sample.reshape(ts, D_MODEL)
    zp, k_p, v_p, kb_p, vb_p = _project(xp, g_mix2, w_in_b, cos_p, sin_p, lb_logits)
    zs, k_s, v_s, kb_s, vb_s = _project(xs_in, g_mix2, w_in_b, cos_s, sin_s, lb_logits)

    ao_p = _attn_prompt(lam, zp, kb_p, vb_p, nb, seq)
    ao_s = _attn_sample(lam, zs, kb_s, vb_s, cache_k[0], cache_v[0], nbd, ld)
    ro_p, st_p = _hgrn(zp, jnp.zeros((nb, HEADS, HEAD_W, HEAD_W), F32), nb, seq // CHUNK, HGRN_GROUP_PROMPT)
    ro_s, st_s = _hgrn(zs, state_rec[0], nbd, ld // CHUNK, HGRN_GROUP_SAMPLE)

    merge_w = (gs2, grn2, woa, wor, wout, g_ffn2, wr, br2)
    zero_cnt = jnp.zeros((1, N_EXPERTS), F32)
    x1_p, xn_p, idx_p, prob_p, rank_p, cnt_p = _merge(lam_init, ao_p, ro_p, zp, xp, *merge_w, zero_cnt)
    x1_s, xn_s, idx_s, prob_s, rank_s, cnt = _merge(lam_init, ao_s, ro_s, zs, xs_in, *merge_w, cnt_p)

    counts = cnt[0].astype(jnp.int32)
    starts = jnp.cumsum(counts) - counts
    experts = jnp.arange(N_EXPERTS, dtype=jnp.int32)

    def positions(idx, rank):
        return (jnp.sum(jnp.where(idx[..., None] == experts, starts, 0), axis=-1) + rank) * ROW_SUB

    pos_p = positions(idx_p, rank_p)
    pos_s2 = positions(idx_s, rank_s)
    n_rows = (tp + ts) * TOP_K
    plan = _expert_plan(counts, n_rows)

    xs_sorted = _dispatch(jnp.concatenate([pos_p, pos_s2], axis=0), xn_p, xn_s)
    ys = _experts(plan, xs_sorted, w_gate[0], b_gate[0], w_up[0], b_up[0], w_down[0], b_down[0])
    y_p = _combine(pos_p, x1_p, prob_p, g_fin2, ys)
    y_s = _combine(pos_s2, x1_s, prob_s, g_fin2, ys)

    return (y_p.reshape(nb, seq, D_MODEL), y_s.reshape(nbd, ld, D_MODEL),
            k_p.reshape(1, nb, seq, HEADS, HEAD_W), v_p.reshape(1, nb, seq, HEADS, HEAD_W), st_p[None],
            k_s.reshape(1, nbd, ld, HEADS, HEAD_W), v_s.reshape(1, nbd, ld, HEADS, HEAD_W), st_s[None])
```

```python
import functools
import math

import jax
import jax.numpy as jnp
from jax import lax
from jax.experimental import pallas as pl
from jax.experimental.pallas import tpu as pltpu

F32 = jnp.float32
BF16 = jnp.bfloat16

D_MODEL = 1024
CHUNK = 64
HEADS = 8
HEAD_W = 128
ATTN_HEAD_DIM = 64
ROPE_THETA = 10000.0
N_EXPERTS = 32
TOP_K = 4
SWIGLU_ALPHA = 1.702
SWIGLU_LIMIT = 7.0
NORM_EPS = 1e-6
N_COL_BLOCKS = 9
COL_Q, COL_K, COL_V, COL_F = range(4)
N_Z_BLOCKS = 7
ZC_Q, ZC_F, ZC_RQ, ZC_RI, ZC_RG, ZC_GA, ZC_GR = range(7)
PROJ_TM = 256

ATTN_TQ = 256
CACHE_COPY_PARTS = 4
HGRN_GROUP_PROMPT = 2
HGRN_GROUP_SAMPLE = 4
MERGE_TM = 512
MOE_TM = 512
MOE_FF_SLICE = 256
ROW_TM = 256
ROW_COPY_UNROLL = 4
DISPATCH_SLOTS = 3
ROW_SUB = D_MODEL // HEAD_W
VMEM_LIMIT = 56 * 1024 * 1024


def _cparams(*sem):
    return pltpu.CompilerParams(dimension_semantics=sem, vmem_limit_bytes=VMEM_LIMIT)


def _dot(a, b):
    return jnp.dot(a, b, preferred_element_type=F32)


def _dot_nt(a, b):
    return lax.dot_general(a, b, (((1,), (1,)), ((), ())), preferred_element_type=F32)


def _proj_kernel(x_ref, g_ref, w_ref, cos_ref, sin_ref, lbl_ref,
                 z_ref, kout_ref, vout_ref, kb_ref, vb_ref):
    tm = x_ref.shape[0]
    x = x_ref[...]
    ms = jnp.mean(x * x, axis=-1, keepdims=True)
    hn = (x * lax.rsqrt(ms + NORM_EPS) * g_ref[...]).astype(BF16)

    def col(j):
        return _dot(hn, w_ref[:, j * D_MODEL:(j + 1) * D_MODEL])

    def zcols(c):
        return slice(c * D_MODEL, (c + 1) * D_MODEL)

    def head(h):
        return slice(h * HEAD_W, (h + 1) * HEAD_W)

    def head_rows(h):
        return pl.ds(h, tm, stride=HEADS)

    def rope(zh):
        lane = lax.broadcasted_iota(jnp.int32, zh.shape, 1)
        first = (lane % ATTN_HEAD_DIM) < (ATTN_HEAD_DIM // 2)
        partner = jnp.where(first, pltpu.roll(zh, HEAD_W - ATTN_HEAD_DIM // 2, 1),
                            pltpu.roll(zh, ATTN_HEAD_DIM // 2, 1))
        return zh * cos_ref[...] + partner * sin_ref[...]

    zq = col(COL_Q)
    for h in range(HEADS):
        z_ref[:, head(h)] = rope(zq[:, head(h)]) * (ATTN_HEAD_DIM ** -0.5)

    zk = col(COL_K)
    for h in range(HEADS):
        kh = rope(zk[:, head(h)])
        kout_ref[head_rows(h), :] = kh
        kb_ref[:, head(h)] = kh.astype(BF16)

    zv = col(COL_V)
    for h in range(HEADS):
        vout_ref[head_rows(h), :] = zv[:, head(h)]
    vb_ref[...] = zv.astype(BF16)

    lbl = lbl_ref[...]
    e = jnp.exp(lbl - jnp.max(lbl, axis=0, keepdims=True))
    lb = e[0:1] / jnp.sum(e, axis=0, keepdims=True)
    z_ref[:, zcols(ZC_F)] = lb + (1.0 - lb) * jax.nn.sigmoid(col(COL_F))

    for j in range(COL_F + 1, N_COL_BLOCKS):
        z_ref[:, zcols(j - COL_V)] = col(j)


def _project(x, g, w_bf16, cos_t, sin_t, lb_logits):
    t = x.shape[0]
    tm = PROJ_TM
    n_tab = cos_t.shape[0] // tm
    tok = lambda i: (i, 0)
    fixed = lambda i: (0, 0)
    return pl.pallas_call(
        _proj_kernel,
        grid=(t // tm,),
        in_specs=[
            pl.BlockSpec((tm, D_MODEL), tok),
            pl.BlockSpec((1, D_MODEL), fixed),
            pl.BlockSpec((D_MODEL, N_COL_BLOCKS * D_MODEL), fixed, pipeline_mode=pl.Buffered(1)),
            pl.BlockSpec((tm, HEAD_W), lambda i: (i % n_tab, 0)),
            pl.BlockSpec((tm, HEAD_W), lambda i: (i % n_tab, 0)),
            pl.BlockSpec((2, D_MODEL), fixed),
        ],
        out_specs=[
            pl.BlockSpec((tm, N_Z_BLOCKS * D_MODEL), tok),
            pl.BlockSpec((tm * HEADS, HEAD_W), tok),
            pl.BlockSpec((tm * HEADS, HEAD_W), tok),
            pl.BlockSpec((tm, D_MODEL), tok),
            pl.BlockSpec((tm, D_MODEL), tok),
        ],
        out_shape=[
            jax.ShapeDtypeStruct((t, N_Z_BLOCKS * D_MODEL), F32),
            jax.ShapeDtypeStruct((t * HEADS, HEAD_W), F32),
            jax.ShapeDtypeStruct((t * HEADS, HEAD_W), F32),
            jax.ShapeDtypeStruct((t, D_MODEL), BF16),
            jax.ShapeDtypeStruct((t, D_MODEL), BF16),
        ],
        compiler_params=_cparams("arbitrary"),
        name="proj",
    )(x, g, w_bf16, cos_t, sin_t, lb_logits)


def _rope_tables(pos):
    half = ATTN_HEAD_DIM // 2
    inv = jnp.power(ROPE_THETA, -2.0 * jnp.arange(half, dtype=F32) / ATTN_HEAD_DIM)
    ang = pos.astype(F32)[:, None] * inv[None, :]
    cos = jnp.tile(jnp.cos(ang), (1, HEAD_W // half))
    sin = jnp.sin(ang)
    sin = jnp.tile(jnp.concatenate([-sin, sin], axis=-1), (1, HEAD_W // ATTN_HEAD_DIM))
    return cos, sin


def _split_components(q):
    lane = lax.broadcasted_iota(jnp.int32, q.shape, 1)
    q1 = jnp.where(lane < ATTN_HEAD_DIM, q, 0.0)
    q2 = jnp.where(lane >= ATTN_HEAD_DIM, q, 0.0)
    return jnp.concatenate([q1, q2], axis=0).astype(BF16)


def _attn_prompt_kernel(lam_ref, q_ref, k_ref, v_ref, o_ref):
    lam = lam_ref[0]
    tq = ATTN_TQ
    seq = k_ref.shape[0]
    k = k_ref[...]
    v = v_ref[...]
    row = lax.broadcasted_iota(jnp.int32, (2 * tq, tq), 0)
    col = lax.broadcasted_iota(jnp.int32, (2 * tq, tq), 1)
    visible = (col // CHUNK) <= ((row % tq) // CHUNK)
    n_tiles = seq // tq

    def scores(qi):
        lo = qi * tq
        qq = _split_components(q_ref[lo:lo + tq, :])
        s_d = jnp.where(visible, _dot_nt(qq, k[lo:lo + tq]), -jnp.inf)
        return s_d, (_dot_nt(qq, k[:lo]) if qi > 0 else None)

    def weighted_values(qi, w_d, w_m):
        lo = qi * tq
        o = _dot(w_d, v[lo:lo + tq])
        if qi > 0:
            o = o + _dot(w_m, v[:lo])
        o_ref[lo:lo + tq, :] = o

    ahead = scores(0)
    behind = None
    for qi in range(n_tiles):
        s_d, s_m = ahead
        if qi + 1 < n_tiles:
            ahead = scores(qi + 1)
        if behind is not None:
            weighted_values(*behind)
        m = jnp.max(s_d, axis=-1, keepdims=True)
        if qi > 0:
            m = jnp.maximum(m, jnp.max(s_m, axis=-1, keepdims=True))
            p_m = jnp.exp(s_m - m)
        p_d = jnp.exp(s_d - m)
        l = jnp.sum(p_d, axis=-1, keepdims=True)
        if qi > 0:
            l = l + jnp.sum(p_m, axis=-1, keepdims=True)
        r = 1.0 / l
        r1 = r[:tq]
        r2 = lam * r[tq:]
        w_d = (p_d[:tq] * r1 - p_d[tq:] * r2).astype(BF16)
        w_m = (p_m[:tq] * r1 - p_m[tq:] * r2).astype(BF16) if qi > 0 else None
        behind = (qi, w_d, w_m)
    weighted_values(*behind)


def _attn_prompt(lam, z, kb, vb, n_batch, seq):
    return pl.pallas_call(
        _attn_prompt_kernel,
        grid=(n_batch, HEADS),
        in_specs=[
            pl.BlockSpec(memory_space=pltpu.SMEM),
            pl.BlockSpec((seq, HEAD_W), lambda b, h: (b, ZC_Q * HEADS + h)),
            pl.BlockSpec((seq, HEAD_W), lambda b, h: (b, h)),
            pl.BlockSpec((seq, HEAD_W), lambda b, h: (b, h)),
        ],
        out_specs=pl.BlockSpec((seq, HEAD_W), lambda b, h: (b, h)),
        out_shape=jax.ShapeDtypeStruct((n_batch * seq, D_MODEL), F32),
        compiler_params=_cparams("arbitrary", "arbitrary"),
        name="attn_prompt",
    )(lam, z, kb, vb)


def _attn_sample_kernel(n_steps, lam_ref, q_ref, kn_ref, vn_ref, ck_hbm, cv_hbm, o_ref, kbuf, vbuf, sem):
    b = pl.program_id(0)
    h = pl.program_id(1)
    step = b * HEADS + h
    slot = step % 2

    def cache_copies(bb, hh, sl):
        part = kbuf.shape[1] // CACHE_COPY_PARTS
        copies = []
        for j in range(CACHE_COPY_PARTS):
            keys = pl.ds(j * part, part)
            copies.append(pltpu.make_async_copy(ck_hbm.at[bb, keys, hh, :], kbuf.at[sl, keys], sem.at[0, sl]))
            copies.append(pltpu.make_async_copy(cv_hbm.at[bb, keys, hh, :], vbuf.at[sl, keys], sem.at[1, sl]))
        return copies

    @pl.when(step == 0)
    def _():
        for c in cache_copies(b, h, slot):
            c.start()

    @pl.when(step + 1 < n_steps)
    def _():
        nxt = step + 1
        for c in cache_copies(nxt // HEADS, nxt % HEADS, 1 - slot):
            c.start()

    for c in cache_copies(b, h, slot):
        c.wait()
    ck_ref = kbuf.at[slot]
    cv_ref = vbuf.at[slot]

    lam = lam_ref[0]
    n_q = q_ref.shape[0]
    qq = _split_components(q_ref[...])
    s_c = _dot_nt(qq, ck_ref[...].astype(BF16))
    s_n = _dot_nt(qq, kn_ref[...])
    m = jnp.maximum(jnp.max(s_c, axis=-1, keepdims=True), jnp.max(s_n, axis=-1, keepdims=True))
    p_c = jnp.exp(s_c - m)
    p_n = jnp.exp(s_n - m)
    r = 1.0 / (jnp.sum(p_c, axis=-1, keepdims=True) + jnp.sum(p_n, axis=-1, keepdims=True))
    r1 = r[:n_q]
    r2 = lam * r[n_q:]
    w_c = (p_c[:n_q] * r1 - p_c[n_q:] * r2).astype(BF16)
    w_n = (p_n[:n_q] * r1 - p_n[n_q:] * r2).astype(BF16)
    o_ref[...] = _dot(w_c, cv_ref[...].astype(BF16)) + _dot(w_n, vn_ref[...])


def _attn_sample(lam, z, kb, vb, cache_k, cache_v, n_batch, n_q):
    past = cache_k.shape[1]
    cache_spec = pl.BlockSpec(memory_space=pl.ANY)
    return pl.pallas_call(
        functools.partial(_attn_sample_kernel, n_batch * HEADS),
        grid=(n_batch, HEADS),
        in_specs=[
            pl.BlockSpec(memory_space=pltpu.SMEM),
            pl.BlockSpec((n_q, HEAD_W), lambda b, h: (b, ZC_Q * HEADS + h)),
            pl.BlockSpec((n_q, HEAD_W), lambda b, h: (b, h)),
            pl.BlockSpec((n_q, HEAD_W), lambda b, h: (b, h)),
            cache_spec, cache_spec,
        ],
        out_specs=pl.BlockSpec((n_q, HEAD_W), lambda b, h: (b, h)),
        out_shape=jax.ShapeDtypeStruct((n_batch * n_q, D_MODEL), F32),
        scratch_shapes=[
            pltpu.VMEM((2, past, HEAD_W), F32),
            pltpu.VMEM((2, past, HEAD_W), F32),
            pltpu.SemaphoreType.DMA((2, 2)),
        ],
        compiler_params=_cparams("arbitrary", "arbitrary"),
        name="attn_sample",
    )(lam, z, kb, vb, cache_k, cache_v)


def _split3(x):
    h1 = x.astype(BF16)
    r1 = x - h1.astype(F32)
    h2 = r1.astype(BF16)
    h3 = (r1 - h2.astype(F32)).astype(BF16)
    return h1, h2, h3


def _when_step(step, n_steps, which):
    def deco(body):
        if n_steps == 1:
            body()
        else:
            pl.when(step == which)(body)
    return deco


def _hgrn_kernel(n_chunks, f_ref, q_ref, i_ref, s0_ref, o_ref, sout_ref, st_ref):
    c = pl.program_id(1)
    n_group = f_ref.shape[0]

    @_when_step(c, n_chunks, 0)
    def _():
        for g in range(n_group):
            for h in range(HEADS):
                st_ref[g, h] = s0_ref[g, h].T

    row = lax.broadcasted_iota(jnp.int32, (CHUNK, CHUNK), 0)
    col = lax.broadcasted_iota(jnp.int32, (CHUNK, CHUNK), 1)
    causal = row >= col
    tril = jnp.where(causal, 1.0, 0.0).astype(BF16)
    for g in range(n_group):
        f = f_ref[g]
        h1, h2, h3 = _split3(jnp.log(f))
        b = _dot(tril, h1) + _dot(tril, h2) + _dot(tril, h3)
        b_last = b[CHUNK - 1:CHUNK, :]
        rk = 1.0 - f
        qd = q_ref[g] * jnp.exp(b)
        kd = rk * jnp.exp(-b)
        kl = rk * jnp.exp(b_last - b)
        eb_last = jnp.exp(b_last)
        v = i_ref[g]
        for h in range(HEADS):
            sl = slice(h * HEAD_W, (h + 1) * HEAD_W)
            qh = qd[:, sl].astype(BF16)
            vh = v[:, sl]
            a = jnp.where(causal, _dot_nt(qh, kd[:, sl].astype(BF16)), 0.0)
            st = st_ref[g, h]
            o_ref[g, :, sl] = _dot_nt(qh, st.astype(BF16)) + _dot(a.astype(BF16), vh.astype(BF16))
            st_ref[g, h] = st * eb_last[:, sl] + _dot(vh.T.astype(BF16), kl[:, sl].astype(BF16))

    @_when_step(c, n_chunks, n_chunks - 1)
    def _():
        for g in range(n_group):
            for h in range(HEADS):
                sout_ref[g, h] = st_ref[g, h].T


def _hgrn(z, s0, n_batch, n_chunks, n_group):
    z3 = z.reshape(n_batch, n_chunks * CHUNK, N_Z_BLOCKS * D_MODEL)

    def zblock(zc):
        return pl.BlockSpec((n_group, CHUNK, D_MODEL), lambda b, c: (b, c, zc))

    state = pl.BlockSpec((n_group, HEADS, HEAD_W, HEAD_W), lambda b, c: (b, 0, 0, 0))
    ro, st = pl.pallas_call(
        functools.partial(_hgrn_kernel, n_chunks),
        grid=(n_batch // n_group, n_chunks),
        in_specs=[zblock(ZC_F), zblock(ZC_RQ), zblock(ZC_RI), state],
        out_specs=[pl.BlockSpec((n_group, CHUNK, D_MODEL), lambda b, c: (b, c, 0)), state],
        out_shape=[
            jax.ShapeDtypeStruct((n_batch, n_chunks * CHUNK, D_MODEL), F32),
            jax.ShapeDtypeStruct((n_batch, HEADS, HEAD_W, HEAD_W), F32),
        ],
        scratch_shapes=[pltpu.VMEM((n_group, HEADS, HEAD_W, HEAD_W), F32)],
        compiler_params=_cparams("arbitrary", "arbitrary"),
        name="hgrn",
    )(z3, z3, z3, s0)
    return ro.reshape(n_batch * n_chunks * CHUNK, D_MODEL), st


def _head_rmsnorm(x, g):
    outs = []
    for h in range(HEADS):
        xh = x[:, h * HEAD_W:(h + 1) * HEAD_W]
        ms = jnp.mean(xh * xh, axis=-1, keepdims=True)
        outs.append(xh * lax.rsqrt(ms + NORM_EPS) * g)
    return jnp.concatenate(outs, axis=-1)


def _merge_kernel(lam_init, ao_ref, ro_ref, rg_ref, ga_ref, gr_ref, x_ref, gs_ref, grn_ref,
                  woa_ref, wor_ref, wout_ref, gffn_ref, wr_ref, br_ref, cnt_in_ref,
                  x1_ref, xn_ref, idx_ref, prob_ref, rank_ref, cnt_ref, run_ref):
    i = pl.program_id(0)
    tm = x_ref.shape[0]

    @pl.when(i == 0)
    def _():
        run_ref[...] = cnt_in_ref[...]

    an = (_head_rmsnorm(ao_ref[...], gs_ref[...]) * (1.0 - lam_init)).astype(BF16)
    rn = (_head_rmsnorm(ro_ref[...], grn_ref[...]) * jax.nn.silu(rg_ref[...])).astype(BF16)
    a = _dot(an, woa_ref[...])
    r = _dot(rn, wor_ref[...])
    mixed = jax.nn.sigmoid(ga_ref[...]) * a + jax.nn.sigmoid(gr_ref[...]) * r
    x1 = x_ref[...] + _dot(mixed.astype(BF16), wout_ref[...])
    x1_ref[...] = x1
    ms = jnp.mean(x1 * x1, axis=-1, keepdims=True)
    xn = x1 * lax.rsqrt(ms + NORM_EPS) * gffn_ref[...]
    for s in range(ROW_SUB):
        xn_ref[pl.ds(s, tm, stride=ROW_SUB), :] = xn[:, s * HEAD_W:(s + 1) * HEAD_W]

    logits = _dot(xn.astype(BF16), wr_ref[...]) + br_ref[...]
    lane = lax.broadcasted_iota(jnp.int32, logits.shape, 1).astype(F32)
    sel = jnp.zeros(logits.shape, F32)
    work = logits
    tops, idxs = [], []
    for _ in range(TOP_K):
        m = jnp.max(work, axis=-1, keepdims=True)
        idx = jnp.min(jnp.where(work == m, lane, float(N_EXPERTS)), axis=-1, keepdims=True)
        hit = lane == idx
        sel = jnp.where(hit, 1.0, sel)
        work = jnp.where(hit, -jnp.inf, work)
        tops.append(m)
        idxs.append(idx)
    es = [jnp.exp(v - tops[0]) for v in tops]
    inv = 1.0 / (es[0] + es[1] + es[2] + es[3])

    trow = lax.broadcasted_iota(jnp.int32, (tm, tm), 0)
    tcol = lax.broadcasted_iota(jnp.int32, (tm, tm), 1)
    before = jnp.where(trow > tcol, 1.0, 0.0).astype(BF16)
    ranks = _dot(before, sel.astype(BF16)) + run_ref[...]
    k_lane = lax.broadcasted_iota(jnp.int32, (tm, TOP_K), 1)
    idx_out = jnp.zeros((tm, TOP_K), F32)
    prob_out = jnp.zeros((tm, TOP_K), F32)
    rank_out = jnp.zeros((tm, TOP_K), F32)
    for k in range(TOP_K):
        rk = jnp.sum(jnp.where(lane == idxs[k], ranks, 0.0), axis=-1, keepdims=True)
        idx_out = jnp.where(k_lane == k, idxs[k], idx_out)
        prob_out = jnp.where(k_lane == k, es[k] * inv, prob_out)
        rank_out = jnp.where(k_lane == k, rk, rank_out)
    idx_ref[...] = idx_out.astype(jnp.int32)
    prob_ref[...] = prob_out
    rank_ref[...] = rank_out.astype(jnp.int32)
    run = run_ref[...] + jnp.sum(sel, axis=0, keepdims=True)
    run_ref[...] = run
    cnt_ref[...] = run


def _merge(lam_init, ao, ro, z, x, g_subln, g_rec, woa, wor, wout, g_ffn, w_router, b_router, cnt_in):
    t = x.shape[0]
    tm = MERGE_TM
    row = lambda i: (i, 0)
    fixed = lambda i: (0, 0)
    tok = pl.BlockSpec((tm, D_MODEL), row)
    wspec = pl.BlockSpec((D_MODEL, D_MODEL), fixed)
    narrow = pl.BlockSpec((tm, TOP_K), row)
    return pl.pallas_call(
        functools.partial(_merge_kernel, lam_init),
        grid=(t // tm,),
        in_specs=[
            tok, tok,
            pl.BlockSpec((tm, D_MODEL), lambda i: (i, ZC_RG)),
            pl.BlockSpec((tm, D_MODEL), lambda i: (i, ZC_GA)),
            pl.BlockSpec((tm, D_MODEL), lambda i: (i, ZC_GR)),
            tok,
            pl.BlockSpec((1, HEAD_W), fixed), pl.BlockSpec((1, HEAD_W), fixed),
            wspec, wspec, wspec,
            pl.BlockSpec((1, D_MODEL), fixed),
            pl.BlockSpec((D_MODEL, N_EXPERTS), fixed),
            pl.BlockSpec((1, N_EXPERTS), fixed),
            pl.BlockSpec((1, N_EXPERTS), fixed),
        ],
        out_specs=[tok, pl.BlockSpec((tm * ROW_SUB, HEAD_W), row), narrow, narrow, narrow,
                   pl.BlockSpec((1, N_EXPERTS), fixed)],
        out_shape=[
            jax.ShapeDtypeStruct((t, D_MODEL), F32),
            jax.ShapeDtypeStruct((t * ROW_SUB, HEAD_W), F32),
            jax.ShapeDtypeStruct((t, TOP_K), jnp.int32),
            jax.ShapeDtypeStruct((t, TOP_K), F32),
            jax.ShapeDtypeStruct((t, TOP_K), jnp.int32),
            jax.ShapeDtypeStruct((1, N_EXPERTS), F32),
        ],
        scratch_shapes=[pltpu.VMEM((1, N_EXPERTS), F32)],
        compiler_params=_cparams("arbitrary"),
        name="merge",
    )(ao, ro, z, z, z, x, g_subln, g_rec, woa, wor, wout, g_ffn, w_router, b_router, cnt_in)


def _row_copy(src, dst, sem):
    return pltpu.make_async_copy(src, dst, sem)


def _dispatch_kernel(n_first, n_steps, pos_ref, xa_hbm, xb_hbm, xs_hbm, stage, load_sem, copy_sem):
    i = pl.program_id(0)
    tm = ROW_TM
    rows = tm * ROW_SUB
    slot = i % DISPATCH_SLOTS

    def row_tile(off):
        return pl.ds(pl.multiple_of(off, ROW_SUB), ROW_SUB)

    def load(t, sl):
        @pl.when(t < n_first)
        def _():
            _row_copy(xa_hbm.at[pl.ds(pl.multiple_of(t * rows, rows), rows)], stage.at[sl], load_sem.at[sl]).start()

        @pl.when(t >= n_first)
        def _():
            _row_copy(xb_hbm.at[pl.ds(pl.multiple_of((t - n_first) * rows, rows), rows)], stage.at[sl],
                      load_sem.at[sl]).start()

    def drain(parity):
        for k in range(TOP_K):
            _row_copy(stage.at[0], xs_hbm.at[pl.ds(0, rows)], copy_sem.at[parity]).wait()

    @pl.when(i == 0)
    def _():
        load(i, slot)

    @pl.when(i + 1 < n_steps)
    def _():
        load(i + 1, (i + 1) % DISPATCH_SLOTS)

    _row_copy(xa_hbm.at[pl.ds(0, rows)], stage.at[slot], load_sem.at[slot]).wait()

    def start(g, carry):
        r0 = g * ROW_COPY_UNROLL
        n = ROW_COPY_UNROLL * TOP_K
        dst = [pos_ref[0, 0, r0 * TOP_K + j] for j in range(n)]
        for j in range(n):
            _row_copy(stage.at[slot, row_tile((r0 + j // TOP_K) * ROW_SUB)],
                      xs_hbm.at[row_tile(dst[j])], copy_sem.at[i % 2]).start()
        return carry

    lax.fori_loop(0, tm // ROW_COPY_UNROLL, start, 0)

    @pl.when(i > 0)
    def _():
        drain((i - 1) % 2)

    @pl.when(i == n_steps - 1)
    def _():
        drain(i % 2)


def _dispatch(pos, xn_a, xn_b):
    tm = ROW_TM
    n_a, n_b = xn_a.shape[0] // (tm * ROW_SUB), xn_b.shape[0] // (tm * ROW_SUB)
    pos3 = pos.reshape(n_a + n_b, 1, tm * TOP_K)
    return pl.pallas_call(
        functools.partial(_dispatch_kernel, n_a, n_a + n_b),
        grid=(n_a + n_b,),
        in_specs=[
            pl.BlockSpec((1, 1, tm * TOP_K), lambda i: (i, 0, 0), memory_space=pltpu.SMEM),
            pl.BlockSpec(memory_space=pl.ANY),
            pl.BlockSpec(memory_space=pl.ANY),
        ],
        out_specs=pl.BlockSpec(memory_space=pl.ANY),
        out_shape=jax.ShapeDtypeStruct((pos.size * ROW_SUB, HEAD_W), F32),
        scratch_shapes=[
            pltpu.VMEM((DISPATCH_SLOTS, tm * ROW_SUB, HEAD_W), F32),
            pltpu.SemaphoreType.DMA((DISPATCH_SLOTS,)),
            pltpu.SemaphoreType.DMA((2,)),
        ],
        compiler_params=_cparams("arbitrary"),
        name="dispatch",
    )(pos3, xn_a, xn_b)


def _experts_kernel(tile_ref, exp_ref, lo_ref, hi_ref, first_ref, fresh_ref, par_ref, nxt_ref, more_ref,
                    xs_ref, wg_hbm, bg_ref, wu_hbm, bu_ref, wd_hbm, bd_ref, o_ref,
                    wg_s, wu_s, wd_s, x_s, wf_s, wsem):
    w = pl.program_id(0)
    lo = lo_ref[w]
    hi = hi_ref[w]
    tm = x_s.shape[0]

    def lanes(s):
        return slice(s * HEAD_W, (s + 1) * HEAD_W)

    def sub_rows(s):
        return pl.ds(s, tm, stride=ROW_SUB)

    def weight_copies(e, sl):
        return [pltpu.make_async_copy(src.at[e], wf_s.at[sl, j], wsem.at[sl])
                for j, src in enumerate((wg_hbm, wu_hbm, wd_hbm))]

    @pl.when(w == 0)
    def _():
        for c in weight_copies(exp_ref[0], par_ref[0]):
            c.start()

    @pl.when(fresh_ref[w] == 1)
    def _():
        sl = par_ref[w]
        for c in weight_copies(exp_ref[w], sl):
            c.wait()

        @pl.when(more_ref[w] == 1)
        def _():
            for c in weight_copies(nxt_ref[w], 1 - sl):
                c.start()

        wg_s[...] = wf_s[sl, 0].astype(BF16)
        wu_s[...] = wf_s[sl, 1].astype(BF16)
        wd_s[...] = wf_s[sl, 2].astype(BF16)

    @pl.when(hi > lo)
    def _():
        for s in range(ROW_SUB):
            x_s[:, lanes(s)] = xs_ref[sub_rows(s), :].astype(BF16)
        x = x_s[...]
        out = bd_ref[0]
        def gate_up(c):
            ff = slice(c, c + MOE_FF_SLICE)
            return _dot(x, wg_s[:, ff]) + bg_ref[0, :, ff], _dot(x, wu_s[:, ff]) + bu_ref[0, :, ff]

        ahead = gate_up(0)
        for c in range(0, D_MODEL, MOE_FF_SLICE):
            gate, up = ahead
            if c + MOE_FF_SLICE < D_MODEL:
                ahead = gate_up(c + MOE_FF_SLICE)
            gate = jnp.minimum(gate, SWIGLU_LIMIT)
            up = jnp.clip(up, -SWIGLU_LIMIT, SWIGLU_LIMIT)
            act = gate * jax.nn.sigmoid(SWIGLU_ALPHA * gate) * (up + 1.0)
            out = out + _dot(act.astype(BF16), wd_s[c:c + MOE_FF_SLICE, :])
        row = lax.broadcasted_iota(jnp.int32, (tm, 1), 0)
        mine = (row >= lo) & (row < hi)

        @pl.when(first_ref[w] == 1)
        def _():
            for s in range(ROW_SUB):
                o_ref[sub_rows(s), :] = jnp.where(mine, out[:, lanes(s)], 0.0)

        @pl.when(first_ref[w] == 0)
        def _():
            for s in range(ROW_SUB):
                o_ref[sub_rows(s), :] = jnp.where(mine, out[:, lanes(s)], o_ref[sub_rows(s), :])


def _experts(plan, xs, w_gate, b_gate, w_up, b_up, w_down, b_down):
    tm = MOE_TM
    n_items = plan[0].shape[0]
    rows = lambda w, tile, *_: (tile[w], 0)
    wspec = pl.BlockSpec(memory_space=pl.ANY)
    bspec = pl.BlockSpec((1, 1, D_MODEL), lambda w, tile, exp, *_: (exp[w], 0, 0))
    grid_spec = pltpu.PrefetchScalarGridSpec(
        num_scalar_prefetch=len(plan),
        grid=(n_items,),
        in_specs=[pl.BlockSpec((tm * ROW_SUB, HEAD_W), rows), wspec, bspec, wspec, bspec, wspec, bspec],
        out_specs=pl.BlockSpec((tm * ROW_SUB, HEAD_W), rows),
        scratch_shapes=[pltpu.VMEM((D_MODEL, D_MODEL), BF16)] * 3 + [
            pltpu.VMEM((tm, D_MODEL), BF16),
            pltpu.VMEM((2, 3, D_MODEL, D_MODEL), F32),
            pltpu.SemaphoreType.DMA((2,)),
        ],
    )
    return pl.pallas_call(
        _experts_kernel,
        grid_spec=grid_spec,
        out_shape=jax.ShapeDtypeStruct(xs.shape, F32),
        compiler_params=_cparams("arbitrary"),
        name="experts",
    )(*plan, xs, w_gate, b_gate.reshape(N_EXPERTS, 1, D_MODEL), w_up, b_up.reshape(N_EXPERTS, 1, D_MODEL),
      w_down, b_down.reshape(N_EXPERTS, 1, D_MODEL))


def _expert_plan(counts, n_rows):
    tm = MOE_TM
    n_tiles = n_rows // tm
    n_items = n_tiles + N_EXPERTS - 1
    ends = jnp.cumsum(counts)
    starts = ends - counts
    first_tile = starts // tm
    last_tile = jnp.maximum(ends - 1, 0) // tm
    items = jnp.where(counts > 0, last_tile - first_tile + 1, 0)
    item_end = jnp.cumsum(items)
    item_start = item_end - items
    total = item_end[-1]
    w = jnp.arange(n_items, dtype=jnp.int32)
    wc = jnp.minimum(w, total - 1)
    exp = jnp.sum((item_end[None, :] <= wc[:, None]).astype(jnp.int32), axis=1)
    ids = jnp.arange(N_EXPERTS, dtype=jnp.int32)
    mine = exp[:, None] == ids[None, :]

    def of_item(per_expert):
        return jnp.sum(jnp.where(mine, per_expert[None, :], 0), axis=1).astype(jnp.int32)

    tile = of_item(first_tile) + wc - of_item(item_start)
    lo = jnp.maximum(of_item(starts), tile * tm) - tile * tm
    hi = jnp.minimum(of_item(ends), (tile + 1) * tm) - tile * tm
    valid = w < total
    lo = jnp.where(valid, lo, 0).astype(jnp.int32)
    hi = jnp.where(valid, hi, 0).astype(jnp.int32)
    prev_tile = jnp.concatenate([jnp.full((1,), -1, jnp.int32), tile[:-1]])
    prev_exp = jnp.concatenate([jnp.full((1,), -1, jnp.int32), exp[:-1]])
    first = (valid & (tile != prev_tile)).astype(jnp.int32)
    fresh = (valid & (exp != prev_exp)).astype(jnp.int32)
    later = jnp.where((counts > 0)[None, :] & (ids[None, :] > ids[:, None]), ids[None, :], N_EXPERTS)
    next_exp = jnp.min(later, axis=1)
    parity = ((jnp.cumsum((counts > 0).astype(jnp.int32)) - 1) % 2).astype(jnp.int32)
    next_of_item = of_item(next_exp)
    more = (next_of_item < N_EXPERTS).astype(jnp.int32)
    nxt = jnp.where(more == 1, next_of_item, exp).astype(jnp.int32)
    return tile, exp, lo, hi, first, fresh, of_item(parity), nxt, more


def _combine_kernel(n_steps, pos_ref, pos_next_ref, x1_ref, prob_ref, gfin_ref, ys_ref, y_ref, buf_ref, sem):
    i = pl.program_id(0)
    tm = x1_ref.shape[0]
    slot = i % 2

    def row_tile(off):
        return pl.ds(pl.multiple_of(off, ROW_SUB), ROW_SUB)

    def gather(p_ref, sl):
        def start(g, carry):
            r0 = g * ROW_COPY_UNROLL
            n = ROW_COPY_UNROLL * TOP_K
            src = [p_ref[0, 0, r0 * TOP_K + j] for j in range(n)]
            for j in range(n):
                _row_copy(ys_ref.at[row_tile(src[j])],
                          buf_ref.at[sl, j % TOP_K, row_tile((r0 + j // TOP_K) * ROW_SUB)], sem.at[sl]).start()
            return carry

        lax.fori_loop(0, tm // ROW_COPY_UNROLL, start, 0)

    @pl.when(i == 0)
    def _():
        gather(pos_ref, slot)

    @pl.when(i + 1 < n_steps)
    def _():
        gather(pos_next_ref, 1 - slot)

    for k in range(TOP_K):
        _row_copy(ys_ref.at[pl.ds(0, tm * ROW_SUB)], buf_ref.at[slot, k], sem.at[slot]).wait()

    prob = prob_ref[...]
    parts = []
    for s in range(ROW_SUB):
        part = x1_ref[:, s * HEAD_W:(s + 1) * HEAD_W]
        for k in range(TOP_K):
            part = part + prob[:, k:k + 1] * buf_ref[slot, k, pl.ds(s, tm, stride=ROW_SUB), :]
        parts.append(part)
    x2 = jnp.concatenate(parts, axis=-1)
    ms = jnp.mean(x2 * x2, axis=-1, keepdims=True)
    y_ref[...] = x2 * lax.rsqrt(ms + NORM_EPS) * gfin_ref[...]


def _combine(pos, x1, prob, g_final, ys):
    t = x1.shape[0]
    tm = ROW_TM
    n = t // tm
    pos3 = pos.reshape(n, 1, tm * TOP_K)
    return pl.pallas_call(
        functools.partial(_combine_kernel, n),
        grid=(n,),
        in_specs=[
            pl.BlockSpec((1, 1, tm * TOP_K), lambda i: (i, 0, 0), memory_space=pltpu.SMEM),
            pl.BlockSpec((1, 1, tm * TOP_K), lambda i: (jnp.minimum(i + 1, n - 1), 0, 0), memory_space=pltpu.SMEM),
            pl.BlockSpec((tm, D_MODEL), lambda i: (i, 0)),
            pl.BlockSpec((tm, TOP_K), lambda i: (i, 0)),
            pl.BlockSpec((1, D_MODEL), lambda i: (0, 0)),
            pl.BlockSpec(memory_space=pl.ANY),
        ],
        out_specs=pl.BlockSpec((tm, D_MODEL), lambda i: (i, 0)),
        out_shape=jax.ShapeDtypeStruct((t, D_MODEL), F32),
        scratch_shapes=[pltpu.VMEM((2, TOP_K, tm * ROW_SUB, HEAD_W), F32), pltpu.SemaphoreType.DMA((2,))],
        compiler_params=_cparams("arbitrary"),
        name="combine",
    )(pos3, pos3, x1, prob, g_final, ys)


def _lambda_init(layer):
    return 0.8 - 0.6 * math.exp(-0.3 * layer)


def kernel(x_prompt, x_sample, cache_k, cache_v, state_rec, g_mix, w_in, lambda_q1, lambda_k1, lambda_q2, lambda_k2, g_subln, w_o_attn, lb_logits, g_rec_norm, w_o_rec, w_out, g_ffn, w_router, b_router, w_gate, b_gate, w_up, b_up, w_down, b_down, g_final):
    nb, seq, _ = x_prompt.shape
    nbd, ld, _ = x_sample.shape
    past = cache_k.shape[2]
    tp, ts = nb * seq, nbd * ld
    lam_init = _lambda_init(0)
    lam = (jnp.exp(jnp.sum(lambda_q1[0].astype(F32) * lambda_k1[0].astype(F32)))
           - jnp.exp(jnp.sum(lambda_q2[0].astype(F32) * lambda_k2[0].astype(F32))) + lam_init).reshape(1)

    w_in_b = w_in[0].astype(BF16)
    woa, wor, wout = w_o_attn[0].astype(BF16), w_o_rec[0].astype(BF16), w_out[0].astype(BF16)
    wr = w_router[0].astype(BF16)
    g_mix2, g_ffn2, g_fin2 = g_mix[0].reshape(1, -1), g_ffn[0].reshape(1, -1), g_final.reshape(1, -1)
    gs2, grn2 = g_subln[0].reshape(1, -1), g_rec_norm[0].reshape(1, -1)
    br2 = b_router[0].reshape(1, -1)

    cos_p, sin_p = _rope_tables(jnp.arange(seq, dtype=jnp.int32))
    pos_s = past + (jnp.arange(PROJ_TM, dtype=jnp.int32) % ld)
    cos_s, sin_s = _rope_tables(pos_s)

    xp = x_prompt.reshape(tp, D_MODEL)
    xs_in = x_sample.reshape(ts, D_MODEL)
    zp, k_p, v_p, kb_p, vb_p = _project(xp, g_mix2, w_in_b, cos_p, sin_p, lb_logits)
    zs, k_s, v_s, kb_s, vb_s = _project(xs_in, g_mix2, w_in_b, cos_s, sin_s, lb_logits)

    ao_p = _attn_prompt(lam, zp, kb_p, vb_p, nb, seq)
    ao_s = _attn_sample(lam, zs, kb_s, vb_s, cache_k[0], cache_v[0], nbd, ld)
    ro_p, st_p = _hgrn(zp, jnp.zeros((nb, HEADS, HEAD_W, HEAD_W), F32), nb, seq // CHUNK, HGRN_GROUP_PROMPT)
    ro_s, st_s = _hgrn(zs, state_rec[0], nbd, ld // CHUNK, HGRN_GROUP_SAMPLE)

    merge_w = (gs2, grn2, woa, wor, wout, g_ffn2, wr, br2)
    zero_cnt = jnp.zeros((1, N_EXPERTS), F32)
    x1_p, xn_p, idx_p, prob_p, rank_p, cnt_p = _merge(lam_init, ao_p, ro_p, zp, xp, *merge_w, zero_cnt)
    x1_s, xn_s, idx_s, prob_s, rank_s, cnt = _merge(lam_init, ao_s, ro_s, zs, xs_in, *merge_w, cnt_p)

    counts = cnt[0].astype(jnp.int32)
    starts = jnp.cumsum(counts) - counts
    experts = jnp.arange(N_EXPERTS, dtype=jnp.int32)

    def positions(idx, rank):
        return (jnp.sum(jnp.where(idx[..., None] == experts, starts, 0), axis=-1) + rank) * ROW_SUB

    pos_p = positions(idx_p, rank_p)
    pos_s2 = positions(idx_s, rank_s)
    n_rows = (tp + ts) * TOP_K
    plan = _expert_plan(counts, n_rows)

    xs_sorted = _dispatch(jnp.concatenate([pos_p, pos_s2], axis=0), xn_p, xn_s)
    ys = _experts(plan, xs_sorted, w_gate[0], b_gate[0], w_up[0], b_up[0], w_down[0], b_down[0])
    y_p = _combine(pos_p, x1_p, prob_p, g_fin2, ys)
    y_s = _combine(pos_s2, x1_s, prob_s, g_fin2, ys)

    return (y_p.reshape(nb, seq, D_MODEL), y_s.reshape(nbd, ld, D_MODEL),
            k_p.reshape(1, nb, seq, HEADS, HEAD_W), v_p.reshape(1, nb, seq, HEADS, HEAD_W), st_p[None],
            k_s.reshape(1, nbd, ld, HEADS, HEAD_W), v_s.reshape(1, nbd, ld, HEADS, HEAD_W), st_s[None])
```

```python
import functools
import math

import jax
import jax.numpy as jnp
from jax import lax
from jax.experimental import pallas as pl
from jax.experimental.pallas import tpu as pltpu

F32 = jnp.float32
BF16 = jnp.bfloat16

D_MODEL = 1024
CHUNK = 64
HEADS = 8
HEAD_W = 128
ATTN_HEAD_DIM = 64
ROPE_THETA = 10000.0
N_EXPERTS = 32
TOP_K = 4
SWIGLU_ALPHA = 1.702
SWIGLU_LIMIT = 7.0
NORM_EPS = 1e-6
N_COL_BLOCKS = 9
COL_Q, COL_K, COL_V, COL_F = range(4)
N_Z_BLOCKS = 7
ZC_Q, ZC_F, ZC_RQ, ZC_RI, ZC_RG, ZC_GA, ZC_GR = range(7)
PROJ_TM = 256

ATTN_TQ = 256
CACHE_COPY_PARTS = 4
HGRN_GROUP_PROMPT = 4
HGRN_GROUP_SAMPLE = 4
MERGE_TM = 512
MERGE_PARTS = 2
MOE_TM = 256
MOE_FF_SLICE = 256
ROW_TM = 256
ROW_COPY_UNROLL = 4
DISPATCH_SLOTS = 3
ROW_SUB = D_MODEL // HEAD_W
VMEM_LIMIT = 56 * 1024 * 1024


def _cparams(*sem):
    return pltpu.CompilerParams(dimension_semantics=sem, vmem_limit_bytes=VMEM_LIMIT)


def _dot(a, b):
    return jnp.dot(a, b, preferred_element_type=F32)


def _dot_nt(a, b):
    return lax.dot_general(a, b, (((1,), (1,)), ((), ())), preferred_element_type=F32)


def _proj_kernel(x_ref, g_ref, w_ref, cos_ref, sin_ref, lbl_ref,
                 z_ref, kout_ref, vout_ref, kb_ref, vb_ref):
    tm = x_ref.shape[0]
    x = x_ref[...]
    ms = jnp.mean(x * x, axis=-1, keepdims=True)
    hn = (x * lax.rsqrt(ms + NORM_EPS) * g_ref[...]).astype(BF16)

    def col(j):
        return _dot(hn, w_ref[:, j * D_MODEL:(j + 1) * D_MODEL])

    def zcols(c):
        return slice(c * D_MODEL, (c + 1) * D_MODEL)

    def head(h):
        return slice(h * HEAD_W, (h + 1) * HEAD_W)

    def head_rows(h):
        return pl.ds(h, tm, stride=HEADS)

    def rope(zh):
        lane = lax.broadcasted_iota(jnp.int32, zh.shape, 1)
        first = (lane % ATTN_HEAD_DIM) < (ATTN_HEAD_DIM // 2)
        partner = jnp.where(first, pltpu.roll(zh, HEAD_W - ATTN_HEAD_DIM // 2, 1),
                            pltpu.roll(zh, ATTN_HEAD_DIM // 2, 1))
        return zh * cos_ref[...] + partner * sin_ref[...]

    zq = col(COL_Q)
    for h in range(HEADS):
        z_ref[:, head(h)] = rope(zq[:, head(h)]) * (ATTN_HEAD_DIM ** -0.5)

    zk = col(COL_K)
    for h in range(HEADS):
        kh = rope(zk[:, head(h)])
        kout_ref[head_rows(h), :] = kh
        kb_ref[:, head(h)] = kh.astype(BF16)

    zv = col(COL_V)
    for h in range(HEADS):
        vout_ref[head_rows(h), :] = zv[:, head(h)]
    vb_ref[...] = zv.astype(BF16)

    lbl = lbl_ref[...]
    e = jnp.exp(lbl - jnp.max(lbl, axis=0, keepdims=True))
    lb = e[0:1] / jnp.sum(e, axis=0, keepdims=True)
    z_ref[:, zcols(ZC_F)] = lb + (1.0 - lb) * jax.nn.sigmoid(col(COL_F))

    for j in range(COL_F + 1, N_COL_BLOCKS):
        z_ref[:, zcols(j - COL_V)] = col(j)


def _project(x, g, w_bf16, cos_t, sin_t, lb_logits):
    t = x.shape[0]
    tm = PROJ_TM
    n_tab = cos_t.shape[0] // tm
    tok = lambda i: (i, 0)
    fixed = lambda i: (0, 0)
    return pl.pallas_call(
        _proj_kernel,
        grid=(t // tm,),
        in_specs=[
            pl.BlockSpec((tm, D_MODEL), tok),
            pl.BlockSpec((1, D_MODEL), fixed),
            pl.BlockSpec((D_MODEL, N_COL_BLOCKS * D_MODEL), fixed, pipeline_mode=pl.Buffered(1)),
            pl.BlockSpec((tm, HEAD_W), lambda i: (i % n_tab, 0)),
            pl.BlockSpec((tm, HEAD_W), lambda i: (i % n_tab, 0)),
            pl.BlockSpec((2, D_MODEL), fixed),
        ],
        out_specs=[
            pl.BlockSpec((tm, N_Z_BLOCKS * D_MODEL), tok),
            pl.BlockSpec((tm * HEADS, HEAD_W), tok),
            pl.BlockSpec((tm * HEADS, HEAD_W), tok),
            pl.BlockSpec((tm, D_MODEL), tok),
            pl.BlockSpec((tm, D_MODEL), tok),
        ],
        out_shape=[
            jax.ShapeDtypeStruct((t, N_Z_BLOCKS * D_MODEL), F32),
            jax.ShapeDtypeStruct((t * HEADS, HEAD_W), F32),
            jax.ShapeDtypeStruct((t * HEADS, HEAD_W), F32),
            jax.ShapeDtypeStruct((t, D_MODEL), BF16),
            jax.ShapeDtypeStruct((t, D_MODEL), BF16),
        ],
        compiler_params=_cparams("arbitrary"),
        name="proj",
    )(x, g, w_bf16, cos_t, sin_t, lb_logits)


def _rope_tables(pos):
    half = ATTN_HEAD_DIM // 2
    inv = jnp.power(ROPE_THETA, -2.0 * jnp.arange(half, dtype=F32) / ATTN_HEAD_DIM)
    ang = pos.astype(F32)[:, None] * inv[None, :]
    cos = jnp.tile(jnp.cos(ang), (1, HEAD_W // half))
    sin = jnp.sin(ang)
    sin = jnp.tile(jnp.concatenate([-sin, sin], axis=-1), (1, HEAD_W // ATTN_HEAD_DIM))
    return cos, sin


def _split_components(q):
    lane = lax.broadcasted_iota(jnp.int32, q.shape, 1)
    q1 = jnp.where(lane < ATTN_HEAD_DIM, q, 0.0)
    q2 = jnp.where(lane >= ATTN_HEAD_DIM, q, 0.0)
    return jnp.concatenate([q1, q2], axis=0).astype(BF16)


def _attn_prompt_kernel(lam_ref, q_ref, k_ref, v_ref, o_ref):
    lam = lam_ref[0]
    tq = ATTN_TQ
    seq = k_ref.shape[0]
    k = k_ref[...]
    v = v_ref[...]
    row = lax.broadcasted_iota(jnp.int32, (2 * tq, tq), 0)
    col = lax.broadcasted_iota(jnp.int32, (2 * tq, tq), 1)
    visible = (col // CHUNK) <= ((row % tq) // CHUNK)
    n_tiles = seq // tq

    def scores(qi):
        lo = qi * tq
        qq = _split_components(q_ref[lo:lo + tq, :])
        s_d = jnp.where(visible, _dot_nt(qq, k[lo:lo + tq]), -jnp.inf)
        return s_d, (_dot_nt(qq, k[:lo]) if qi > 0 else None)

    def weighted_values(qi, w_d, w_m):
        lo = qi * tq
        o = _dot(w_d, v[lo:lo + tq])
        if qi > 0:
            o = o + _dot(w_m, v[:lo])
        o_ref[lo:lo + tq, :] = o

    ahead = scores(0)
    behind = None
    for qi in range(n_tiles):
        s_d, s_m = ahead
        if qi + 1 < n_tiles:
            ahead = scores(qi + 1)
        if behind is not None:
            weighted_values(*behind)
        m = jnp.max(s_d, axis=-1, keepdims=True)
        if qi > 0:
            m = jnp.maximum(m, jnp.max(s_m, axis=-1, keepdims=True))
            p_m = jnp.exp(s_m - m)
        p_d = jnp.exp(s_d - m)
        l = jnp.sum(p_d, axis=-1, keepdims=True)
        if qi > 0:
            l = l + jnp.sum(p_m, axis=-1, keepdims=True)
        r = 1.0 / l
        r1 = r[:tq]
        r2 = lam * r[tq:]
        w_d = (p_d[:tq] * r1 - p_d[tq:] * r2).astype(BF16)
        w_m = (p_m[:tq] * r1 - p_m[tq:] * r2).astype(BF16) if qi > 0 else None
        behind = (qi, w_d, w_m)
    weighted_values(*behind)


def _attn_prompt(lam, z, kb, vb, n_batch, seq):
    return pl.pallas_call(
        _attn_prompt_kernel,
        grid=(n_batch, HEADS),
        in_specs=[
            pl.BlockSpec(memory_space=pltpu.SMEM),
            pl.BlockSpec((seq, HEAD_W), lambda b, h: (b, ZC_Q * HEADS + h)),
            pl.BlockSpec((seq, HEAD_W), lambda b, h: (b, h)),
            pl.BlockSpec((seq, HEAD_W), lambda b, h: (b, h)),
        ],
        out_specs=pl.BlockSpec((seq, HEAD_W), lambda b, h: (b, h)),
        out_shape=jax.ShapeDtypeStruct((n_batch * seq, D_MODEL), F32),
        compiler_params=_cparams("arbitrary", "arbitrary"),
        name="attn_prompt",
    )(lam, z, kb, vb)


def _attn_sample_kernel(n_steps, lam_ref, q_ref, kn_ref, vn_ref, ck_hbm, cv_hbm, o_ref, kbuf, vbuf, sem):
    b = pl.program_id(0)
    h = pl.program_id(1)
    step = b * HEADS + h
    slot = step % 2

    def cache_copies(bb, hh, sl):
        part = kbuf.shape[1] // CACHE_COPY_PARTS
        copies = []
        for j in range(CACHE_COPY_PARTS):
            keys = pl.ds(j * part, part)
            copies.append(pltpu.make_async_copy(ck_hbm.at[bb, keys, hh, :], kbuf.at[sl, keys], sem.at[0, sl]))
            copies.append(pltpu.make_async_copy(cv_hbm.at[bb, keys, hh, :], vbuf.at[sl, keys], sem.at[1, sl]))
        return copies

    @pl.when(step == 0)
    def _():
        for c in cache_copies(b, h, slot):
            c.start()

    @pl.when(step + 1 < n_steps)
    def _():
        nxt = step + 1
        for c in cache_copies(nxt // HEADS, nxt % HEADS, 1 - slot):
            c.start()

    for c in cache_copies(b, h, slot):
        c.wait()
    ck_ref = kbuf.at[slot]
    cv_ref = vbuf.at[slot]

    lam = lam_ref[0]
    n_q = q_ref.shape[0]
    qq = _split_components(q_ref[...])
    s_c = _dot_nt(qq, ck_ref[...].astype(BF16))
    s_n = _dot_nt(qq, kn_ref[...])
    m = jnp.maximum(jnp.max(s_c, axis=-1, keepdims=True), jnp.max(s_n, axis=-1, keepdims=True))
    p_c = jnp.exp(s_c - m)
    p_n = jnp.exp(s_n - m)
    r = 1.0 / (jnp.sum(p_c, axis=-1, keepdims=True) + jnp.sum(p_n, axis=-1, keepdims=True))
    r1 = r[:n_q]
    r2 = lam * r[n_q:]
    w_c = (p_c[:n_q] * r1 - p_c[n_q:] * r2).astype(BF16)
    w_n = (p_n[:n_q] * r1 - p_n[n_q:] * r2).astype(BF16)
    o_ref[...] = _dot(w_c, cv_ref[...].astype(BF16)) + _dot(w_n, vn_ref[...])


def _attn_sample(lam, z, kb, vb, cache_k, cache_v, n_batch, n_q):
    past = cache_k.shape[1]
    cache_spec = pl.BlockSpec(memory_space=pl.ANY)
    return pl.pallas_call(
        functools.partial(_attn_sample_kernel, n_batch * HEADS),
        grid=(n_batch, HEADS),
        in_specs=[
            pl.BlockSpec(memory_space=pltpu.SMEM),
            pl.BlockSpec((n_q, HEAD_W), lambda b, h: (b, ZC_Q * HEADS + h)),
            pl.BlockSpec((n_q, HEAD_W), lambda b, h: (b, h)),
            pl.BlockSpec((n_q, HEAD_W), lambda b, h: (b, h)),
            cache_spec, cache_spec,
        ],
        out_specs=pl.BlockSpec((n_q, HEAD_W), lambda b, h: (b, h)),
        out_shape=jax.ShapeDtypeStruct((n_batch * n_q, D_MODEL), F32),
        scratch_shapes=[
            pltpu.VMEM((2, past, HEAD_W), F32),
            pltpu.VMEM((2, past, HEAD_W), F32),
            pltpu.SemaphoreType.DMA((2, 2)),
        ],
        compiler_params=_cparams("arbitrary", "arbitrary"),
        name="attn_sample",
    )(lam, z, kb, vb, cache_k, cache_v)


def _split3(x):
    h1 = x.astype(BF16)
    r1 = x - h1.astype(F32)
    h2 = r1.astype(BF16)
    h3 = (r1 - h2.astype(F32)).astype(BF16)
    return h1, h2, h3


def _when_step(step, n_steps, which):
    def deco(body):
        if n_steps == 1:
            body()
        else:
            pl.when(step == which)(body)
    return deco


def _hgrn_kernel(n_chunks, f_ref, q_ref, i_ref, s0_ref, o_ref, sout_ref, st_ref):
    c = pl.program_id(1)
    n_group = f_ref.shape[0]

    @_when_step(c, n_chunks, 0)
    def _():
        for g in range(n_group):
            for h in range(HEADS):
                st_ref[g, h] = s0_ref[g, h].T

    row = lax.broadcasted_iota(jnp.int32, (CHUNK, CHUNK), 0)
    col = lax.broadcasted_iota(jnp.int32, (CHUNK, CHUNK), 1)
    causal = row >= col
    tril = jnp.where(causal, 1.0, 0.0).astype(BF16)
    heads = [slice(h * HEAD_W, (h + 1) * HEAD_W) for h in range(HEADS)]
    groups = range(n_group)
    fs = [f_ref[g] for g in groups]
    splits = [_split3(jnp.log(f)) for f in fs]
    bs = [_dot(tril, h1) + _dot(tril, h2) + _dot(tril, h3) for h1, h2, h3 in splits]
    scaled = []
    for g in groups:
        b = bs[g]
        b_last = b[CHUNK - 1:CHUNK, :]
        rk = 1.0 - fs[g]
        v = i_ref[g]
        scaled.append(((q_ref[g] * jnp.exp(b)).astype(BF16), (rk * jnp.exp(-b)).astype(BF16),
                       (rk * jnp.exp(b_last - b)).astype(BF16), v, v.astype(BF16), jnp.exp(b_last)))
    first = []
    for g in groups:
        qd_b, kd_b, kl_b, v, v_b, eb_last = scaled[g]
        states = [st_ref[g, h] for h in range(HEADS)]
        a_all = [_dot_nt(qd_b[:, sl], kd_b[:, sl]) for sl in heads]
        o_state = [_dot_nt(qd_b[:, sl], st.astype(BF16)) for sl, st in zip(heads, states)]
        s_upd = [_dot(v[:, sl].T.astype(BF16), kl_b[:, sl]) for sl in heads]
        first.append((states, a_all, o_state, s_upd))
    for g in groups:
        v_b, eb_last = scaled[g][4], scaled[g][5]
        states, a_all, o_state, s_upd = first[g]
        for h, sl in enumerate(heads):
            a = jnp.where(causal, a_all[h], 0.0).astype(BF16)
            o_ref[g, :, sl] = o_state[h] + _dot(a, v_b[:, sl])
            st_ref[g, h] = states[h] * eb_last[:, sl] + s_upd[h]

    @_when_step(c, n_chunks, n_chunks - 1)
    def _():
        for g in range(n_group):
            for h in range(HEADS):
                sout_ref[g, h] = st_ref[g, h].T


def _hgrn(z, s0, n_batch, n_chunks, n_group):
    z3 = z.reshape(n_batch, n_chunks * CHUNK, N_Z_BLOCKS * D_MODEL)

    def zblock(zc):
        return pl.BlockSpec((n_group, CHUNK, D_MODEL), lambda b, c: (b, c, zc))

    state = pl.BlockSpec((n_group, HEADS, HEAD_W, HEAD_W), lambda b, c: (b, 0, 0, 0))
    ro, st = pl.pallas_call(
        functools.partial(_hgrn_kernel, n_chunks),
        grid=(n_batch // n_group, n_chunks),
        in_specs=[zblock(ZC_F), zblock(ZC_RQ), zblock(ZC_RI), state],
        out_specs=[pl.BlockSpec((n_group, CHUNK, D_MODEL), lambda b, c: (b, c, 0)), state],
        out_shape=[
            jax.ShapeDtypeStruct((n_batch, n_chunks * CHUNK, D_MODEL), F32),
            jax.ShapeDtypeStruct((n_batch, HEADS, HEAD_W, HEAD_W), F32),
        ],
        scratch_shapes=[pltpu.VMEM((n_group, HEADS, HEAD_W, HEAD_W), F32)],
        compiler_params=_cparams("arbitrary", "arbitrary"),
        name="hgrn",
    )(z3, z3, z3, s0)
    return ro.reshape(n_batch * n_chunks * CHUNK, D_MODEL), st


def _head_rmsnorm(x, g):
    outs = []
    for h in range(HEADS):
        xh = x[:, h * HEAD_W:(h + 1) * HEAD_W]
        ms = jnp.mean(xh * xh, axis=-1, keepdims=True)
        outs.append(xh * lax.rsqrt(ms + NORM_EPS) * g)
    return jnp.concatenate(outs, axis=-1)


def _merge_kernel(lam_init, ao_ref, ro_ref, rg_ref, ga_ref, gr_ref, x_ref, gs_ref, grn_ref,
                  woa_ref, wor_ref, wout_ref, gffn_ref, wr_ref, br_ref, cnt_in_ref,
                  x1_ref, xn_ref, idx_ref, prob_ref, rank_ref, cnt_ref, run_ref):
    i = pl.program_id(0)
    tm = x_ref.shape[0]

    @pl.when(i == 0)
    def _():
        run_ref[...] = cnt_in_ref[...]

    part = tm // MERGE_PARTS

    def rows(p):
        return slice(p * part, (p + 1) * part)

    def norm_inputs(p, _):
        an = (_head_rmsnorm(ao_ref[rows(p), :], gs_ref[...]) * (1.0 - lam_init)).astype(BF16)
        rn = (_head_rmsnorm(ro_ref[rows(p), :], grn_ref[...]) * jax.nn.silu(rg_ref[rows(p), :])).astype(BF16)
        return an, rn

    def branch_projections(p, st):
        an, rn = st
        return _dot(an, woa_ref[...]), _dot(rn, wor_ref[...])

    def gate(p, st):
        a, r = st
        return (jax.nn.sigmoid(ga_ref[rows(p), :]) * a + jax.nn.sigmoid(gr_ref[rows(p), :]) * r).astype(BF16)

    def residual(p, mixed):
        x1 = x_ref[rows(p), :] + _dot(mixed, wout_ref[...])
        x1_ref[rows(p), :] = x1
        return x1

    def ffn_norm_router(p, x1):
        ms = jnp.mean(x1 * x1, axis=-1, keepdims=True)
        xn = x1 * lax.rsqrt(ms + NORM_EPS) * gffn_ref[...]
        for s in range(ROW_SUB):
            xn_ref[pl.ds(p * part * ROW_SUB + s, part, stride=ROW_SUB), :] = xn[:, s * HEAD_W:(s + 1) * HEAD_W]
        return _dot(xn.astype(BF16), wr_ref[...])

    stages = (norm_inputs, branch_projections, gate, residual, ffn_norm_router)
    state = [None] * MERGE_PARTS
    for stage in stages:
        for p in range(MERGE_PARTS):
            state[p] = stage(p, state[p])
    logits = jnp.concatenate(state, axis=0) + br_ref[...]
    lane = lax.broadcasted_iota(jnp.int32, logits.shape, 1).astype(F32)
    sel = jnp.zeros(logits.shape, F32)
    work = logits
    tops, idxs = [], []
    for _ in range(TOP_K):
        m = jnp.max(work, axis=-1, keepdims=True)
        idx = jnp.min(jnp.where(work == m, lane, float(N_EXPERTS)), axis=-1, keepdims=True)
        hit = lane == idx
        sel = jnp.where(hit, 1.0, sel)
        work = jnp.where(hit, -jnp.inf, work)
        tops.append(m)
        idxs.append(idx)
    es = [jnp.exp(v - tops[0]) for v in tops]
    inv = 1.0 / (es[0] + es[1] + es[2] + es[3])

    trow = lax.broadcasted_iota(jnp.int32, (tm, tm), 0)
    tcol = lax.broadcasted_iota(jnp.int32, (tm, tm), 1)
    before = jnp.where(trow > tcol, 1.0, 0.0).astype(BF16)
    ranks = _dot(before, sel.astype(BF16)) + run_ref[...]
    k_lane = lax.broadcasted_iota(jnp.int32, (tm, TOP_K), 1)
    idx_out = jnp.zeros((tm, TOP_K), F32)
    prob_out = jnp.zeros((tm, TOP_K), F32)
    rank_out = jnp.zeros((tm, TOP_K), F32)
    for k in range(TOP_K):
        rk = jnp.sum(jnp.where(lane == idxs[k], ranks, 0.0), axis=-1, keepdims=True)
        idx_out = jnp.where(k_lane == k, idxs[k], idx_out)
        prob_out = jnp.where(k_lane == k, es[k] * inv, prob_out)
        rank_out = jnp.where(k_lane == k, rk, rank_out)
    idx_ref[...] = idx_out.astype(jnp.int32)
    prob_ref[...] = prob_out
    rank_ref[...] = rank_out.astype(jnp.int32)
    run = run_ref[...] + jnp.sum(sel, axis=0, keepdims=True)
    run_ref[...] = run
    cnt_ref[...] = run


def _merge(lam_init, ao, ro, z, x, g_subln, g_rec, woa, wor, wout, g_ffn, w_router, b_router, cnt_in):
    t = x.shape[0]
    tm = MERGE_TM
    row = lambda i: (i, 0)
    fixed = lambda i: (0, 0)
    tok = pl.BlockSpec((tm, D_MODEL), row)
    wspec = pl.BlockSpec((D_MODEL, D_MODEL), fixed)
    narrow = pl.BlockSpec((tm, TOP_K), row)
    return pl.pallas_call(
        functools.partial(_merge_kernel, lam_init),
        grid=(t // tm,),
        in_specs=[
            tok, tok,
            pl.BlockSpec((tm, D_MODEL), lambda i: (i, ZC_RG)),
            pl.BlockSpec((tm, D_MODEL), lambda i: (i, ZC_GA)),
            pl.BlockSpec((tm, D_MODEL), lambda i: (i, ZC_GR)),
            tok,
            pl.BlockSpec((1, HEAD_W), fixed), pl.BlockSpec((1, HEAD_W), fixed),
            wspec, wspec, wspec,
            pl.BlockSpec((1, D_MODEL), fixed),
            pl.BlockSpec((D_MODEL, N_EXPERTS), fixed),
            pl.BlockSpec((1, N_EXPERTS), fixed),
            pl.BlockSpec((1, N_EXPERTS), fixed),
        ],
        out_specs=[tok, pl.BlockSpec((tm * ROW_SUB, HEAD_W), row), narrow, narrow, narrow,
                   pl.BlockSpec((1, N_EXPERTS), fixed)],
        out_shape=[
            jax.ShapeDtypeStruct((t, D_MODEL), F32),
            jax.ShapeDtypeStruct((t * ROW_SUB, HEAD_W), F32),
            jax.ShapeDtypeStruct((t, TOP_K), jnp.int32),
            jax.ShapeDtypeStruct((t, TOP_K), F32),
            jax.ShapeDtypeStruct((t, TOP_K), jnp.int32),
            jax.ShapeDtypeStruct((1, N_EXPERTS), F32),
        ],
        scratch_shapes=[pltpu.VMEM((1, N_EXPERTS), F32)],
        compiler_params=_cparams("arbitrary"),
        name="merge",
    )(ao, ro, z, z, z, x, g_subln, g_rec, woa, wor, wout, g_ffn, w_router, b_router, cnt_in)


def _row_copy(src, dst, sem):
    return pltpu.make_async_copy(src, dst, sem)


def _dispatch_kernel(n_first, n_steps, pos_ref, xa_hbm, xb_hbm, xs_hbm, stage, load_sem, copy_sem):
    i = pl.program_id(0)
    tm = ROW_TM
    rows = tm * ROW_SUB
    slot = i % DISPATCH_SLOTS

    def row_tile(off):
        return pl.ds(pl.multiple_of(off, ROW_SUB), ROW_SUB)

    def load(t, sl):
        @pl.when(t < n_first)
        def _():
            _row_copy(xa_hbm.at[pl.ds(pl.multiple_of(t * rows, rows), rows)], stage.at[sl], load_sem.at[sl]).start()

        @pl.when(t >= n_first)
        def _():
            _row_copy(xb_hbm.at[pl.ds(pl.multiple_of((t - n_first) * rows, rows), rows)], stage.at[sl],
                      load_sem.at[sl]).start()

    def drain(parity):
        for k in range(TOP_K):
            _row_copy(stage.at[0], xs_hbm.at[pl.ds(0, rows)], copy_sem.at[parity]).wait()

    @pl.when(i == 0)
    def _():
        load(i, slot)

    @pl.when(i + 1 < n_steps)
    def _():
        load(i + 1, (i + 1) % DISPATCH_SLOTS)

    _row_copy(xa_hbm.at[pl.ds(0, rows)], stage.at[slot], load_sem.at[slot]).wait()

    def start(g, carry):
        r0 = g * ROW_COPY_UNROLL
        n = ROW_COPY_UNROLL * TOP_K
        dst = [pos_ref[0, 0, r0 * TOP_K + j] for j in range(n)]
        for j in range(n):
            _row_copy(stage.at[slot, row_tile((r0 + j // TOP_K) * ROW_SUB)],
                      xs_hbm.at[row_tile(dst[j])], copy_sem.at[i % 2]).start()
        return carry

    lax.fori_loop(0, tm // ROW_COPY_UNROLL, start, 0)

    @pl.when(i > 0)
    def _():
        drain((i - 1) % 2)

    @pl.when(i == n_steps - 1)
    def _():
        drain(i % 2)


def _dispatch(pos, xn_a, xn_b):
    tm = ROW_TM
    n_a, n_b = xn_a.shape[0] // (tm * ROW_SUB), xn_b.shape[0] // (tm * ROW_SUB)
    pos3 = pos.reshape(n_a + n_b, 1, tm * TOP_K)
    return pl.pallas_call(
        functools.partial(_dispatch_kernel, n_a, n_a + n_b),
        grid=(n_a + n_b,),
        in_specs=[
            pl.BlockSpec((1, 1, tm * TOP_K), lambda i: (i, 0, 0), memory_space=pltpu.SMEM),
            pl.BlockSpec(memory_space=pl.ANY),
            pl.BlockSpec(memory_space=pl.ANY),
        ],
        out_specs=pl.BlockSpec(memory_space=pl.ANY),
        out_shape=jax.ShapeDtypeStruct((pos.size * ROW_SUB, HEAD_W), F32),
        scratch_shapes=[
            pltpu.VMEM((DISPATCH_SLOTS, tm * ROW_SUB, HEAD_W), F32),
            pltpu.SemaphoreType.DMA((DISPATCH_SLOTS,)),
            pltpu.SemaphoreType.DMA((2,)),
        ],
        compiler_params=_cparams("arbitrary"),
        name="dispatch",
    )(pos3, xn_a, xn_b)


def _experts_kernel(tile_ref, exp_ref, lo_ref, hi_ref, first_ref, fresh_ref, par_ref, nxt_ref, more_ref,
                    xs_ref, wg_hbm, bg_ref, wu_hbm, bu_ref, wd_hbm, bd_ref, o_ref,
                    wg_s, wu_s, wd_s, x_s, wf_s, wsem):
    w = pl.program_id(0)
    lo = lo_ref[w]
    hi = hi_ref[w]
    tm = x_s.shape[0]

    def lanes(s):
        return slice(s * HEAD_W, (s + 1) * HEAD_W)

    def sub_rows(s):
        return pl.ds(s, tm, stride=ROW_SUB)

    def weight_copies(e, sl):
        return [pltpu.make_async_copy(src.at[e], wf_s.at[sl, j], wsem.at[sl])
                for j, src in enumerate((wg_hbm, wu_hbm, wd_hbm))]

    @pl.when(w == 0)
    def _():
        for c in weight_copies(exp_ref[0], par_ref[0]):
            c.start()

    @pl.when(fresh_ref[w] == 1)
    def _():
        sl = par_ref[w]
        for c in weight_copies(exp_ref[w], sl):
            c.wait()

        @pl.when(more_ref[w] == 1)
        def _():
            for c in weight_copies(nxt_ref[w], 1 - sl):
                c.start()

        wg_s[...] = wf_s[sl, 0].astype(BF16)
        wu_s[...] = wf_s[sl, 1].astype(BF16)
        wd_s[...] = wf_s[sl, 2].astype(BF16)

    @pl.when(hi > lo)
    def _():
        for s in range(ROW_SUB):
            x_s[:, lanes(s)] = xs_ref[sub_rows(s), :].astype(BF16)
        x = x_s[...]
        out = bd_ref[0]
        def gate_up(c):
            ff = slice(c, c + MOE_FF_SLICE)
            return _dot(x, wg_s[:, ff]) + bg_ref[0, :, ff], _dot(x, wu_s[:, ff]) + bu_ref[0, :, ff]

        slices = range(0, D_MODEL, MOE_FF_SLICE)
        gus = [gate_up(c) for c in slices]
        acts = []
        for gate, up in gus:
            gate = jnp.minimum(gate, SWIGLU_LIMIT)
            up = jnp.clip(up, -SWIGLU_LIMIT, SWIGLU_LIMIT)
            acts.append((gate * jax.nn.sigmoid(SWIGLU_ALPHA * gate) * (up + 1.0)).astype(BF16))
        for c, act in zip(slices, acts):
            out = out + _dot(act, wd_s[c:c + MOE_FF_SLICE, :])
        row = lax.broadcasted_iota(jnp.int32, (tm, 1), 0)
        mine = (row >= lo) & (row < hi)

        @pl.when(first_ref[w] == 1)
        def _():
            for s in range(ROW_SUB):
                o_ref[sub_rows(s), :] = jnp.where(mine, out[:, lanes(s)], 0.0)

        @pl.when(first_ref[w] == 0)
        def _():
            for s in range(ROW_SUB):
                o_ref[sub_rows(s), :] = jnp.where(mine, out[:, lanes(s)], o_ref[sub_rows(s), :])


def _experts(plan, xs, w_gate, b_gate, w_up, b_up, w_down, b_down):
    tm = MOE_TM
    n_items = plan[0].shape[0]
    rows = lambda w, tile, *_: (tile[w], 0)
    wspec = pl.BlockSpec(memory_space=pl.ANY)
    bspec = pl.BlockSpec((1, 1, D_MODEL), lambda w, tile, exp, *_: (exp[w], 0, 0))
    grid_spec = pltpu.PrefetchScalarGridSpec(
        num_scalar_prefetch=len(plan),
        grid=(n_items,),
        in_specs=[pl.BlockSpec((tm * ROW_SUB, HEAD_W), rows), wspec, bspec, wspec, bspec, wspec, bspec],
        out_specs=pl.BlockSpec((tm * ROW_SUB, HEAD_W), rows),
        scratch_shapes=[pltpu.VMEM((D_MODEL, D_MODEL), BF16)] * 3 + [
            pltpu.VMEM((tm, D_MODEL), BF16),
            pltpu.VMEM((2, 3, D_MODEL, D_MODEL), F32),
            pltpu.SemaphoreType.DMA((2,)),
        ],
    )
    return pl.pallas_call(
        _experts_kernel,
        grid_spec=grid_spec,
        out_shape=jax.ShapeDtypeStruct(xs.shape, F32),
        compiler_params=_cparams("arbitrary"),
        name="experts",
    )(*plan, xs, w_gate, b_gate.reshape(N_EXPERTS, 1, D_MODEL), w_up, b_up.reshape(N_EXPERTS, 1, D_MODEL),
      w_down, b_down.reshape(N_EXPERTS, 1, D_MODEL))


def _expert_plan(counts, n_rows):
    tm = MOE_TM
    n_tiles = n_rows // tm
    n_items = n_tiles + N_EXPERTS - 1
    ends = jnp.cumsum(counts)
    starts = ends - counts
    first_tile = starts // tm
    last_tile = jnp.maximum(ends - 1, 0) // tm
    items = jnp.where(counts > 0, last_tile - first_tile + 1, 0)
    item_end = jnp.cumsum(items)
    item_start = item_end - items
    total = item_end[-1]
    w = jnp.arange(n_items, dtype=jnp.int32)
    wc = jnp.minimum(w, total - 1)
    exp = jnp.sum((item_end[None, :] <= wc[:, None]).astype(jnp.int32), axis=1)
    ids = jnp.arange(N_EXPERTS, dtype=jnp.int32)
    mine = exp[:, None] == ids[None, :]

    def of_item(per_expert):
        return jnp.sum(jnp.where(mine, per_expert[None, :], 0), axis=1).astype(jnp.int32)

    tile = of_item(first_tile) + wc - of_item(item_start)
    lo = jnp.maximum(of_item(starts), tile * tm) - tile * tm
    hi = jnp.minimum(of_item(ends), (tile + 1) * tm) - tile * tm
    valid = w < total
    lo = jnp.where(valid, lo, 0).astype(jnp.int32)
    hi = jnp.where(valid, hi, 0).astype(jnp.int32)
    prev_tile = jnp.concatenate([jnp.full((1,), -1, jnp.int32), tile[:-1]])
    prev_exp = jnp.concatenate([jnp.full((1,), -1, jnp.int32), exp[:-1]])
    first = (valid & (tile != prev_tile)).astype(jnp.int32)
    fresh = (valid & (exp != prev_exp)).astype(jnp.int32)
    later = jnp.where((counts > 0)[None, :] & (ids[None, :] > ids[:, None]), ids[None, :], N_EXPERTS)
    next_exp = jnp.min(later, axis=1)
    parity = ((jnp.cumsum((counts > 0).astype(jnp.int32)) - 1) % 2).astype(jnp.int32)
    next_of_item = of_item(next_exp)
    more = (next_of_item < N_EXPERTS).astype(jnp.int32)
    nxt = jnp.where(more == 1, next_of_item, exp).astype(jnp.int32)
    return tile, exp, lo, hi, first, fresh, of_item(parity), nxt, more


def _combine_kernel(n_steps, pos_ref, pos_next_ref, x1_ref, prob_ref, gfin_ref, ys_ref, y_ref, buf_ref, sem):
    i = pl.program_id(0)
    tm = x1_ref.shape[0]
    slot = i % 2

    def row_tile(off):
        return pl.ds(pl.multiple_of(off, ROW_SUB), ROW_SUB)

    def gather(p_ref, sl):
        def start(g, carry):
            r0 = g * ROW_COPY_UNROLL
            n = ROW_COPY_UNROLL * TOP_K
            src = [p_ref[0, 0, r0 * TOP_K + j] for j in range(n)]
            for j in range(n):
                _row_copy(ys_ref.at[row_tile(src[j])],
                          buf_ref.at[sl, j % TOP_K, row_tile((r0 + j // TOP_K) * ROW_SUB)], sem.at[sl]).start()
            return carry

        lax.fori_loop(0, tm // ROW_COPY_UNROLL, start, 0)

    @pl.when(i == 0)
    def _():
        gather(pos_ref, slot)

    @pl.when(i + 1 < n_steps)
    def _():
        gather(pos_next_ref, 1 - slot)

    for k in range(TOP_K):
        _row_copy(ys_ref.at[pl.ds(0, tm * ROW_SUB)], buf_ref.at[slot, k], sem.at[slot]).wait()

    prob = prob_ref[...]
    parts = []
    for s in range(ROW_SUB):
        part = x1_ref[:, s * HEAD_W:(s + 1) * HEAD_W]
        for k in range(TOP_K):
            part = part + prob[:, k:k + 1] * buf_ref[slot, k, pl.ds(s, tm, stride=ROW_SUB), :]
        parts.append(part)
    x2 = jnp.concatenate(parts, axis=-1)
    ms = jnp.mean(x2 * x2, axis=-1, keepdims=True)
    y_ref[...] = x2 * lax.rsqrt(ms + NORM_EPS) * gfin_ref[...]


def _combine(pos, x1, prob, g_final, ys):
    t = x1.shape[0]
    tm = ROW_TM
    n = t // tm
    pos3 = pos.reshape(n, 1, tm * TOP_K)
    return pl.pallas_call(
        functools.partial(_combine_kernel, n),
        grid=(n,),
        in_specs=[
            pl.BlockSpec((1, 1, tm * TOP_K), lambda i: (i, 0, 0), memory_space=pltpu.SMEM),
            pl.BlockSpec((1, 1, tm * TOP_K), lambda i: (jnp.minimum(i + 1, n - 1), 0, 0), memory_space=pltpu.SMEM),
            pl.BlockSpec((tm, D_MODEL), lambda i: (i, 0)),
            pl.BlockSpec((tm, TOP_K), lambda i: (i, 0)),
            pl.BlockSpec((1, D_MODEL), lambda i: (0, 0)),
            pl.BlockSpec(memory_space=pl.ANY),
        ],
        out_specs=pl.BlockSpec((tm, D_MODEL), lambda i: (i, 0)),
        out_shape=jax.ShapeDtypeStruct((t, D_MODEL), F32),
        scratch_shapes=[pltpu.VMEM((2, TOP_K, tm * ROW_SUB, HEAD_W), F32), pltpu.SemaphoreType.DMA((2,))],
        compiler_params=_cparams("arbitrary"),
        name="combine",
    )(pos3, pos3, x1, prob, g_final, ys)


def _lambda_init(layer):
    return 0.8 - 0.6 * math.exp(-0.3 * layer)


def kernel(x_prompt, x_sample, cache_k, cache_v, state_rec, g_mix, w_in, lambda_q1, lambda_k1, lambda_q2, lambda_k2, g_subln, w_o_attn, lb_logits, g_rec_norm, w_o_rec, w_out, g_ffn, w_router, b_router, w_gate, b_gate, w_up, b_up, w_down, b_down, g_final):
    nb, seq, _ = x_prompt.shape
    nbd, ld, _ = x_sample.shape
    past = cache_k.shape[2]
    tp, ts = nb * seq, nbd * ld
    lam_init = _lambda_init(0)
    lam = (jnp.exp(jnp.sum(lambda_q1[0].astype(F32) * lambda_k1[0].astype(F32)))
           - jnp.exp(jnp.sum(lambda_q2[0].astype(F32) * lambda_k2[0].astype(F32))) + lam_init).reshape(1)

    w_in_b = w_in[0].astype(BF16)
    woa, wor, wout = w_o_attn[0].astype(BF16), w_o_rec[0].astype(BF16), w_out[0].astype(BF16)
    wr = w_router[0].astype(BF16)
    g_mix2, g_ffn2, g_fin2 = g_mix[0].reshape(1, -1), g_ffn[0].reshape(1, -1), g_final.reshape(1, -1)
    gs2, grn2 = g_subln[0].reshape(1, -1), g_rec_norm[0].reshape(1, -1)
    br2 = b_router[0].reshape(1, -1)

    cos_p, sin_p = _rope_tables(jnp.arange(seq, dtype=jnp.int32))
    pos_s = past + (jnp.arange(PROJ_TM, dtype=jnp.int32) % ld)
    cos_s, sin_s = _rope_tables(pos_s)

    xp = x_prompt.reshape(tp, D_MODEL)
    xs_in = x_sample.reshape(ts, D_MODEL)
    zp, k_p, v_p, kb_p, vb_p = _project(xp, g_mix2, w_in_b, cos_p, sin_p, lb_logits)
    zs, k_s, v_s, kb_s, vb_s = _project(xs_in, g_mix2, w_in_b, cos_s, sin_s, lb_logits)

    ao_p = _attn_prompt(lam, zp, kb_p, vb_p, nb, seq)
    ao_s = _attn_sample(lam, zs, kb_s, vb_s, cache_k[0], cache_v[0], nbd, ld)
    ro_p, st_p = _hgrn(zp, jnp.zeros((nb, HEADS, HEAD_W, HEAD_W), F32), nb, seq // CHUNK, HGRN_GROUP_PROMPT)
    ro_s, st_s = _hgrn(zs, state_rec[0], nbd, ld // CHUNK, HGRN_GROUP_SAMPLE)

    merge_w = (gs2, grn2, woa, wor, wout, g_ffn2, wr, br2)
    zero_cnt = jnp.zeros((1, N_EXPERTS), F32)
    x1_p, xn_p, idx_p, prob_p, rank_p, cnt_p = _merge(lam_init, ao_p, ro_p, zp, xp, *merge_w, zero_cnt)
    x1_s, xn_s, idx_s, prob_s, rank_s, cnt = _merge(lam_init, ao_s, ro_s, zs, xs_in, *merge_w, cnt_p)

    counts = cnt[0].astype(jnp.int32)
    starts = jnp.cumsum(counts) - counts
    experts = jnp.arange(N_EXPERTS, dtype=jnp.int32)

    def positions(idx, rank):
        return (jnp.sum(jnp.where(idx[..., None] == experts, starts, 0), axis=-1) + rank) * ROW_SUB

    pos_p = positions(idx_p, rank_p)
    pos_s2 = positions(idx_s, rank_s)
    n_rows = (tp + ts) * TOP_K
    plan = _expert_plan(counts, n_rows)

    xs_sorted = _dispatch(jnp.concatenate([pos_p, pos_s2], axis=0), xn_p, xn_s)
    ys = _experts(plan, xs_sorted, w_gate[0], b_gate[0], w_up[0], b_up[0], w_down[0], b_down[0])
    y_p = _combine(pos_p, x1_p, prob_p, g_fin2, ys)
    y_s = _combine(pos_s2, x1_s, prob_s, g_fin2, ys)

    return (y_p.reshape(nb, seq, D_MODEL), y_s.reshape(nbd, ld, D_MODEL),
            k_p.reshape(1, nb, seq, HEADS, HEAD_W), v_p.reshape(1, nb, seq, HEADS, HEAD_W), st_p[None],
            k_s.reshape(1, nbd, ld, HEADS, HEAD_W), v_s.reshape(1, nbd, ld, HEADS, HEAD_W), st_s[None])
```

```python
import functools
import math

import jax
import jax.numpy as jnp
from jax import lax
from jax.experimental import pallas as pl
from jax.experimental.pallas import tpu as pltpu

F32 = jnp.float32
BF16 = jnp.bfloat16

D_MODEL = 1024
CHUNK = 64
HEADS = 8
HEAD_W = 128
ATTN_HEAD_DIM = 64
ROPE_THETA = 10000.0
Q_SCALE = ATTN_HEAD_DIM ** -0.5 * math.log2(math.e)
N_EXPERTS = 32
TOP_K = 4
SWIGLU_ALPHA = 1.702
SWIGLU_LIMIT = 7.0
NORM_EPS = 1e-6
N_COL_BLOCKS = 9
COL_Q, COL_K, COL_V, COL_F = range(4)
N_Z_BLOCKS = 7
ZC_Q, ZC_F, ZC_RQ, ZC_RI, ZC_RG, ZC_GA, ZC_GR = range(7)
PROJ_TM = 256

ATTN_TQ = 256
CACHE_COPY_PARTS = 4
HGRN_GROUP_PROMPT = 4
HGRN_GROUP_SAMPLE = 4
MERGE_TM = 512
MERGE_PARTS = 2
MOE_TM = 256
MOE_FF_SLICE = 256
ROW_TM = 256
ROW_COPY_UNROLL = 4
DISPATCH_SLOTS = 3
ROW_SUB = D_MODEL // HEAD_W
VMEM_LIMIT = 56 * 1024 * 1024


def _cparams(*sem):
    return pltpu.CompilerParams(dimension_semantics=sem, vmem_limit_bytes=VMEM_LIMIT)


def _dot(a, b):
    return jnp.dot(a, b, preferred_element_type=F32)


def _dot_nt(a, b):
    return lax.dot_general(a, b, (((1,), (1,)), ((), ())), preferred_element_type=F32)


def _proj_kernel(x_ref, g_ref, w_ref, cos_ref, sin_ref, lbl_ref,
                 z_ref, kout_ref, vout_ref, kb_ref, vb_ref):
    tm = x_ref.shape[0]
    x = x_ref[...]
    ms = jnp.mean(x * x, axis=-1, keepdims=True)
    hn = (x * lax.rsqrt(ms + NORM_EPS) * g_ref[...]).astype(BF16)

    def col(j):
        return _dot(hn, w_ref[:, j * D_MODEL:(j + 1) * D_MODEL])

    def zcols(c):
        return slice(c * D_MODEL, (c + 1) * D_MODEL)

    def head(h):
        return slice(h * HEAD_W, (h + 1) * HEAD_W)

    def head_rows(h):
        return pl.ds(h, tm, stride=HEADS)

    def rope(zh):
        lane = lax.broadcasted_iota(jnp.int32, zh.shape, 1)
        first = (lane % ATTN_HEAD_DIM) < (ATTN_HEAD_DIM // 2)
        partner = jnp.where(first, pltpu.roll(zh, HEAD_W - ATTN_HEAD_DIM // 2, 1),
                            pltpu.roll(zh, ATTN_HEAD_DIM // 2, 1))
        return zh * cos_ref[...] + partner * sin_ref[...]

    zq = col(COL_Q)
    for h in range(HEADS):
        z_ref[:, head(h)] = rope(zq[:, head(h)]) * Q_SCALE

    zk = col(COL_K)
    for h in range(HEADS):
        kh = rope(zk[:, head(h)])
        kout_ref[head_rows(h), :] = kh
        kb_ref[:, head(h)] = kh.astype(BF16)

    zv = col(COL_V)
    for h in range(HEADS):
        vout_ref[head_rows(h), :] = zv[:, head(h)]
    vb_ref[...] = zv.astype(BF16)

    lbl = lbl_ref[...]
    e = jnp.exp(lbl - jnp.max(lbl, axis=0, keepdims=True))
    lb = e[0:1] / jnp.sum(e, axis=0, keepdims=True)
    z_ref[:, zcols(ZC_F)] = lb + (1.0 - lb) * jax.nn.sigmoid(col(COL_F))

    for j in range(COL_F + 1, N_COL_BLOCKS):
        z_ref[:, zcols(j - COL_V)] = col(j)


def _project(x, g, w_bf16, cos_t, sin_t, lb_logits):
    t = x.shape[0]
    tm = PROJ_TM
    n_tab = cos_t.shape[0] // tm
    tok = lambda i: (i, 0)
    fixed = lambda i: (0, 0)
    return pl.pallas_call(
        _proj_kernel,
        grid=(t // tm,),
        in_specs=[
            pl.BlockSpec((tm, D_MODEL), tok),
            pl.BlockSpec((1, D_MODEL), fixed),
            pl.BlockSpec((D_MODEL, N_COL_BLOCKS * D_MODEL), fixed, pipeline_mode=pl.Buffered(1)),
            pl.BlockSpec((tm, HEAD_W), lambda i: (i % n_tab, 0)),
            pl.BlockSpec((tm, HEAD_W), lambda i: (i % n_tab, 0)),
            pl.BlockSpec((2, D_MODEL), fixed),
        ],
        out_specs=[
            pl.BlockSpec((tm, N_Z_BLOCKS * D_MODEL), tok),
            pl.BlockSpec((tm * HEADS, HEAD_W), tok),
            pl.BlockSpec((tm * HEADS, HEAD_W), tok),
            pl.BlockSpec((tm, D_MODEL), tok),
            pl.BlockSpec((tm, D_MODEL), tok),
        ],
        out_shape=[
            jax.ShapeDtypeStruct((t, N_Z_BLOCKS * D_MODEL), F32),
            jax.ShapeDtypeStruct((t * HEADS, HEAD_W), F32),
            jax.ShapeDtypeStruct((t * HEADS, HEAD_W), F32),
            jax.ShapeDtypeStruct((t, D_MODEL), BF16),
            jax.ShapeDtypeStruct((t, D_MODEL), BF16),
        ],
        compiler_params=_cparams("arbitrary"),
        name="proj",
    )(x, g, w_bf16, cos_t, sin_t, lb_logits)


def _rope_tables(pos):
    half = ATTN_HEAD_DIM // 2
    inv = jnp.power(ROPE_THETA, -2.0 * jnp.arange(half, dtype=F32) / ATTN_HEAD_DIM)
    ang = pos.astype(F32)[:, None] * inv[None, :]
    cos = jnp.tile(jnp.cos(ang), (1, HEAD_W // half))
    sin = jnp.sin(ang)
    sin = jnp.tile(jnp.concatenate([-sin, sin], axis=-1), (1, HEAD_W // ATTN_HEAD_DIM))
    return cos, sin


def _split_components(q):
    lane = lax.broadcasted_iota(jnp.int32, q.shape, 1)
    q1 = jnp.where(lane < ATTN_HEAD_DIM, q, 0.0)
    q2 = jnp.where(lane >= ATTN_HEAD_DIM, q, 0.0)
    return jnp.concatenate([q1, q2], axis=0).astype(BF16)


def _attn_prompt_kernel(lam_ref, q_ref, k_ref, v_ref, o_ref):
    lam = lam_ref[0]
    tq = ATTN_TQ
    seq = k_ref.shape[0]
    k = k_ref[...]
    v = v_ref[...]
    row = lax.broadcasted_iota(jnp.int32, (2 * tq, tq), 0)
    col = lax.broadcasted_iota(jnp.int32, (2 * tq, tq), 1)
    visible = (col // CHUNK) <= ((row % tq) // CHUNK)
    n_tiles = seq // tq

    def scores(qi):
        lo = qi * tq
        qq = _split_components(q_ref[lo:lo + tq, :])
        s_d = jnp.where(visible, _dot_nt(qq, k[lo:lo + tq]), -jnp.inf)
        return s_d, (_dot_nt(qq, k[:lo]) if qi > 0 else None)

    def weighted_values(qi, w_d, w_m):
        lo = qi * tq
        o = _dot(w_d, v[lo:lo + tq])
        if qi > 0:
            o = o + _dot(w_m, v[:lo])
        o_ref[lo:lo + tq, :] = o

    ahead = scores(0)
    behind = None
    for qi in range(n_tiles):
        s_d, s_m = ahead
        if qi + 1 < n_tiles:
            ahead = scores(qi + 1)
        if behind is not None:
            weighted_values(*behind)
        m = jnp.max(s_d, axis=-1, keepdims=True)
        if qi > 0:
            m = jnp.maximum(m, jnp.max(s_m, axis=-1, keepdims=True))
            p_m = jnp.exp2(s_m - m)
        p_d = jnp.exp2(s_d - m)
        l = jnp.sum(p_d, axis=-1, keepdims=True)
        if qi > 0:
            l = l + jnp.sum(p_m, axis=-1, keepdims=True)
        r = 1.0 / l
        r1 = r[:tq]
        r2 = lam * r[tq:]
        w_d = (p_d[:tq] * r1 - p_d[tq:] * r2).astype(BF16)
        w_m = (p_m[:tq] * r1 - p_m[tq:] * r2).astype(BF16) if qi > 0 else None
        behind = (qi, w_d, w_m)
    weighted_values(*behind)


def _attn_prompt(lam, z, kb, vb, n_batch, seq):
    return pl.pallas_call(
        _attn_prompt_kernel,
        grid=(n_batch, HEADS),
        in_specs=[
            pl.BlockSpec(memory_space=pltpu.SMEM),
            pl.BlockSpec((seq, HEAD_W), lambda b, h: (b, ZC_Q * HEADS + h)),
            pl.BlockSpec((seq, HEAD_W), lambda b, h: (b, h)),
            pl.BlockSpec((seq, HEAD_W), lambda b, h: (b, h)),
        ],
        out_specs=pl.BlockSpec((seq, HEAD_W), lambda b, h: (b, h)),
        out_shape=jax.ShapeDtypeStruct((n_batch * seq, D_MODEL), F32),
        compiler_params=_cparams("arbitrary", "arbitrary"),
        name="attn_prompt",
    )(lam, z, kb, vb)


def _attn_sample_kernel(n_steps, lam_ref, q_ref, kn_ref, vn_ref, ck_hbm, cv_hbm, o_ref, kbuf, vbuf, sem):
    b = pl.program_id(0)
    h = pl.program_id(1)
    step = b * HEADS + h
    slot = step % 2

    def cache_copies(bb, hh, sl):
        part = kbuf.shape[1] // CACHE_COPY_PARTS
        copies = []
        for j in range(CACHE_COPY_PARTS):
            keys = pl.ds(j * part, part)
            copies.append(pltpu.make_async_copy(ck_hbm.at[bb, keys, hh, :], kbuf.at[sl, keys], sem.at[0, sl]))
            copies.append(pltpu.make_async_copy(cv_hbm.at[bb, keys, hh, :], vbuf.at[sl, keys], sem.at[1, sl]))
        return copies

    @pl.when(step == 0)
    def _():
        for c in cache_copies(b, h, slot):
            c.start()

    @pl.when(step + 1 < n_steps)
    def _():
        nxt = step + 1
        for c in cache_copies(nxt // HEADS, nxt % HEADS, 1 - slot):
            c.start()

    for c in cache_copies(b, h, slot):
        c.wait()
    ck_ref = kbuf.at[slot]
    cv_ref = vbuf.at[slot]

    lam = lam_ref[0]
    n_q = q_ref.shape[0]
    qq = _split_components(q_ref[...])
    s_c = _dot_nt(qq, ck_ref[...].astype(BF16))
    s_n = _dot_nt(qq, kn_ref[...])
    m = jnp.maximum(jnp.max(s_c, axis=-1, keepdims=True), jnp.max(s_n, axis=-1, keepdims=True))
    p_c = jnp.exp2(s_c - m)
    p_n = jnp.exp2(s_n - m)
    r = 1.0 / (jnp.sum(p_c, axis=-1, keepdims=True) + jnp.sum(p_n, axis=-1, keepdims=True))
    r1 = r[:n_q]
    r2 = lam * r[n_q:]
    w_c = (p_c[:n_q] * r1 - p_c[n_q:] * r2).astype(BF16)
    w_n = (p_n[:n_q] * r1 - p_n[n_q:] * r2).astype(BF16)
    o_ref[...] = _dot(w_c, cv_ref[...].astype(BF16)) + _dot(w_n, vn_ref[...])


def _attn_sample(lam, z, kb, vb, cache_k, cache_v, n_batch, n_q):
    past = cache_k.shape[1]
    cache_spec = pl.BlockSpec(memory_space=pl.ANY)
    return pl.pallas_call(
        functools.partial(_attn_sample_kernel, n_batch * HEADS),
        grid=(n_batch, HEADS),
        in_specs=[
            pl.BlockSpec(memory_space=pltpu.SMEM),
            pl.BlockSpec((n_q, HEAD_W), lambda b, h: (b, ZC_Q * HEADS + h)),
            pl.BlockSpec((n_q, HEAD_W), lambda b, h: (b, h)),
            pl.BlockSpec((n_q, HEAD_W), lambda b, h: (b, h)),
            cache_spec, cache_spec,
        ],
        out_specs=pl.BlockSpec((n_q, HEAD_W), lambda b, h: (b, h)),
        out_shape=jax.ShapeDtypeStruct((n_batch * n_q, D_MODEL), F32),
        scratch_shapes=[
            pltpu.VMEM((2, past, HEAD_W), F32),
            pltpu.VMEM((2, past, HEAD_W), F32),
            pltpu.SemaphoreType.DMA((2, 2)),
        ],
        compiler_params=_cparams("arbitrary", "arbitrary"),
        name="attn_sample",
    )(lam, z, kb, vb, cache_k, cache_v)


def _split3(x):
    h1 = x.astype(BF16)
    r1 = x - h1.astype(F32)
    h2 = r1.astype(BF16)
    h3 = (r1 - h2.astype(F32)).astype(BF16)
    return h1, h2, h3


def _when_step(step, n_steps, which):
    def deco(body):
        if n_steps == 1:
            body()
        else:
            pl.when(step == which)(body)
    return deco


def _hgrn_kernel(n_chunks, f_ref, q_ref, i_ref, s0_ref, o_ref, sout_ref, st_ref):
    c = pl.program_id(1)
    n_group = f_ref.shape[0]

    @_when_step(c, n_chunks, 0)
    def _():
        for g in range(n_group):
            for h in range(HEADS):
                st_ref[g, h] = s0_ref[g, h].T

    row = lax.broadcasted_iota(jnp.int32, (CHUNK, CHUNK), 0)
    col = lax.broadcasted_iota(jnp.int32, (CHUNK, CHUNK), 1)
    causal = row >= col
    tril = jnp.where(causal, 1.0, 0.0).astype(BF16)
    heads = [slice(h * HEAD_W, (h + 1) * HEAD_W) for h in range(HEADS)]
    groups = range(n_group)
    fs = [f_ref[g] for g in groups]
    splits = [_split3(jnp.log(f)) for f in fs]
    bs = [_dot(tril, h1) + _dot(tril, h2) + _dot(tril, h3) for h1, h2, h3 in splits]
    scaled = []
    for g in groups:
        b = bs[g]
        b_last = b[CHUNK - 1:CHUNK, :]
        rk = 1.0 - fs[g]
        v = i_ref[g]
        scaled.append(((q_ref[g] * jnp.exp(b)).astype(BF16), (rk * jnp.exp(-b)).astype(BF16),
                       (rk * jnp.exp(b_last - b)).astype(BF16), v, v.astype(BF16), jnp.exp(b_last)))
    first = []
    for g in groups:
        qd_b, kd_b, kl_b, v, v_b, eb_last = scaled[g]
        states = [st_ref[g, h] for h in range(HEADS)]
        a_all = [_dot_nt(qd_b[:, sl], kd_b[:, sl]) for sl in heads]
        o_state = [_dot_nt(qd_b[:, sl], st.astype(BF16)) for sl, st in zip(heads, states)]
        s_upd = [_dot(v[:, sl].T.astype(BF16), kl_b[:, sl]) for sl in heads]
        first.append((states, a_all, o_state, s_upd))
    for g in groups:
        v_b, eb_last = scaled[g][4], scaled[g][5]
        states, a_all, o_state, s_upd = first[g]
        for h, sl in enumerate(heads):
            a = jnp.where(causal, a_all[h], 0.0).astype(BF16)
            o_ref[g, :, sl] = o_state[h] + _dot(a, v_b[:, sl])
            st_ref[g, h] = states[h] * eb_last[:, sl] + s_upd[h]

    @_when_step(c, n_chunks, n_chunks - 1)
    def _():
        for g in range(n_group):
            for h in range(HEADS):
                sout_ref[g, h] = st_ref[g, h].T


def _hgrn(z, s0, n_batch, n_chunks, n_group):
    z3 = z.reshape(n_batch, n_chunks * CHUNK, N_Z_BLOCKS * D_MODEL)

    def zblock(zc):
        return pl.BlockSpec((n_group, CHUNK, D_MODEL), lambda b, c: (b, c, zc))

    state = pl.BlockSpec((n_group, HEADS, HEAD_W, HEAD_W), lambda b, c: (b, 0, 0, 0))
    ro, st = pl.pallas_call(
        functools.partial(_hgrn_kernel, n_chunks),
        grid=(n_batch // n_group, n_chunks),
        in_specs=[zblock(ZC_F), zblock(ZC_RQ), zblock(ZC_RI), state],
        out_specs=[pl.BlockSpec((n_group, CHUNK, D_MODEL), lambda b, c: (b, c, 0)), state],
        out_shape=[
            jax.ShapeDtypeStruct((n_batch, n_chunks * CHUNK, D_MODEL), F32),
            jax.ShapeDtypeStruct((n_batch, HEADS, HEAD_W, HEAD_W), F32),
        ],
        scratch_shapes=[pltpu.VMEM((n_group, HEADS, HEAD_W, HEAD_W), F32)],
        compiler_params=_cparams("arbitrary", "arbitrary"),
        name="hgrn",
    )(z3, z3, z3, s0)
    return ro.reshape(n_batch * n_chunks * CHUNK, D_MODEL), st


def _head_rmsnorm(x, g):
    outs = []
    for h in range(HEADS):
        xh = x[:, h * HEAD_W:(h + 1) * HEAD_W]
        ms = jnp.mean(xh * xh, axis=-1, keepdims=True)
        outs.append(xh * lax.rsqrt(ms + NORM_EPS) * g)
    return jnp.concatenate(outs, axis=-1)


def _merge_kernel(lam_init, ao_ref, ro_ref, rg_ref, ga_ref, gr_ref, x_ref, gs_ref, grn_ref,
                  woa_ref, wor_ref, wout_ref, gffn_ref, wr_ref, br_ref, cnt_in_ref,
                  x1_ref, xn_ref, idx_ref, prob_ref, rank_ref, cnt_ref, run_ref):
    i = pl.program_id(0)
    tm = x_ref.shape[0]

    @pl.when(i == 0)
    def _():
        run_ref[...] = cnt_in_ref[...]

    part = tm // MERGE_PARTS

    def rows(p):
        return slice(p * part, (p + 1) * part)

    def norm_inputs(p, _):
        an = (_head_rmsnorm(ao_ref[rows(p), :], gs_ref[...]) * (1.0 - lam_init)).astype(BF16)
        rn = (_head_rmsnorm(ro_ref[rows(p), :], grn_ref[...]) * jax.nn.silu(rg_ref[rows(p), :])).astype(BF16)
        return an, rn

    def branch_projections(p, st):
        an, rn = st
        return _dot(an, woa_ref[...]), _dot(rn, wor_ref[...])

    def gate(p, st):
        a, r = st
        return (jax.nn.sigmoid(ga_ref[rows(p), :]) * a + jax.nn.sigmoid(gr_ref[rows(p), :]) * r).astype(BF16)

    def residual(p, mixed):
        x1 = x_ref[rows(p), :] + _dot(mixed, wout_ref[...])
        x1_ref[rows(p), :] = x1
        return x1

    def ffn_norm_router(p, x1):
        ms = jnp.mean(x1 * x1, axis=-1, keepdims=True)
        xn = x1 * lax.rsqrt(ms + NORM_EPS) * gffn_ref[...]
        for s in range(ROW_SUB):
            xn_ref[pl.ds(p * part * ROW_SUB + s, part, stride=ROW_SUB), :] = xn[:, s * HEAD_W:(s + 1) * HEAD_W]
        return _dot(xn.astype(BF16), wr_ref[...])

    stages = (norm_inputs, branch_projections, gate, residual, ffn_norm_router)
    state = [None] * MERGE_PARTS
    for stage in stages:
        for p in range(MERGE_PARTS):
            state[p] = stage(p, state[p])
    logits = jnp.concatenate(state, axis=0) + br_ref[...]
    lane = lax.broadcasted_iota(jnp.int32, logits.shape, 1).astype(F32)
    sel = jnp.zeros(logits.shape, F32)
    work = logits
    tops, idxs = [], []
    for _ in range(TOP_K):
        m = jnp.max(work, axis=-1, keepdims=True)
        idx = jnp.min(jnp.where(work == m, lane, float(N_EXPERTS)), axis=-1, keepdims=True)
        hit = lane == idx
        sel = jnp.where(hit, 1.0, sel)
        work = jnp.where(hit, -jnp.inf, work)
        tops.append(m)
        idxs.append(idx)
    es = [jnp.exp(v - tops[0]) for v in tops]
    inv = 1.0 / (es[0] + es[1] + es[2] + es[3])

    trow = lax.broadcasted_iota(jnp.int32, (tm, tm), 0)
    tcol = lax.broadcasted_iota(jnp.int32, (tm, tm), 1)
    before = jnp.where(trow > tcol, 1.0, 0.0).astype(BF16)
    ranks = _dot(before, sel.astype(BF16)) + run_ref[...]
    k_lane = lax.broadcasted_iota(jnp.int32, (tm, TOP_K), 1)
    idx_out = jnp.zeros((tm, TOP_K), F32)
    prob_out = jnp.zeros((tm, TOP_K), F32)
    rank_out = jnp.zeros((tm, TOP_K), F32)
    for k in range(TOP_K):
        rk = jnp.sum(jnp.where(lane == idxs[k], ranks, 0.0), axis=-1, keepdims=True)
        idx_out = jnp.where(k_lane == k, idxs[k], idx_out)
        prob_out = jnp.where(k_lane == k, es[k] * inv, prob_out)
        rank_out = jnp.where(k_lane == k, rk, rank_out)
    idx_ref[...] = idx_out.astype(jnp.int32)
    prob_ref[...] = prob_out
    rank_ref[...] = rank_out.astype(jnp.int32)
    run = run_ref[...] + jnp.sum(sel, axis=0, keepdims=True)
    run_ref[...] = run
    cnt_ref[...] = run


def _merge(lam_init, ao, ro, z, x, g_subln, g_rec, woa, wor, wout, g_ffn, w_router, b_router, cnt_in):
    t = x.shape[0]
    tm = MERGE_TM
    row = lambda i: (i, 0)
    fixed = lambda i: (0, 0)
    tok = pl.BlockSpec((tm, D_MODEL), row)
    wspec = pl.BlockSpec((D_MODEL, D_MODEL), fixed)
    narrow = pl.BlockSpec((tm, TOP_K), row)
    return pl.pallas_call(
        functools.partial(_merge_kernel, lam_init),
        grid=(t // tm,),
        in_specs=[
            tok, tok,
            pl.BlockSpec((tm, D_MODEL), lambda i: (i, ZC_RG)),
            pl.BlockSpec((tm, D_MODEL), lambda i: (i, ZC_GA)),
            pl.BlockSpec((tm, D_MODEL), lambda i: (i, ZC_GR)),
            tok,
            pl.BlockSpec((1, HEAD_W), fixed), pl.BlockSpec((1, HEAD_W), fixed),
            wspec, wspec, wspec,
            pl.BlockSpec((1, D_MODEL), fixed),
            pl.BlockSpec((D_MODEL, N_EXPERTS), fixed),
            pl.BlockSpec((1, N_EXPERTS), fixed),
            pl.BlockSpec((1, N_EXPERTS), fixed),
        ],
        out_specs=[tok, pl.BlockSpec((tm * ROW_SUB, HEAD_W), row), narrow, narrow, narrow,
                   pl.BlockSpec((1, N_EXPERTS), fixed)],
        out_shape=[
            jax.ShapeDtypeStruct((t, D_MODEL), F32),
            jax.ShapeDtypeStruct((t * ROW_SUB, HEAD_W), F32),
            jax.ShapeDtypeStruct((t, TOP_K), jnp.int32),
            jax.ShapeDtypeStruct((t, TOP_K), F32),
            jax.ShapeDtypeStruct((t, TOP_K), jnp.int32),
            jax.ShapeDtypeStruct((1, N_EXPERTS), F32),
        ],
        scratch_shapes=[pltpu.VMEM((1, N_EXPERTS), F32)],
        compiler_params=_cparams("arbitrary"),
        name="merge",
    )(ao, ro, z, z, z, x, g_subln, g_rec, woa, wor, wout, g_ffn, w_router, b_router, cnt_in)


def _row_copy(src, dst, sem):
    return pltpu.make_async_copy(src, dst, sem)


def _dispatch_kernel(n_first, n_steps, pos_ref, xa_hbm, xb_hbm, xs_hbm, stage, load_sem, copy_sem):
    i = pl.program_id(0)
    tm = ROW_TM
    rows = tm * ROW_SUB
    slot = i % DISPATCH_SLOTS

    def row_tile(off):
        return pl.ds(pl.multiple_of(off, ROW_SUB), ROW_SUB)

    def load(t, sl):
        @pl.when(t < n_first)
        def _():
            _row_copy(xa_hbm.at[pl.ds(pl.multiple_of(t * rows, rows), rows)], stage.at[sl], load_sem.at[sl]).start()

        @pl.when(t >= n_first)
        def _():
            _row_copy(xb_hbm.at[pl.ds(pl.multiple_of((t - n_first) * rows, rows), rows)], stage.at[sl],
                      load_sem.at[sl]).start()

    def drain(parity):
        for k in range(TOP_K):
            _row_copy(stage.at[0], xs_hbm.at[pl.ds(0, rows)], copy_sem.at[parity]).wait()

    @pl.when(i == 0)
    def _():
        load(i, slot)

    @pl.when(i + 1 < n_steps)
    def _():
        load(i + 1, (i + 1) % DISPATCH_SLOTS)

    _row_copy(xa_hbm.at[pl.ds(0, rows)], stage.at[slot], load_sem.at[slot]).wait()

    def start(g, carry):
        r0 = g * ROW_COPY_UNROLL
        n = ROW_COPY_UNROLL * TOP_K
        dst = [pos_ref[0, 0, r0 * TOP_K + j] for j in range(n)]
        for j in range(n):
            _row_copy(stage.at[slot, row_tile((r0 + j // TOP_K) * ROW_SUB)],
                      xs_hbm.at[row_tile(dst[j])], copy_sem.at[i % 2]).start()
        return carry

    lax.fori_loop(0, tm // ROW_COPY_UNROLL, start, 0)

    @pl.when(i > 0)
    def _():
        drain((i - 1) % 2)

    @pl.when(i == n_steps - 1)
    def _():
        drain(i % 2)


def _dispatch(pos, xn_a, xn_b):
    tm = ROW_TM
    n_a, n_b = xn_a.shape[0] // (tm * ROW_SUB), xn_b.shape[0] // (tm * ROW_SUB)
    pos3 = pos.reshape(n_a + n_b, 1, tm * TOP_K)
    return pl.pallas_call(
        functools.partial(_dispatch_kernel, n_a, n_a + n_b),
        grid=(n_a + n_b,),
        in_specs=[
            pl.BlockSpec((1, 1, tm * TOP_K), lambda i: (i, 0, 0), memory_space=pltpu.SMEM),
            pl.BlockSpec(memory_space=pl.ANY),
            pl.BlockSpec(memory_space=pl.ANY),
        ],
        out_specs=pl.BlockSpec(memory_space=pl.ANY),
        out_shape=jax.ShapeDtypeStruct((pos.size * ROW_SUB, HEAD_W), F32),
        scratch_shapes=[
            pltpu.VMEM((DISPATCH_SLOTS, tm * ROW_SUB, HEAD_W), F32),
            pltpu.SemaphoreType.DMA((DISPATCH_SLOTS,)),
            pltpu.SemaphoreType.DMA((2,)),
        ],
        compiler_params=_cparams("arbitrary"),
        name="dispatch",
    )(pos3, xn_a, xn_b)


def _experts_kernel(tile_ref, exp_ref, lo_ref, hi_ref, fresh_ref, par_ref, nxt_ref, more_ref,
                    xs_ref, wg_hbm, bg_ref, wu_hbm, bu_ref, wd_hbm, bd_ref, o_ref,
                    wg_s, wu_s, wd_s, x_s, acc_s, wf_s, wsem):
    w = pl.program_id(0)
    lo = lo_ref[w]
    hi = hi_ref[w]
    tm = x_s.shape[0]

    def lanes(s):
        return slice(s * HEAD_W, (s + 1) * HEAD_W)

    def sub_rows(s):
        return pl.ds(s, tm, stride=ROW_SUB)

    def weight_copies(e, sl):
        return [pltpu.make_async_copy(src.at[e], wf_s.at[sl, j], wsem.at[sl])
                for j, src in enumerate((wg_hbm, wu_hbm, wd_hbm))]

    @pl.when(w == 0)
    def _():
        for c in weight_copies(exp_ref[0], par_ref[0]):
            c.start()
        acc_s[...] = jnp.zeros(acc_s.shape, F32)

    @pl.when(fresh_ref[w] == 1)
    def _():
        sl = par_ref[w]
        for c in weight_copies(exp_ref[w], sl):
            c.wait()

        @pl.when(more_ref[w] == 1)
        def _():
            for c in weight_copies(nxt_ref[w], 1 - sl):
                c.start()

        wg_s[...] = wf_s[sl, 0].astype(BF16)
        wu_s[...] = wf_s[sl, 1].astype(BF16)
        wd_s[...] = wf_s[sl, 2].astype(BF16)

    @pl.when(hi > lo)
    def _():
        for s in range(ROW_SUB):
            x_s[:, lanes(s)] = xs_ref[sub_rows(s), :].astype(BF16)
        x = x_s[...]
        out = bd_ref[0]
        def gate_up(c):
            ff = slice(c, c + MOE_FF_SLICE)
            return _dot(x, wg_s[:, ff]) + bg_ref[0, :, ff], _dot(x, wu_s[:, ff]) + bu_ref[0, :, ff]

        slices = range(0, D_MODEL, MOE_FF_SLICE)
        gus = [gate_up(c) for c in slices]
        acts = []
        for gate, up in gus:
            gate = jnp.minimum(gate, SWIGLU_LIMIT)
            up = jnp.clip(up, -SWIGLU_LIMIT, SWIGLU_LIMIT)
            acts.append((gate * jax.nn.sigmoid(SWIGLU_ALPHA * gate) * (up + 1.0)).astype(BF16))
        for c, act in zip(slices, acts):
            out = out + _dot(act, wd_s[c:c + MOE_FF_SLICE, :])
        row = lax.broadcasted_iota(jnp.int32, (tm, 1), 0)
        merged = jnp.where((row >= lo) & (row < hi), out, acc_s[...])
        acc_s[...] = merged
        for s in range(ROW_SUB):
            o_ref[sub_rows(s), :] = merged[:, lanes(s)]


def _experts(plan, xs, w_gate, b_gate, w_up, b_up, w_down, b_down):
    tm = MOE_TM
    n_items = plan[0].shape[0]
    rows = lambda w, tile, *_: (tile[w], 0)
    wspec = pl.BlockSpec(memory_space=pl.ANY)
    bspec = pl.BlockSpec((1, 1, D_MODEL), lambda w, tile, exp, *_: (exp[w], 0, 0))
    grid_spec = pltpu.PrefetchScalarGridSpec(
        num_scalar_prefetch=len(plan),
        grid=(n_items,),
        in_specs=[pl.BlockSpec((tm * ROW_SUB, HEAD_W), rows), wspec, bspec, wspec, bspec, wspec, bspec],
        out_specs=pl.BlockSpec((tm * ROW_SUB, HEAD_W), rows),
        scratch_shapes=[pltpu.VMEM((D_MODEL, D_MODEL), BF16)] * 3 + [
            pltpu.VMEM((tm, D_MODEL), BF16),
            pltpu.VMEM((tm, D_MODEL), F32),
            pltpu.VMEM((2, 3, D_MODEL, D_MODEL), F32),
            pltpu.SemaphoreType.DMA((2,)),
        ],
    )
    return pl.pallas_call(
        _experts_kernel,
        grid_spec=grid_spec,
        out_shape=jax.ShapeDtypeStruct(xs.shape, F32),
        compiler_params=_cparams("arbitrary"),
        name="experts",
    )(*plan, xs, w_gate, b_gate.reshape(N_EXPERTS, 1, D_MODEL), w_up, b_up.reshape(N_EXPERTS, 1, D_MODEL),
      w_down, b_down.reshape(N_EXPERTS, 1, D_MODEL))


def _expert_plan(counts, n_rows):
    tm = MOE_TM
    n_tiles = n_rows // tm
    n_items = n_tiles + N_EXPERTS - 1
    ends = jnp.cumsum(counts)
    starts = ends - counts
    first_tile = starts // tm
    last_tile = jnp.maximum(ends - 1, 0) // tm
    items = jnp.where(counts > 0, last_tile - first_tile + 1, 0)
    item_end = jnp.cumsum(items)
    item_start = item_end - items
    total = item_end[-1]
    w = jnp.arange(n_items, dtype=jnp.int32)
    wc = jnp.minimum(w, total - 1)
    exp = jnp.sum((item_end[None, :] <= wc[:, None]).astype(jnp.int32), axis=1)
    ids = jnp.arange(N_EXPERTS, dtype=jnp.int32)
    mine = exp[:, None] == ids[None, :]

    def of_item(per_expert):
        return jnp.sum(jnp.where(mine, per_expert[None, :], 0), axis=1).astype(jnp.int32)

    tile = of_item(first_tile) + wc - of_item(item_start)
    lo = jnp.maximum(of_item(starts), tile * tm) - tile * tm
    hi = jnp.minimum(of_item(ends), (tile + 1) * tm) - tile * tm
    valid = w < total
    lo = jnp.where(valid, lo, 0).astype(jnp.int32)
    hi = jnp.where(valid, hi, 0).astype(jnp.int32)
    prev_exp = jnp.concatenate([jnp.full((1,), -1, jnp.int32), exp[:-1]])
    fresh = (valid & (exp != prev_exp)).astype(jnp.int32)
    later = jnp.where((counts > 0)[None, :] & (ids[None, :] > ids[:, None]), ids[None, :], N_EXPERTS)
    next_exp = jnp.min(later, axis=1)
    parity = ((jnp.cumsum((counts > 0).astype(jnp.int32)) - 1) % 2).astype(jnp.int32)
    next_of_item = of_item(next_exp)
    more = (next_of_item < N_EXPERTS).astype(jnp.int32)
    nxt = jnp.where(more == 1, next_of_item, exp).astype(jnp.int32)
    return tile, exp, lo, hi, fresh, of_item(parity), nxt, more


def _combine_kernel(n_steps, pos_ref, pos_next_ref, x1_ref, prob_ref, gfin_ref, ys_ref, y_ref, buf_ref, sem):
    i = pl.program_id(0)
    tm = x1_ref.shape[0]
    slot = i % 2

    def row_tile(off):
        return pl.ds(pl.multiple_of(off, ROW_SUB), ROW_SUB)

    def gather(p_ref, sl):
        def start(g, carry):
            r0 = g * ROW_COPY_UNROLL
            n = ROW_COPY_UNROLL * TOP_K
            src = [p_ref[0, 0, r0 * TOP_K + j] for j in range(n)]
            for j in range(n):
                _row_copy(ys_ref.at[row_tile(src[j])],
                          buf_ref.at[sl, j % TOP_K, row_tile((r0 + j // TOP_K) * ROW_SUB)], sem.at[sl]).start()
            return carry

        lax.fori_loop(0, tm // ROW_COPY_UNROLL, start, 0)

    @pl.when(i == 0)
    def _():
        gather(pos_ref, slot)

    @pl.when(i + 1 < n_steps)
    def _():
        gather(pos_next_ref, 1 - slot)

    for k in range(TOP_K):
        _row_copy(ys_ref.at[pl.ds(0, tm * ROW_SUB)], buf_ref.at[slot, k], sem.at[slot]).wait()

    prob = prob_ref[...]
    parts = []
    for s in range(ROW_SUB):
        part = x1_ref[:, s * HEAD_W:(s + 1) * HEAD_W]
        for k in range(TOP_K):
            part = part + prob[:, k:k + 1] * buf_ref[slot, k, pl.ds(s, tm, stride=ROW_SUB), :]
        parts.append(part)
    x2 = jnp.concatenate(parts, axis=-1)
    ms = jnp.mean(x2 * x2, axis=-1, keepdims=True)
    y_ref[...] = x2 * lax.rsqrt(ms + NORM_EPS) * gfin_ref[...]


def _combine(pos, x1, prob, g_final, ys):
    t = x1.shape[0]
    tm = ROW_TM
    n = t // tm
    pos3 = pos.reshape(n, 1, tm * TOP_K)
    return pl.pallas_call(
        functools.partial(_combine_kernel, n),
        grid=(n,),
        in_specs=[
            pl.BlockSpec((1, 1, tm * TOP_K), lambda i: (i, 0, 0), memory_space=pltpu.SMEM),
            pl.BlockSpec((1, 1, tm * TOP_K), lambda i: (jnp.minimum(i + 1, n - 1), 0, 0), memory_space=pltpu.SMEM),
            pl.BlockSpec((tm, D_MODEL), lambda i: (i, 0)),
            pl.BlockSpec((tm, TOP_K), lambda i: (i, 0)),
            pl.BlockSpec((1, D_MODEL), lambda i: (0, 0)),
            pl.BlockSpec(memory_space=pl.ANY),
        ],
        out_specs=pl.BlockSpec((tm, D_MODEL), lambda i: (i, 0)),
        out_shape=jax.ShapeDtypeStruct((t, D_MODEL), F32),
        scratch_shapes=[pltpu.VMEM((2, TOP_K, tm * ROW_SUB, HEAD_W), F32), pltpu.SemaphoreType.DMA((2,))],
        compiler_params=_cparams("arbitrary"),
        name="combine",
    )(pos3, pos3, x1, prob, g_final, ys)


def _lambda_init(layer):
    return 0.8 - 0.6 * math.exp(-0.3 * layer)


def kernel(x_prompt, x_sample, cache_k, cache_v, state_rec, g_mix, w_in, lambda_q1, lambda_k1, lambda_q2, lambda_k2, g_subln, w_o_attn, lb_logits, g_rec_norm, w_o_rec, w_out, g_ffn, w_router, b_router, w_gate, b_gate, w_up, b_up, w_down, b_down, g_final):
    nb, seq, _ = x_prompt.shape
    nbd, ld, _ = x_sample.shape
    past = cache_k.shape[2]
    tp, ts = nb * seq, nbd * ld
    lam_init = _lambda_init(0)
    lam = (jnp.exp(jnp.sum(lambda_q1[0].astype(F32) * lambda_k1[0].astype(F32)))
           - jnp.exp(jnp.sum(lambda_q2[0].astype(F32) * lambda_k2[0].astype(F32))) + lam_init).reshape(1)

    w_in_b = w_in[0].astype(BF16)
    woa, wor, wout = w_o_attn[0].astype(BF16), w_o_rec[0].astype(BF16), w_out[0].astype(BF16)
    wr = w_router[0].astype(BF16)
    g_mix2, g_ffn2, g_fin2 = g_mix[0].reshape(1, -1), g_ffn[0].reshape(1, -1), g_final.reshape(1, -1)
    gs2, grn2 = g_subln[0].reshape(1, -1), g_rec_norm[0].reshape(1, -1)
    br2 = b_router[0].reshape(1, -1)

    cos_p, sin_p = _rope_tables(jnp.arange(seq, dtype=jnp.int32))
    pos_s = past + (jnp.arange(PROJ_TM, dtype=jnp.int32) % ld)
    cos_s, sin_s = _rope_tables(pos_s)

    xp = x_prompt.reshape(tp, D_MODEL)
    xs_in = x_sample.reshape(ts, D_MODEL)
    zp, k_p, v_p, kb_p, vb_p = _project(xp, g_mix2, w_in_b, cos_p, sin_p, lb_logits)
    zs, k_s, v_s, kb_s, vb_s = _project(xs_in, g_mix2, w_in_b, cos_s, sin_s, lb_logits)

    ao_p = _attn_prompt(lam, zp, kb_p, vb_p, nb, seq)
    ao_s = _attn_sample(lam, zs, kb_s, vb_s, cache_k[0], cache_v[0], nbd, ld)
    ro_p, st_p = _hgrn(zp, jnp.zeros((nb, HEADS, HEAD_W, HEAD_W), F32), nb, seq // CHUNK, HGRN_GROUP_PROMPT)
    ro_s, st_s = _hgrn(zs, state_rec[0], nbd, ld // CHUNK, HGRN_GROUP_SAMPLE)

    merge_w = (gs2, grn2, woa, wor, wout, g_ffn2, wr, br2)
    zero_cnt = jnp.zeros((1, N_EXPERTS), F32)
    x1_p, xn_p, idx_p, prob_p, rank_p, cnt_p = _merge(lam_init, ao_p, ro_p, zp, xp, *merge_w, zero_cnt)
    x1_s, xn_s, idx_s, prob_s, rank_s, cnt = _merge(lam_init, ao_s, ro_s, zs, xs_in, *merge_w, cnt_p)

    counts = cnt[0].astype(jnp.int32)
    starts = jnp.cumsum(counts) - counts
    experts = jnp.arange(N_EXPERTS, dtype=jnp.int32)

    def positions(idx, rank):
        return (jnp.sum(jnp.where(idx[..., None] == experts, starts, 0), axis=-1) + rank) * ROW_SUB

    pos_p = positions(idx_p, rank_p)
    pos_s2 = positions(idx_s, rank_s)
    n_rows = (tp + ts) * TOP_K
    plan = _expert_plan(counts, n_rows)

    xs_sorted = _dispatch(jnp.concatenate([pos_p, pos_s2], axis=0), xn_p, xn_s)
    ys = _experts(plan, xs_sorted, w_gate[0], b_gate[0], w_up[0], b_up[0], w_down[0], b_down[0])
    y_p = _combine(pos_p, x1_p, prob_p, g_fin2, ys)
    y_s = _combine(pos_s2, x1_s, prob_s, g_fin2, ys)

    return (y_p.reshape(nb, seq, D_MODEL), y_s.reshape(nbd, ld, D_MODEL),
            k_p.reshape(1, nb, seq, HEADS, HEAD_W), v_p.reshape(1, nb, seq, HEADS, HEAD_W), st_p[None],
            k_s.reshape(1, nbd, ld, HEADS, HEAD_W), v_s.reshape(1, nbd, ld, HEADS, HEAD_W), st_s[None])
```

```python
import functools
import math

import jax
import jax.numpy as jnp
from jax import lax
from jax.experimental import pallas as pl
from jax.experimental.pallas import tpu as pltpu

F32 = jnp.float32
BF16 = jnp.bfloat16

D_MODEL = 1024
CHUNK = 64
HEADS = 8
HEAD_W = 128
ATTN_HEAD_DIM = 64
ROPE_THETA = 10000.0
Q_SCALE = ATTN_HEAD_DIM ** -0.5 * math.log2(math.e)
N_EXPERTS = 32
TOP_K = 4
SWIGLU_ALPHA = 1.702
SWIGLU_LIMIT = 7.0
NORM_EPS = 1e-6
N_COL_BLOCKS = 9
COL_Q, COL_K, COL_V, COL_F = range(4)
N_Z_BLOCKS = 7
ZC_Q, ZC_F, ZC_RQ, ZC_RI, ZC_RG, ZC_GA, ZC_GR = range(7)
PROJ_TM = 256

ATTN_TQ = 256
CACHE_COPY_PARTS = 4
HGRN_GROUP_PROMPT = 4
HGRN_GROUP_SAMPLE = 4
MERGE_TM = 512
MERGE_PARTS = 2
MOE_TM = 256
MOE_FF_SLICE = 256
ROW_TM = 256
ROW_COPY_UNROLL = 4
DISPATCH_SLOTS = 3
ROW_SUB = D_MODEL // HEAD_W
VMEM_LIMIT = 56 * 1024 * 1024


def _cparams(*sem):
    return pltpu.CompilerParams(dimension_semantics=sem, vmem_limit_bytes=VMEM_LIMIT)


def _dot(a, b):
    return jnp.dot(a, b, preferred_element_type=F32)


def _dot_nt(a, b):
    return lax.dot_general(a, b, (((1,), (1,)), ((), ())), preferred_element_type=F32)


def _proj_kernel(x_ref, g_ref, w_ref, cos_ref, sin_ref, lbl_ref,
                 z_ref, kout_ref, vout_ref, kb_ref, vb_ref):
    tm = x_ref.shape[0]
    x = x_ref[...]
    ms = jnp.mean(x * x, axis=-1, keepdims=True)
    hn = (x * lax.rsqrt(ms + NORM_EPS) * g_ref[...]).astype(BF16)

    def col(j):
        return _dot(hn, w_ref[:, j * D_MODEL:(j + 1) * D_MODEL])

    def zcols(c):
        return slice(c * D_MODEL, (c + 1) * D_MODEL)

    def head(h):
        return slice(h * HEAD_W, (h + 1) * HEAD_W)

    def head_rows(h):
        return pl.ds(h, tm, stride=HEADS)

    def rope(zh):
        lane = lax.broadcasted_iota(jnp.int32, zh.shape, 1)
        first = (lane % ATTN_HEAD_DIM) < (ATTN_HEAD_DIM // 2)
        partner = jnp.where(first, pltpu.roll(zh, HEAD_W - ATTN_HEAD_DIM // 2, 1),
                            pltpu.roll(zh, ATTN_HEAD_DIM // 2, 1))
        return zh * cos_ref[...] + partner * sin_ref[...]

    zq = col(COL_Q)
    for h in range(HEADS):
        z_ref[:, head(h)] = rope(zq[:, head(h)]) * Q_SCALE

    zk = col(COL_K)
    for h in range(HEADS):
        kh = rope(zk[:, head(h)])
        kout_ref[head_rows(h), :] = kh
        kb_ref[:, head(h)] = kh.astype(BF16)

    zv = col(COL_V)
    for h in range(HEADS):
        vout_ref[head_rows(h), :] = zv[:, head(h)]
    vb_ref[...] = zv.astype(BF16)

    lbl = lbl_ref[...]
    e = jnp.exp(lbl - jnp.max(lbl, axis=0, keepdims=True))
    lb = e[0:1] / jnp.sum(e, axis=0, keepdims=True)
    z_ref[:, zcols(ZC_F)] = lb + (1.0 - lb) * jax.nn.sigmoid(col(COL_F))

    for j in range(COL_F + 1, N_COL_BLOCKS):
        z_ref[:, zcols(j - COL_V)] = col(j)


def _project(x, g, w_bf16, cos_t, sin_t, lb_logits):
    t = x.shape[0]
    tm = PROJ_TM
    n_tab = cos_t.shape[0] // tm
    tok = lambda i: (i, 0)
    fixed = lambda i: (0, 0)
    return pl.pallas_call(
        _proj_kernel,
        grid=(t // tm,),
        in_specs=[
            pl.BlockSpec((tm, D_MODEL), tok),
            pl.BlockSpec((1, D_MODEL), fixed),
            pl.BlockSpec((D_MODEL, N_COL_BLOCKS * D_MODEL), fixed, pipeline_mode=pl.Buffered(1)),
            pl.BlockSpec((tm, HEAD_W), lambda i: (i % n_tab, 0)),
            pl.BlockSpec((tm, HEAD_W), lambda i: (i % n_tab, 0)),
            pl.BlockSpec((2, D_MODEL), fixed),
        ],
        out_specs=[
            pl.BlockSpec((tm, N_Z_BLOCKS * D_MODEL), tok),
            pl.BlockSpec((tm * HEADS, HEAD_W), tok),
            pl.BlockSpec((tm * HEADS, HEAD_W), tok),
            pl.BlockSpec((tm, D_MODEL), tok),
            pl.BlockSpec((tm, D_MODEL), tok),
        ],
        out_shape=[
            jax.ShapeDtypeStruct((t, N_Z_BLOCKS * D_MODEL), F32),
            jax.ShapeDtypeStruct((t * HEADS, HEAD_W), F32),
            jax.ShapeDtypeStruct((t * HEADS, HEAD_W), F32),
            jax.ShapeDtypeStruct((t, D_MODEL), BF16),
            jax.ShapeDtypeStruct((t, D_MODEL), BF16),
        ],
        compiler_params=_cparams("arbitrary"),
        name="proj",
    )(x, g, w_bf16, cos_t, sin_t, lb_logits)


def _rope_tables(pos):
    half = ATTN_HEAD_DIM // 2
    inv = jnp.power(ROPE_THETA, -2.0 * jnp.arange(half, dtype=F32) / ATTN_HEAD_DIM)
    ang = pos.astype(F32)[:, None] * inv[None, :]
    cos = jnp.tile(jnp.cos(ang), (1, HEAD_W // half))
    sin = jnp.sin(ang)
    sin = jnp.tile(jnp.concatenate([-sin, sin], axis=-1), (1, HEAD_W // ATTN_HEAD_DIM))
    return cos, sin


def _split_components(q):
    lane = lax.broadcasted_iota(jnp.int32, q.shape, 1)
    q1 = jnp.where(lane < ATTN_HEAD_DIM, q, 0.0)
    q2 = jnp.where(lane >= ATTN_HEAD_DIM, q, 0.0)
    return jnp.concatenate([q1, q2], axis=0).astype(BF16)


def _attn_prompt_kernel(lam_ref, q_ref, k_ref, v_ref, o_ref):
    lam = lam_ref[0]
    tq = ATTN_TQ
    seq = k_ref.shape[0]
    k = k_ref[...]
    v = v_ref[...]
    row = lax.broadcasted_iota(jnp.int32, (2 * tq, tq), 0)
    col = lax.broadcasted_iota(jnp.int32, (2 * tq, tq), 1)
    visible = (col // CHUNK) <= ((row % tq) // CHUNK)
    n_tiles = seq // tq

    def scores(qi):
        lo = qi * tq
        qq = _split_components(q_ref[lo:lo + tq, :])
        s_d = jnp.where(visible, _dot_nt(qq, k[lo:lo + tq]), -jnp.inf)
        return s_d, (_dot_nt(qq, k[:lo]) if qi > 0 else None)

    def weighted_values(qi, w_d, w_m):
        lo = qi * tq
        o = _dot(w_d, v[lo:lo + tq])
        if qi > 0:
            o = o + _dot(w_m, v[:lo])
        o_ref[lo:lo + tq, :] = o

    ahead = scores(0)
    behind = None
    for qi in range(n_tiles):
        s_d, s_m = ahead
        if qi + 1 < n_tiles:
            ahead = scores(qi + 1)
        if behind is not None:
            weighted_values(*behind)
        m = jnp.max(s_d, axis=-1, keepdims=True)
        if qi > 0:
            m = jnp.maximum(m, jnp.max(s_m, axis=-1, keepdims=True))
            p_m = jnp.exp2(s_m - m)
        p_d = jnp.exp2(s_d - m)
        l = jnp.sum(p_d, axis=-1, keepdims=True)
        if qi > 0:
            l = l + jnp.sum(p_m, axis=-1, keepdims=True)
        r = 1.0 / l
        r1 = r[:tq]
        r2 = lam * r[tq:]
        w_d = (p_d[:tq] * r1 - p_d[tq:] * r2).astype(BF16)
        w_m = (p_m[:tq] * r1 - p_m[tq:] * r2).astype(BF16) if qi > 0 else None
        behind = (qi, w_d, w_m)
    weighted_values(*behind)


def _attn_prompt(lam, z, kb, vb, n_batch, seq):
    return pl.pallas_call(
        _attn_prompt_kernel,
        grid=(n_batch, HEADS),
        in_specs=[
            pl.BlockSpec(memory_space=pltpu.SMEM),
            pl.BlockSpec((seq, HEAD_W), lambda b, h: (b, ZC_Q * HEADS + h)),
            pl.BlockSpec((seq, HEAD_W), lambda b, h: (b, h)),
            pl.BlockSpec((seq, HEAD_W), lambda b, h: (b, h)),
        ],
        out_specs=pl.BlockSpec((seq, HEAD_W), lambda b, h: (b, h)),
        out_shape=jax.ShapeDtypeStruct((n_batch * seq, D_MODEL), F32),
        compiler_params=_cparams("arbitrary", "arbitrary"),
        name="attn_prompt",
    )(lam, z, kb, vb)


def _sample_attention(lam, q, kn, vn, ck_ref, cv_ref):
    n_q = q.shape[0]
    qq = _split_components(q)
    s_c = _dot_nt(qq, ck_ref[...].astype(BF16))
    s_n = _dot_nt(qq, kn)
    m = jnp.maximum(jnp.max(s_c, axis=-1, keepdims=True), jnp.max(s_n, axis=-1, keepdims=True))
    p_c = jnp.exp2(s_c - m)
    p_n = jnp.exp2(s_n - m)
    r = 1.0 / (jnp.sum(p_c, axis=-1, keepdims=True) + jnp.sum(p_n, axis=-1, keepdims=True))
    r1 = r[:n_q]
    r2 = lam * r[n_q:]
    w_c = (p_c[:n_q] * r1 - p_c[n_q:] * r2).astype(BF16)
    w_n = (p_n[:n_q] * r1 - p_n[n_q:] * r2).astype(BF16)
    return _dot(w_c, cv_ref[...].astype(BF16)) + _dot(w_n, vn)


def _split3(x):
    h1 = x.astype(BF16)
    r1 = x - h1.astype(F32)
    h2 = r1.astype(BF16)
    h3 = (r1 - h2.astype(F32)).astype(BF16)
    return h1, h2, h3


def _when_step(step, n_steps, which):
    def deco(body):
        if n_steps == 1:
            body()
        else:
            pl.when(step == which)(body)
    return deco


def _hgrn_kernel(n_chunks, side, *refs):
    if side is None:
        f_ref, q_ref, i_ref, s0_ref, o_ref, sout_ref, st_ref = refs
    else:
        (f_ref, q_ref, i_ref, s0_ref, lam_ref, qs_ref, kns_ref, vns_ref, ck_hbm, cv_hbm,
         o_ref, sout_ref, os_ref, st_ref, kbuf, vbuf, sem) = refs
    c = pl.program_id(1)
    n_group = f_ref.shape[0]

    @_when_step(c, n_chunks, 0)
    def _():
        for g in range(n_group):
            for h in range(HEADS):
                st_ref[g, h] = s0_ref[g, h].T

    if side is not None:
        pairs_per_step, n_pairs = side
        step = pl.program_id(0) * n_chunks + c

        def cache_copies(pair, sl):
            bb, hh = pair // HEADS, pair % HEADS
            part = kbuf.shape[1] // CACHE_COPY_PARTS
            copies = []
            for j in range(CACHE_COPY_PARTS):
                keys = pl.ds(j * part, part)
                copies.append(pltpu.make_async_copy(ck_hbm.at[bb, keys, hh, :], kbuf.at[sl, keys], sem.at[0, sl]))
                copies.append(pltpu.make_async_copy(cv_hbm.at[bb, keys, hh, :], vbuf.at[sl, keys], sem.at[1, sl]))
            return copies

        @pl.when(step == 0)
        def _():
            for cp in cache_copies(0, 0):
                cp.start()

        def sample_pair(j):
            pair = step * pairs_per_step + j
            sl = j % 2

            @pl.when(pair + 1 < n_pairs)
            def _():
                for cp in cache_copies(pair + 1, 1 - sl):
                    cp.start()

            for cp in cache_copies(pair, sl):
                cp.wait()
            lanes = slice(j * HEAD_W, (j + 1) * HEAD_W)
            os_ref[:, lanes] = _sample_attention(lam_ref[0], qs_ref[:, lanes], kns_ref[:, lanes], vns_ref[:, lanes],
                                                 kbuf.at[sl], vbuf.at[sl])
    else:
        pairs_per_step = 0

        def sample_pair(j):
            del j

    def sample_pairs(quarter):
        for j in range(quarter * pairs_per_step // 4, (quarter + 1) * pairs_per_step // 4):
            sample_pair(j)

    row = lax.broadcasted_iota(jnp.int32, (CHUNK, CHUNK), 0)
    col = lax.broadcasted_iota(jnp.int32, (CHUNK, CHUNK), 1)
    causal = row >= col
    tril = jnp.where(causal, 1.0, 0.0).astype(BF16)
    heads = [slice(h * HEAD_W, (h + 1) * HEAD_W) for h in range(HEADS)]
    groups = range(n_group)
    sample_pairs(0)
    fs = [f_ref[g] for g in groups]
    splits = [_split3(jnp.log(f)) for f in fs]
    bs = [_dot(tril, h1) + _dot(tril, h2) + _dot(tril, h3) for h1, h2, h3 in splits]
    scaled = []
    for g in groups:
        b = bs[g]
        b_last = b[CHUNK - 1:CHUNK, :]
        rk = 1.0 - fs[g]
        v = i_ref[g]
        scaled.append(((q_ref[g] * jnp.exp(b)).astype(BF16), (rk * jnp.exp(-b)).astype(BF16),
                       (rk * jnp.exp(b_last - b)).astype(BF16), v, v.astype(BF16), jnp.exp(b_last)))
    sample_pairs(1)
    first = []
    for g in groups:
        qd_b, kd_b, kl_b, v, v_b, eb_last = scaled[g]
        states = [st_ref[g, h] for h in range(HEADS)]
        a_all = [_dot_nt(qd_b[:, sl], kd_b[:, sl]) for sl in heads]
        o_state = [_dot_nt(qd_b[:, sl], st.astype(BF16)) for sl, st in zip(heads, states)]
        s_upd = [_dot(v[:, sl].T.astype(BF16), kl_b[:, sl]) for sl in heads]
        first.append((states, a_all, o_state, s_upd))
    sample_pairs(2)
    for g in groups:
        v_b, eb_last = scaled[g][4], scaled[g][5]
        states, a_all, o_state, s_upd = first[g]
        for h, sl in enumerate(heads):
            a = jnp.where(causal, a_all[h], 0.0).astype(BF16)
            o_ref[g, :, sl] = o_state[h] + _dot(a, v_b[:, sl])
            st_ref[g, h] = states[h] * eb_last[:, sl] + s_upd[h]
    sample_pairs(3)

    @_when_step(c, n_chunks, n_chunks - 1)
    def _():
        for g in range(n_group):
            for h in range(HEADS):
                sout_ref[g, h] = st_ref[g, h].T


def _hgrn(z, s0, n_batch, n_chunks, n_group, sample=None):
    z3 = z.reshape(n_batch, n_chunks * CHUNK, N_Z_BLOCKS * D_MODEL)
    n_steps = (n_batch // n_group) * n_chunks

    def zblock(zc):
        return pl.BlockSpec((n_group, CHUNK, D_MODEL), lambda b, c: (b, c, zc))

    state = pl.BlockSpec((n_group, HEADS, HEAD_W, HEAD_W), lambda b, c: (b, 0, 0, 0))
    in_specs = [zblock(ZC_F), zblock(ZC_RQ), zblock(ZC_RI), state]
    operands = [z3, z3, z3, s0]
    out_specs = [pl.BlockSpec((n_group, CHUNK, D_MODEL), lambda b, c: (b, c, 0)), state]
    out_shape = [
        jax.ShapeDtypeStruct((n_batch, n_chunks * CHUNK, D_MODEL), F32),
        jax.ShapeDtypeStruct((n_batch, HEADS, HEAD_W, HEAD_W), F32),
    ]
    scratch = [pltpu.VMEM((n_group, HEADS, HEAD_W, HEAD_W), F32)]
    side = None
    if sample is not None:
        lam, zs, kb_s, vb_s, cache_k, cache_v, n_sb, n_q = sample
        n_pairs = n_sb * HEADS
        pps = n_pairs // n_steps
        assert pps * n_steps == n_pairs and pps % 4 == 0 and HEADS % pps == 0
        col_groups = HEADS // pps
        past = cache_k.shape[1]
        side = (pps, n_pairs)

        def srow(b, c):
            return (b * n_chunks + c) // col_groups

        def scol(b, c):
            return (b * n_chunks + c) % col_groups

        in_specs += [
            pl.BlockSpec(memory_space=pltpu.SMEM),
            pl.BlockSpec((n_q, pps * HEAD_W), lambda b, c: (srow(b, c), ZC_Q * col_groups + scol(b, c))),
            pl.BlockSpec((n_q, pps * HEAD_W), lambda b, c: (srow(b, c), scol(b, c))),
            pl.BlockSpec((n_q, pps * HEAD_W), lambda b, c: (srow(b, c), scol(b, c))),
            pl.BlockSpec(memory_space=pl.ANY), pl.BlockSpec(memory_space=pl.ANY),
        ]
        operands += [lam, zs, kb_s, vb_s, cache_k, cache_v]
        out_specs.append(pl.BlockSpec((n_q, pps * HEAD_W), lambda b, c: (srow(b, c), scol(b, c))))
        out_shape.append(jax.ShapeDtypeStruct((n_sb * n_q, D_MODEL), F32))
        scratch += [pltpu.VMEM((2, past, HEAD_W), F32), pltpu.VMEM((2, past, HEAD_W), F32),
                    pltpu.SemaphoreType.DMA((2, 2))]
    outs = pl.pallas_call(
        functools.partial(_hgrn_kernel, n_chunks, side),
        grid=(n_batch // n_group, n_chunks),
        in_specs=in_specs,
        out_specs=out_specs,
        out_shape=out_shape,
        scratch_shapes=scratch,
        compiler_params=_cparams("arbitrary", "arbitrary"),
        name="hgrn",
    )(*operands)
    return (outs[0].reshape(n_batch * n_chunks * CHUNK, D_MODEL),) + tuple(outs[1:])


def _head_rmsnorm(x, g):
    outs = []
    for h in range(HEADS):
        xh = x[:, h * HEAD_W:(h + 1) * HEAD_W]
        ms = jnp.mean(xh * xh, axis=-1, keepdims=True)
        outs.append(xh * lax.rsqrt(ms + NORM_EPS) * g)
    return jnp.concatenate(outs, axis=-1)


def _merge_kernel(lam_init, ao_ref, ro_ref, rg_ref, ga_ref, gr_ref, x_ref, gs_ref, grn_ref,
                  woa_ref, wor_ref, wout_ref, gffn_ref, wr_ref, br_ref, cnt_in_ref,
                  x1_ref, xn_ref, idx_ref, prob_ref, rank_ref, cnt_ref, run_ref):
    i = pl.program_id(0)
    tm = x_ref.shape[0]

    @pl.when(i == 0)
    def _():
        run_ref[...] = cnt_in_ref[...]

    part = tm // MERGE_PARTS

    def rows(p):
        return slice(p * part, (p + 1) * part)

    def norm_inputs(p, _):
        an = (_head_rmsnorm(ao_ref[rows(p), :], gs_ref[...]) * (1.0 - lam_init)).astype(BF16)
        rn = (_head_rmsnorm(ro_ref[rows(p), :], grn_ref[...]) * jax.nn.silu(rg_ref[rows(p), :])).astype(BF16)
        return an, rn

    def branch_projections(p, st):
        an, rn = st
        return _dot(an, woa_ref[...]), _dot(rn, wor_ref[...])

    def gate(p, st):
        a, r = st
        return (jax.nn.sigmoid(ga_ref[rows(p), :]) * a + jax.nn.sigmoid(gr_ref[rows(p), :]) * r).astype(BF16)

    def residual(p, mixed):
        x1 = x_ref[rows(p), :] + _dot(mixed, wout_ref[...])
        x1_ref[rows(p), :] = x1
        return x1

    def ffn_norm_router(p, x1):
        ms = jnp.mean(x1 * x1, axis=-1, keepdims=True)
        xn = x1 * lax.rsqrt(ms + NORM_EPS) * gffn_ref[...]
        for s in range(ROW_SUB):
            xn_ref[pl.ds(p * part * ROW_SUB + s, part, stride=ROW_SUB), :] = xn[:, s * HEAD_W:(s + 1) * HEAD_W]
        return _dot(xn.astype(BF16), wr_ref[...])

    stages = (norm_inputs, branch_projections, gate, residual, ffn_norm_router)
    state = [None] * MERGE_PARTS
    for stage in stages:
        for p in range(MERGE_PARTS):
            state[p] = stage(p, state[p])
    logits = jnp.concatenate(state, axis=0) + br_ref[...]
    lane = lax.broadcasted_iota(jnp.int32, logits.shape, 1).astype(F32)
    sel = jnp.zeros(logits.shape, F32)
    work = logits
    tops, idxs = [], []
    for _ in range(TOP_K):
        m = jnp.max(work, axis=-1, keepdims=True)
        idx = jnp.min(jnp.where(work == m, lane, float(N_EXPERTS)), axis=-1, keepdims=True)
        hit = lane == idx
        sel = jnp.where(hit, 1.0, sel)
        work = jnp.where(hit, -jnp.inf, work)
        tops.append(m)
        idxs.append(idx)
    es = [jnp.exp(v - tops[0]) for v in tops]
    inv = 1.0 / (es[0] + es[1] + es[2] + es[3])

    trow = lax.broadcasted_iota(jnp.int32, (tm, tm), 0)
    tcol = lax.broadcasted_iota(jnp.int32, (tm, tm), 1)
    before = jnp.where(trow > tcol, 1.0, 0.0).astype(BF16)
    ranks = _dot(before, sel.astype(BF16)) + run_ref[...]
    k_lane = lax.broadcasted_iota(jnp.int32, (tm, TOP_K), 1)
    idx_out = jnp.zeros((tm, TOP_K), F32)
    prob_out = jnp.zeros((tm, TOP_K), F32)
    rank_out = jnp.zeros((tm, TOP_K), F32)
    for k in range(TOP_K):
        rk = jnp.sum(jnp.where(lane == idxs[k], ranks, 0.0), axis=-1, keepdims=True)
        idx_out = jnp.where(k_lane == k, idxs[k], idx_out)
        prob_out = jnp.where(k_lane == k, es[k] * inv, prob_out)
        rank_out = jnp.where(k_lane == k, rk, rank_out)
    idx_ref[...] = idx_out.astype(jnp.int32)
    prob_ref[...] = prob_out
    rank_ref[...] = rank_out.astype(jnp.int32)
    run = run_ref[...] + jnp.sum(sel, axis=0, keepdims=True)
    run_ref[...] = run
    cnt_ref[...] = run


def _merge(lam_init, ao, ro, z, x, g_subln, g_rec, woa, wor, wout, g_ffn, w_router, b_router, cnt_in):
    t = x.shape[0]
    tm = MERGE_TM
    row = lambda i: (i, 0)
    fixed = lambda i: (0, 0)
    tok = pl.BlockSpec((tm, D_MODEL), row)
    wspec = pl.BlockSpec((D_MODEL, D_MODEL), fixed)
    narrow = pl.BlockSpec((tm, TOP_K), row)
    return pl.pallas_call(
        functools.partial(_merge_kernel, lam_init),
        grid=(t // tm,),
        in_specs=[
            tok, tok,
            pl.BlockSpec((tm, D_MODEL), lambda i: (i, ZC_RG)),
            pl.BlockSpec((tm, D_MODEL), lambda i: (i, ZC_GA)),
            pl.BlockSpec((tm, D_MODEL), lambda i: (i, ZC_GR)),
            tok,
            pl.BlockSpec((1, HEAD_W), fixed), pl.BlockSpec((1, HEAD_W), fixed),
            wspec, wspec, wspec,
            pl.BlockSpec((1, D_MODEL), fixed),
            pl.BlockSpec((D_MODEL, N_EXPERTS), fixed),
            pl.BlockSpec((1, N_EXPERTS), fixed),
            pl.BlockSpec((1, N_EXPERTS), fixed),
        ],
        out_specs=[tok, pl.BlockSpec((tm * ROW_SUB, HEAD_W), row), narrow, narrow, narrow,
                   pl.BlockSpec((1, N_EXPERTS), fixed)],
        out_shape=[
            jax.ShapeDtypeStruct((t, D_MODEL), F32),
            jax.ShapeDtypeStruct((t * ROW_SUB, HEAD_W), F32),
            jax.ShapeDtypeStruct((t, TOP_K), jnp.int32),
            jax.ShapeDtypeStruct((t, TOP_K), F32),
            jax.ShapeDtypeStruct((t, TOP_K), jnp.int32),
            jax.ShapeDtypeStruct((1, N_EXPERTS), F32),
        ],
        scratch_shapes=[pltpu.VMEM((1, N_EXPERTS), F32)],
        compiler_params=_cparams("arbitrary"),
        name="merge",
    )(ao, ro, z, z, z, x, g_subln, g_rec, woa, wor, wout, g_ffn, w_router, b_router, cnt_in)


def _row_copy(src, dst, sem):
    return pltpu.make_async_copy(src, dst, sem)


def _dispatch_kernel(n_first, n_steps, pos_ref, xa_hbm, xb_hbm, xs_hbm, stage, load_sem, copy_sem):
    i = pl.program_id(0)
    tm = ROW_TM
    rows = tm * ROW_SUB
    slot = i % DISPATCH_SLOTS

    def row_tile(off):
        return pl.ds(pl.multiple_of(off, ROW_SUB), ROW_SUB)

    def load(t, sl):
        @pl.when(t < n_first)
        def _():
            _row_copy(xa_hbm.at[pl.ds(pl.multiple_of(t * rows, rows), rows)], stage.at[sl], load_sem.at[sl]).start()

        @pl.when(t >= n_first)
        def _():
            _row_copy(xb_hbm.at[pl.ds(pl.multiple_of((t - n_first) * rows, rows), rows)], stage.at[sl],
                      load_sem.at[sl]).start()

    def drain(parity):
        for k in range(TOP_K):
            _row_copy(stage.at[0], xs_hbm.at[pl.ds(0, rows)], copy_sem.at[parity]).wait()

    @pl.when(i == 0)
    def _():
        load(i, slot)

    @pl.when(i + 1 < n_steps)
    def _():
        load(i + 1, (i + 1) % DISPATCH_SLOTS)

    _row_copy(xa_hbm.at[pl.ds(0, rows)], stage.at[slot], load_sem.at[slot]).wait()

    def start(g, carry):
        r0 = g * ROW_COPY_UNROLL
        n = ROW_COPY_UNROLL * TOP_K
        dst = [pos_ref[0, 0, r0 * TOP_K + j] for j in range(n)]
        for j in range(n):
            _row_copy(stage.at[slot, row_tile((r0 + j // TOP_K) * ROW_SUB)],
                      xs_hbm.at[row_tile(dst[j])], copy_sem.at[i % 2]).start()
        return carry

    lax.fori_loop(0, tm // ROW_COPY_UNROLL, start, 0)

    @pl.when(i > 0)
    def _():
        drain((i - 1) % 2)

    @pl.when(i == n_steps - 1)
    def _():
        drain(i % 2)


def _dispatch(pos, xn_a, xn_b):
    tm = ROW_TM
    n_a, n_b = xn_a.shape[0] // (tm * ROW_SUB), xn_b.shape[0] // (tm * ROW_SUB)
    pos3 = pos.reshape(n_a + n_b, 1, tm * TOP_K)
    return pl.pallas_call(
        functools.partial(_dispatch_kernel, n_a, n_a + n_b),
        grid=(n_a + n_b,),
        in_specs=[
            pl.BlockSpec((1, 1, tm * TOP_K), lambda i: (i, 0, 0), memory_space=pltpu.SMEM),
            pl.BlockSpec(memory_space=pl.ANY),
            pl.BlockSpec(memory_space=pl.ANY),
        ],
        out_specs=pl.BlockSpec(memory_space=pl.ANY),
        out_shape=jax.ShapeDtypeStruct((pos.size * ROW_SUB, HEAD_W), F32),
        scratch_shapes=[
            pltpu.VMEM((DISPATCH_SLOTS, tm * ROW_SUB, HEAD_W), F32),
            pltpu.SemaphoreType.DMA((DISPATCH_SLOTS,)),
            pltpu.SemaphoreType.DMA((2,)),
        ],
        compiler_params=_cparams("arbitrary"),
        name="dispatch",
    )(pos3, xn_a, xn_b)


def _experts_kernel(tile_ref, exp_ref, lo_ref, hi_ref, fresh_ref, par_ref, nxt_ref, more_ref,
                    xs_ref, wg_hbm, bg_ref, wu_hbm, bu_ref, wd_hbm, bd_ref, o_ref,
                    wg_s, wu_s, wd_s, x_s, acc_s, wf_s, wsem):
    w = pl.program_id(0)
    lo = lo_ref[w]
    hi = hi_ref[w]
    tm = x_s.shape[0]

    def lanes(s):
        return slice(s * HEAD_W, (s + 1) * HEAD_W)

    def sub_rows(s):
        return pl.ds(s, tm, stride=ROW_SUB)

    def weight_copies(e, sl):
        return [pltpu.make_async_copy(src.at[e], wf_s.at[sl, j], wsem.at[sl])
                for j, src in enumerate((wg_hbm, wu_hbm, wd_hbm))]

    @pl.when(w == 0)
    def _():
        for c in weight_copies(exp_ref[0], par_ref[0]):
            c.start()
        acc_s[...] = jnp.zeros(acc_s.shape, F32)

    @pl.when(fresh_ref[w] == 1)
    def _():
        sl = par_ref[w]
        for c in weight_copies(exp_ref[w], sl):
            c.wait()

        @pl.when(more_ref[w] == 1)
        def _():
            for c in weight_copies(nxt_ref[w], 1 - sl):
                c.start()

        wg_s[...] = wf_s[sl, 0].astype(BF16)
        wu_s[...] = wf_s[sl, 1].astype(BF16)
        wd_s[...] = wf_s[sl, 2].astype(BF16)

    @pl.when(hi > lo)
    def _():
        for s in range(ROW_SUB):
            x_s[:, lanes(s)] = xs_ref[sub_rows(s), :].astype(BF16)
        x = x_s[...]
        out = bd_ref[0]
        def gate_up(c):
            ff = slice(c, c + MOE_FF_SLICE)
            return _dot(x, wg_s[:, ff]) + bg_ref[0, :, ff], _dot(x, wu_s[:, ff]) + bu_ref[0, :, ff]

        slices = range(0, D_MODEL, MOE_FF_SLICE)
        gus = [gate_up(c) for c in slices]
        acts = []
        for gate, up in gus:
            gate = jnp.minimum(gate, SWIGLU_LIMIT)
            up = jnp.clip(up, -SWIGLU_LIMIT, SWIGLU_LIMIT)
            acts.append((gate * jax.nn.sigmoid(SWIGLU_ALPHA * gate) * (up + 1.0)).astype(BF16))
        for c, act in zip(slices, acts):
            out = out + _dot(act, wd_s[c:c + MOE_FF_SLICE, :])
        row = lax.broadcasted_iota(jnp.int32, (tm, 1), 0)
        merged = jnp.where((row >= lo) & (row < hi), out, acc_s[...])
        acc_s[...] = merged
        for s in range(ROW_SUB):
            o_ref[sub_rows(s), :] = merged[:, lanes(s)]


def _experts(plan, xs, w_gate, b_gate, w_up, b_up, w_down, b_down):
    tm = MOE_TM
    n_items = plan[0].shape[0]
    rows = lambda w, tile, *_: (tile[w], 0)
    wspec = pl.BlockSpec(memory_space=pl.ANY)
    bspec = pl.BlockSpec((1, 1, D_MODEL), lambda w, tile, exp, *_: (exp[w], 0, 0))
    grid_spec = pltpu.PrefetchScalarGridSpec(
        num_scalar_prefetch=len(plan),
        grid=(n_items,),
        in_specs=[pl.BlockSpec((tm * ROW_SUB, HEAD_W), rows), wspec, bspec, wspec, bspec, wspec, bspec],
        out_specs=pl.BlockSpec((tm * ROW_SUB, HEAD_W), rows),
        scratch_shapes=[pltpu.VMEM((D_MODEL, D_MODEL), BF16)] * 3 + [
            pltpu.VMEM((tm, D_MODEL), BF16),
            pltpu.VMEM((tm, D_MODEL), F32),
            pltpu.VMEM((2, 3, D_MODEL, D_MODEL), F32),
            pltpu.SemaphoreType.DMA((2,)),
        ],
    )
    return pl.pallas_call(
        _experts_kernel,
        grid_spec=grid_spec,
        out_shape=jax.ShapeDtypeStruct(xs.shape, F32),
        compiler_params=_cparams("arbitrary"),
        name="experts",
    )(*plan, xs, w_gate, b_gate.reshape(N_EXPERTS, 1, D_MODEL), w_up, b_up.reshape(N_EXPERTS, 1, D_MODEL),
      w_down, b_down.reshape(N_EXPERTS, 1, D_MODEL))


def _expert_plan(counts, n_rows):
    tm = MOE_TM
    n_tiles = n_rows // tm
    n_items = n_tiles + N_EXPERTS - 1
    ends = jnp.cumsum(counts)
    starts = ends - counts
    first_tile = starts // tm
    last_tile = jnp.maximum(ends - 1, 0) // tm
    items = jnp.where(counts > 0, last_tile - first_tile + 1, 0)
    item_end = jnp.cumsum(items)
    item_start = item_end - items
    total = item_end[-1]
    w = jnp.arange(n_items, dtype=jnp.int32)
    wc = jnp.minimum(w, total - 1)
    exp = jnp.sum((item_end[None, :] <= wc[:, None]).astype(jnp.int32), axis=1)
    ids = jnp.arange(N_EXPERTS, dtype=jnp.int32)
    mine = exp[:, None] == ids[None, :]

    def of_item(per_expert):
        return jnp.sum(jnp.where(mine, per_expert[None, :], 0), axis=1).astype(jnp.int32)

    tile = of_item(first_tile) + wc - of_item(item_start)
    lo = jnp.maximum(of_item(starts), tile * tm) - tile * tm
    hi = jnp.minimum(of_item(ends), (tile + 1) * tm) - tile * tm
    valid = w < total
    lo = jnp.where(valid, lo, 0).astype(jnp.int32)
    hi = jnp.where(valid, hi, 0).astype(jnp.int32)
    prev_exp = jnp.concatenate([jnp.full((1,), -1, jnp.int32), exp[:-1]])
    fresh = (valid & (exp != prev_exp)).astype(jnp.int32)
    later = jnp.where((counts > 0)[None, :] & (ids[None, :] > ids[:, None]), ids[None, :], N_EXPERTS)
    next_exp = jnp.min(later, axis=1)
    parity = ((jnp.cumsum((counts > 0).astype(jnp.int32)) - 1) % 2).astype(jnp.int32)
    next_of_item = of_item(next_exp)
    more = (next_of_item < N_EXPERTS).astype(jnp.int32)
    nxt = jnp.where(more == 1, next_of_item, exp).astype(jnp.int32)
    return tile, exp, lo, hi, fresh, of_item(parity), nxt, more


def _combine_kernel(n_steps, pos_ref, pos_next_ref, x1_ref, prob_ref, gfin_ref, ys_ref, y_ref, buf_ref, sem):
    i = pl.program_id(0)
    tm = x1_ref.shape[0]
    slot = i % 2

    def row_tile(off):
        return pl.ds(pl.multiple_of(off, ROW_SUB), ROW_SUB)

    def gather(p_ref, sl):
        def start(g, carry):
            r0 = g * ROW_COPY_UNROLL
            n = ROW_COPY_UNROLL * TOP_K
            src = [p_ref[0, 0, r0 * TOP_K + j] for j in range(n)]
            for j in range(n):
                _row_copy(ys_ref.at[row_tile(src[j])],
                          buf_ref.at[sl, j % TOP_K, row_tile((r0 + j // TOP_K) * ROW_SUB)], sem.at[sl]).start()
            return carry

        lax.fori_loop(0, tm // ROW_COPY_UNROLL, start, 0)

    @pl.when(i == 0)
    def _():
        gather(pos_ref, slot)

    @pl.when(i + 1 < n_steps)
    def _():
        gather(pos_next_ref, 1 - slot)

    for k in range(TOP_K):
        _row_copy(ys_ref.at[pl.ds(0, tm * ROW_SUB)], buf_ref.at[slot, k], sem.at[slot]).wait()

    prob = prob_ref[...]
    parts = []
    for s in range(ROW_SUB):
        part = x1_ref[:, s * HEAD_W:(s + 1) * HEAD_W]
        for k in range(TOP_K):
            part = part + prob[:, k:k + 1] * buf_ref[slot, k, pl.ds(s, tm, stride=ROW_SUB), :]
        parts.append(part)
    x2 = jnp.concatenate(parts, axis=-1)
    ms = jnp.mean(x2 * x2, axis=-1, keepdims=True)
    y_ref[...] = x2 * lax.rsqrt(ms + NORM_EPS) * gfin_ref[...]


def _combine(pos, x1, prob, g_final, ys):
    t = x1.shape[0]
    tm = ROW_TM
    n = t // tm
    pos3 = pos.reshape(n, 1, tm * TOP_K)
    return pl.pallas_call(
        functools.partial(_combine_kernel, n),
        grid=(n,),
        in_specs=[
            pl.BlockSpec((1, 1, tm * TOP_K), lambda i: (i, 0, 0), memory_space=pltpu.SMEM),
            pl.BlockSpec((1, 1, tm * TOP_K), lambda i: (jnp.minimum(i + 1, n - 1), 0, 0), memory_space=pltpu.SMEM),
            pl.BlockSpec((tm, D_MODEL), lambda i: (i, 0)),
            pl.BlockSpec((tm, TOP_K), lambda i: (i, 0)),
            pl.BlockSpec((1, D_MODEL), lambda i: (0, 0)),
            pl.BlockSpec(memory_space=pl.ANY),
        ],
        out_specs=pl.BlockSpec((tm, D_MODEL), lambda i: (i, 0)),
        out_shape=jax.ShapeDtypeStruct((t, D_MODEL), F32),
        scratch_shapes=[pltpu.VMEM((2, TOP_K, tm * ROW_SUB, HEAD_W), F32), pltpu.SemaphoreType.DMA((2,))],
        compiler_params=_cparams("arbitrary"),
        name="combine",
    )(pos3, pos3, x1, prob, g_final, ys)


def _lambda_init(layer):
    return 0.8 - 0.6 * math.exp(-0.3 * layer)


def kernel(x_prompt, x_sample, cache_k, cache_v, state_rec, g_mix, w_in, lambda_q1, lambda_k1, lambda_q2, lambda_k2, g_subln, w_o_attn, lb_logits, g_rec_norm, w_o_rec, w_out, g_ffn, w_router, b_router, w_gate, b_gate, w_up, b_up, w_down, b_down, g_final):
    nb, seq, _ = x_prompt.shape
    nbd, ld, _ = x_sample.shape
    past = cache_k.shape[2]
    tp, ts = nb * seq, nbd * ld
    lam_init = _lambda_init(0)
    lam = (jnp.exp(jnp.sum(lambda_q1[0].astype(F32) * lambda_k1[0].astype(F32)))
           - jnp.exp(jnp.sum(lambda_q2[0].astype(F32) * lambda_k2[0].astype(F32))) + lam_init).reshape(1)

    w_in_b = w_in[0].astype(BF16)
    woa, wor, wout = w_o_attn[0].astype(BF16), w_o_rec[0].astype(BF16), w_out[0].astype(BF16)
    wr = w_router[0].astype(BF16)
    g_mix2, g_ffn2, g_fin2 = g_mix[0].reshape(1, -1), g_ffn[0].reshape(1, -1), g_final.reshape(1, -1)
    gs2, grn2 = g_subln[0].reshape(1, -1), g_rec_norm[0].reshape(1, -1)
    br2 = b_router[0].reshape(1, -1)

    cos_p, sin_p = _rope_tables(jnp.arange(seq, dtype=jnp.int32))
    pos_s = past + (jnp.arange(PROJ_TM, dtype=jnp.int32) % ld)
    cos_s, sin_s = _rope_tables(pos_s)

    xp = x_prompt.reshape(tp, D_MODEL)
    xs_in = x_sample.reshape(ts, D_MODEL)
    zp, k_p, v_p, kb_p, vb_p = _project(xp, g_mix2, w_in_b, cos_p, sin_p, lb_logits)
    zs, k_s, v_s, kb_s, vb_s = _project(xs_in, g_mix2, w_in_b, cos_s, sin_s, lb_logits)

    ao_p = _attn_prompt(lam, zp, kb_p, vb_p, nb, seq)
    ro_p, st_p, ao_s = _hgrn(zp, jnp.zeros((nb, HEADS, HEAD_W, HEAD_W), F32), nb, seq // CHUNK, HGRN_GROUP_PROMPT,
                             sample=(lam, zs, kb_s, vb_s, cache_k[0], cache_v[0], nbd, ld))
    ro_s, st_s = _hgrn(zs, state_rec[0], nbd, ld // CHUNK, HGRN_GROUP_SAMPLE)

    merge_w = (gs2, grn2, woa, wor, wout, g_ffn2, wr, br2)
    zero_cnt = jnp.zeros((1, N_EXPERTS), F32)
    x1_p, xn_p, idx_p, prob_p, rank_p, cnt_p = _merge(lam_init, ao_p, ro_p, zp, xp, *merge_w, zero_cnt)
    x1_s, xn_s, idx_s, prob_s, rank_s, cnt = _merge(lam_init, ao_s, ro_s, zs, xs_in, *merge_w, cnt_p)

    counts = cnt[0].astype(jnp.int32)
    starts = jnp.cumsum(counts) - counts
    experts = jnp.arange(N_EXPERTS, dtype=jnp.int32)

    def positions(idx, rank):
        return (jnp.sum(jnp.where(idx[..., None] == experts, starts, 0), axis=-1) + rank) * ROW_SUB

    pos_p = positions(idx_p, rank_p)
    pos_s2 = positions(idx_s, rank_s)
    n_rows = (tp + ts) * TOP_K
    plan = _expert_plan(counts, n_rows)

    xs_sorted = _dispatch(jnp.concatenate([pos_p, pos_s2], axis=0), xn_p, xn_s)
    ys = _experts(plan, xs_sorted, w_gate[0], b_gate[0], w_up[0], b_up[0], w_down[0], b_down[0])
    y_p = _combine(pos_p, x1_p, prob_p, g_fin2, ys)
    y_s = _combine(pos_s2, x1_s, prob_s, g_fin2, ys)

    return (y_p.reshape(nb, seq, D_MODEL), y_s.reshape(nbd, ld, D_MODEL),
            k_p.reshape(1, nb, seq, HEADS, HEAD_W), v_p.reshape(1, nb, seq, HEADS, HEAD_W), st_p[None],
            k_s.reshape(1, nbd, ld, HEADS, HEAD_W), v_s.reshape(1, nbd, ld, HEADS, HEAD_W), st_s[None])
```

```python
import functools
import math

import jax
import jax.numpy as jnp
from jax import lax
from jax.experimental import pallas as pl
from jax.experimental.pallas import tpu as pltpu

F32 = jnp.float32
BF16 = jnp.bfloat16

D_MODEL = 1024
CHUNK = 64
HEADS = 8
HEAD_W = 128
ATTN_HEAD_DIM = 64
ROPE_THETA = 10000.0
Q_SCALE = ATTN_HEAD_DIM ** -0.5 * math.log2(math.e)
N_EXPERTS = 32
TOP_K = 4
SWIGLU_ALPHA = 1.702
SWIGLU_LIMIT = 7.0
NORM_EPS = 1e-6
N_COL_BLOCKS = 9
COL_Q, COL_K, COL_V, COL_F = range(4)
N_Z_BLOCKS = 7
ZC_Q, ZC_F, ZC_RQ, ZC_RI, ZC_RG, ZC_GA, ZC_GR = range(7)
PROJ_TM = 256

ATTN_TQ = 256
CACHE_COPY_PARTS = 4
SAMPLE_CACHE_SLOTS = 3
HGRN_GROUP_PROMPT = 4
HGRN_GROUP_SAMPLE = 4
MERGE_TM = 512
MERGE_PARTS = 2
MOE_TM = 256
MOE_FF_SLICE = 256
ROW_TM = 256
ROW_COPY_UNROLL = 4
DISPATCH_SLOTS = 3
ROW_SUB = D_MODEL // HEAD_W
VMEM_LIMIT = 56 * 1024 * 1024


def _cparams(*sem):
    return pltpu.CompilerParams(dimension_semantics=sem, vmem_limit_bytes=VMEM_LIMIT)


def _dot(a, b):
    return jnp.dot(a, b, preferred_element_type=F32)


def _dot_nt(a, b):
    return lax.dot_general(a, b, (((1,), (1,)), ((), ())), preferred_element_type=F32)


def _proj_kernel(x_ref, g_ref, w_ref, cos_ref, sin_ref, lbl_ref,
                 z_ref, kout_ref, vout_ref, kb_ref, vb_ref):
    tm = x_ref.shape[0]
    x = x_ref[...]
    ms = jnp.mean(x * x, axis=-1, keepdims=True)
    hn = (x * lax.rsqrt(ms + NORM_EPS) * g_ref[...]).astype(BF16)

    def col(j):
        return _dot(hn, w_ref[:, j * D_MODEL:(j + 1) * D_MODEL])

    def zcols(c):
        return slice(c * D_MODEL, (c + 1) * D_MODEL)

    def head(h):
        return slice(h * HEAD_W, (h + 1) * HEAD_W)

    def head_rows(h):
        return pl.ds(h, tm, stride=HEADS)

    def rope(zh):
        lane = lax.broadcasted_iota(jnp.int32, zh.shape, 1)
        first = (lane % ATTN_HEAD_DIM) < (ATTN_HEAD_DIM // 2)
        partner = jnp.where(first, pltpu.roll(zh, HEAD_W - ATTN_HEAD_DIM // 2, 1),
                            pltpu.roll(zh, ATTN_HEAD_DIM // 2, 1))
        return zh * cos_ref[...] + partner * sin_ref[...]

    zq = col(COL_Q)
    for h in range(HEADS):
        z_ref[:, head(h)] = rope(zq[:, head(h)]) * Q_SCALE

    zk = col(COL_K)
    for h in range(HEADS):
        kh = rope(zk[:, head(h)])
        kout_ref[head_rows(h), :] = kh
        kb_ref[:, head(h)] = kh.astype(BF16)

    zv = col(COL_V)
    for h in range(HEADS):
        vout_ref[head_rows(h), :] = zv[:, head(h)]
    vb_ref[...] = zv.astype(BF16)

    lbl = lbl_ref[...]
    e = jnp.exp(lbl - jnp.max(lbl, axis=0, keepdims=True))
    lb = e[0:1] / jnp.sum(e, axis=0, keepdims=True)
    z_ref[:, zcols(ZC_F)] = lb + (1.0 - lb) * jax.nn.sigmoid(col(COL_F))

    for j in range(COL_F + 1, N_COL_BLOCKS):
        z_ref[:, zcols(j - COL_V)] = col(j)


def _project(x, g, w_bf16, cos_t, sin_t, lb_logits):
    t = x.shape[0]
    tm = PROJ_TM
    n_tab = cos_t.shape[0] // tm
    tok = lambda i: (i, 0)
    fixed = lambda i: (0, 0)
    return pl.pallas_call(
        _proj_kernel,
        grid=(t // tm,),
        in_specs=[
            pl.BlockSpec((tm, D_MODEL), tok),
            pl.BlockSpec((1, D_MODEL), fixed),
            pl.BlockSpec((D_MODEL, N_COL_BLOCKS * D_MODEL), fixed, pipeline_mode=pl.Buffered(1)),
            pl.BlockSpec((tm, HEAD_W), lambda i: (i % n_tab, 0)),
            pl.BlockSpec((tm, HEAD_W), lambda i: (i % n_tab, 0)),
            pl.BlockSpec((2, D_MODEL), fixed),
        ],
        out_specs=[
            pl.BlockSpec((tm, N_Z_BLOCKS * D_MODEL), tok),
            pl.BlockSpec((tm * HEADS, HEAD_W), tok),
            pl.BlockSpec((tm * HEADS, HEAD_W), tok),
            pl.BlockSpec((tm, D_MODEL), tok),
            pl.BlockSpec((tm, D_MODEL), tok),
        ],
        out_shape=[
            jax.ShapeDtypeStruct((t, N_Z_BLOCKS * D_MODEL), F32),
            jax.ShapeDtypeStruct((t * HEADS, HEAD_W), F32),
            jax.ShapeDtypeStruct((t * HEADS, HEAD_W), F32),
            jax.ShapeDtypeStruct((t, D_MODEL), BF16),
            jax.ShapeDtypeStruct((t, D_MODEL), BF16),
        ],
        compiler_params=_cparams("arbitrary"),
        name="proj",
    )(x, g, w_bf16, cos_t, sin_t, lb_logits)


def _rope_tables(pos):
    half = ATTN_HEAD_DIM // 2
    inv = jnp.power(ROPE_THETA, -2.0 * jnp.arange(half, dtype=F32) / ATTN_HEAD_DIM)
    ang = pos.astype(F32)[:, None] * inv[None, :]
    cos = jnp.tile(jnp.cos(ang), (1, HEAD_W // half))
    sin = jnp.sin(ang)
    sin = jnp.tile(jnp.concatenate([-sin, sin], axis=-1), (1, HEAD_W // ATTN_HEAD_DIM))
    return cos, sin


def _split_components(q):
    lane = lax.broadcasted_iota(jnp.int32, q.shape, 1)
    q1 = jnp.where(lane < ATTN_HEAD_DIM, q, 0.0)
    q2 = jnp.where(lane >= ATTN_HEAD_DIM, q, 0.0)
    return jnp.concatenate([q1, q2], axis=0).astype(BF16)


def _attn_prompt_kernel(lam_ref, q_ref, k_ref, v_ref, o_ref):
    lam = lam_ref[0]
    tq = ATTN_TQ
    seq = k_ref.shape[0]
    k = k_ref[...]
    v = v_ref[...]
    row = lax.broadcasted_iota(jnp.int32, (2 * tq, tq), 0)
    col = lax.broadcasted_iota(jnp.int32, (2 * tq, tq), 1)
    visible = (col // CHUNK) <= ((row % tq) // CHUNK)
    n_tiles = seq // tq

    def scores(qi):
        lo = qi * tq
        qq = _split_components(q_ref[lo:lo + tq, :])
        s_d = jnp.where(visible, _dot_nt(qq, k[lo:lo + tq]), -jnp.inf)
        return s_d, (_dot_nt(qq, k[:lo]) if qi > 0 else None)

    def weighted_values(qi, w_d, w_m):
        lo = qi * tq
        o = _dot(w_d, v[lo:lo + tq])
        if qi > 0:
            o = o + _dot(w_m, v[:lo])
        o_ref[lo:lo + tq, :] = o

    ahead = scores(0)
    behind = None
    for qi in range(n_tiles):
        s_d, s_m = ahead
        if qi + 1 < n_tiles:
            ahead = scores(qi + 1)
        if behind is not None:
            weighted_values(*behind)
        m = jnp.max(s_d, axis=-1, keepdims=True)
        if qi > 0:
            m = jnp.maximum(m, jnp.max(s_m, axis=-1, keepdims=True))
            p_m = jnp.exp2(s_m - m)
        p_d = jnp.exp2(s_d - m)
        l = jnp.sum(p_d, axis=-1, keepdims=True)
        if qi > 0:
            l = l + jnp.sum(p_m, axis=-1, keepdims=True)
        r = 1.0 / l
        r1 = r[:tq]
        r2 = lam * r[tq:]
        w_d = (p_d[:tq] * r1 - p_d[tq:] * r2).astype(BF16)
        w_m = (p_m[:tq] * r1 - p_m[tq:] * r2).astype(BF16) if qi > 0 else None
        behind = (qi, w_d, w_m)
    weighted_values(*behind)


def _attn_prompt(lam, z, kb, vb, n_batch, seq):
    return pl.pallas_call(
        _attn_prompt_kernel,
        grid=(n_batch, HEADS),
        in_specs=[
            pl.BlockSpec(memory_space=pltpu.SMEM),
            pl.BlockSpec((seq, HEAD_W), lambda b, h: (b, ZC_Q * HEADS + h)),
            pl.BlockSpec((seq, HEAD_W), lambda b, h: (b, h)),
            pl.BlockSpec((seq, HEAD_W), lambda b, h: (b, h)),
        ],
        out_specs=pl.BlockSpec((seq, HEAD_W), lambda b, h: (b, h)),
        out_shape=jax.ShapeDtypeStruct((n_batch * seq, D_MODEL), F32),
        compiler_params=_cparams("arbitrary", "arbitrary"),
        name="attn_prompt",
    )(lam, z, kb, vb)


def _sample_attention(lam, q, kn, vn, ck_ref, cv_ref):
    n_q = q.shape[0]
    qq = _split_components(q)
    s_c = _dot_nt(qq, ck_ref[...].astype(BF16))
    s_n = _dot_nt(qq, kn)
    m = jnp.maximum(jnp.max(s_c, axis=-1, keepdims=True), jnp.max(s_n, axis=-1, keepdims=True))
    p_c = jnp.exp2(s_c - m)
    p_n = jnp.exp2(s_n - m)
    r = 1.0 / (jnp.sum(p_c, axis=-1, keepdims=True) + jnp.sum(p_n, axis=-1, keepdims=True))
    r1 = r[:n_q]
    r2 = lam * r[n_q:]
    w_c = (p_c[:n_q] * r1 - p_c[n_q:] * r2).astype(BF16)
    w_n = (p_n[:n_q] * r1 - p_n[n_q:] * r2).astype(BF16)
    return _dot(w_c, cv_ref[...].astype(BF16)) + _dot(w_n, vn)


def _split3(x):
    h1 = x.astype(BF16)
    r1 = x - h1.astype(F32)
    h2 = r1.astype(BF16)
    h3 = (r1 - h2.astype(F32)).astype(BF16)
    return h1, h2, h3


def _when_step(step, n_steps, which):
    def deco(body):
        if n_steps == 1:
            body()
        else:
            pl.when(step == which)(body)
    return deco


def _hgrn_kernel(n_chunks, side, *refs):
    if side is None:
        f_ref, q_ref, i_ref, s0_ref, o_ref, sout_ref, st_ref = refs
    else:
        (f_ref, q_ref, i_ref, s0_ref, lam_ref, qs_ref, kns_ref, vns_ref, ck_hbm, cv_hbm,
         o_ref, sout_ref, os_ref, st_ref, kbuf, vbuf, sem) = refs
    c = pl.program_id(1)
    n_group = f_ref.shape[0]

    @_when_step(c, n_chunks, 0)
    def _():
        for g in range(n_group):
            for h in range(HEADS):
                st_ref[g, h] = s0_ref[g, h].T

    if side is not None:
        pairs_per_step, n_pairs = side
        step = pl.program_id(0) * n_chunks + c

        def cache_copies(pair, sl):
            bb, hh = pair // HEADS, pair % HEADS
            part = kbuf.shape[1] // CACHE_COPY_PARTS
            copies = []
            for j in range(CACHE_COPY_PARTS):
                keys = pl.ds(j * part, part)
                copies.append(pltpu.make_async_copy(ck_hbm.at[bb, keys, hh, :], kbuf.at[sl, keys], sem.at[0, sl]))
                copies.append(pltpu.make_async_copy(cv_hbm.at[bb, keys, hh, :], vbuf.at[sl, keys], sem.at[1, sl]))
            return copies

        n_slots = kbuf.shape[0]
        ahead = n_slots - 1

        @pl.when(step == 0)
        def _():
            for p in range(ahead):
                for cp in cache_copies(p, p):
                    cp.start()

        def sample_pair(j):
            pair = step * pairs_per_step + j
            sl = pair % n_slots

            @pl.when(pair + ahead < n_pairs)
            def _():
                for cp in cache_copies(pair + ahead, (pair + ahead) % n_slots):
                    cp.start()

            for cp in cache_copies(pair, sl):
                cp.wait()
            lanes = slice(j * HEAD_W, (j + 1) * HEAD_W)
            os_ref[:, lanes] = _sample_attention(lam_ref[0], qs_ref[:, lanes], kns_ref[:, lanes], vns_ref[:, lanes],
                                                 kbuf.at[sl], vbuf.at[sl])
    else:
        pairs_per_step = 0

        def sample_pair(j):
            del j

    def sample_pairs(quarter):
        for j in range(quarter * pairs_per_step // 4, (quarter + 1) * pairs_per_step // 4):
            sample_pair(j)

    row = lax.broadcasted_iota(jnp.int32, (CHUNK, CHUNK), 0)
    col = lax.broadcasted_iota(jnp.int32, (CHUNK, CHUNK), 1)
    causal = row >= col
    tril = jnp.where(causal, 1.0, 0.0).astype(BF16)
    heads = [slice(h * HEAD_W, (h + 1) * HEAD_W) for h in range(HEADS)]
    groups = range(n_group)
    sample_pairs(0)
    fs = [f_ref[g] for g in groups]
    splits = [_split3(jnp.log(f)) for f in fs]
    bs = [_dot(tril, h1) + _dot(tril, h2) + _dot(tril, h3) for h1, h2, h3 in splits]
    scaled = []
    for g in groups:
        b = bs[g]
        b_last = b[CHUNK - 1:CHUNK, :]
        rk = 1.0 - fs[g]
        v = i_ref[g]
        scaled.append(((q_ref[g] * jnp.exp(b)).astype(BF16), (rk * jnp.exp(-b)).astype(BF16),
                       (rk * jnp.exp(b_last - b)).astype(BF16), v, v.astype(BF16), jnp.exp(b_last)))
    sample_pairs(1)
    first = []
    for g in groups:
        qd_b, kd_b, kl_b, v, v_b, eb_last = scaled[g]
        states = [st_ref[g, h] for h in range(HEADS)]
        a_all = [_dot_nt(qd_b[:, sl], kd_b[:, sl]) for sl in heads]
        o_state = [_dot_nt(qd_b[:, sl], st.astype(BF16)) for sl, st in zip(heads, states)]
        s_upd = [_dot(v[:, sl].T.astype(BF16), kl_b[:, sl]) for sl in heads]
        first.append((states, a_all, o_state, s_upd))
    sample_pairs(2)
    for g in groups:
        v_b, eb_last = scaled[g][4], scaled[g][5]
        states, a_all, o_state, s_upd = first[g]
        for h, sl in enumerate(heads):
            a = jnp.where(causal, a_all[h], 0.0).astype(BF16)
            o_ref[g, :, sl] = o_state[h] + _dot(a, v_b[:, sl])
            st_ref[g, h] = states[h] * eb_last[:, sl] + s_upd[h]
    sample_pairs(3)

    @_when_step(c, n_chunks, n_chunks - 1)
    def _():
        for g in range(n_group):
            for h in range(HEADS):
                sout_ref[g, h] = st_ref[g, h].T


def _hgrn(z, s0, n_batch, n_chunks, n_group, sample=None):
    z3 = z.reshape(n_batch, n_chunks * CHUNK, N_Z_BLOCKS * D_MODEL)
    n_steps = (n_batch // n_group) * n_chunks

    def zblock(zc):
        return pl.BlockSpec((n_group, CHUNK, D_MODEL), lambda b, c: (b, c, zc))

    state = pl.BlockSpec((n_group, HEADS, HEAD_W, HEAD_W), lambda b, c: (b, 0, 0, 0))
    in_specs = [zblock(ZC_F), zblock(ZC_RQ), zblock(ZC_RI), state]
    operands = [z3, z3, z3, s0]
    out_specs = [pl.BlockSpec((n_group, CHUNK, D_MODEL), lambda b, c: (b, c, 0)), state]
    out_shape = [
        jax.ShapeDtypeStruct((n_batch, n_chunks * CHUNK, D_MODEL), F32),
        jax.ShapeDtypeStruct((n_batch, HEADS, HEAD_W, HEAD_W), F32),
    ]
    scratch = [pltpu.VMEM((n_group, HEADS, HEAD_W, HEAD_W), F32)]
    side = None
    if sample is not None:
        lam, zs, kb_s, vb_s, cache_k, cache_v, n_sb, n_q = sample
        n_pairs = n_sb * HEADS
        pps = n_pairs // n_steps
        assert pps * n_steps == n_pairs and pps % 4 == 0 and HEADS % pps == 0
        col_groups = HEADS // pps
        past = cache_k.shape[1]
        side = (pps, n_pairs)

        def srow(b, c):
            return (b * n_chunks + c) // col_groups

        def scol(b, c):
            return (b * n_chunks + c) % col_groups

        in_specs += [
            pl.BlockSpec(memory_space=pltpu.SMEM),
            pl.BlockSpec((n_q, pps * HEAD_W), lambda b, c: (srow(b, c), ZC_Q * col_groups + scol(b, c))),
            pl.BlockSpec((n_q, pps * HEAD_W), lambda b, c: (srow(b, c), scol(b, c))),
            pl.BlockSpec((n_q, pps * HEAD_W), lambda b, c: (srow(b, c), scol(b, c))),
            pl.BlockSpec(memory_space=pl.ANY), pl.BlockSpec(memory_space=pl.ANY),
        ]
        operands += [lam, zs, kb_s, vb_s, cache_k, cache_v]
        out_specs.append(pl.BlockSpec((n_q, pps * HEAD_W), lambda b, c: (srow(b, c), scol(b, c))))
        out_shape.append(jax.ShapeDtypeStruct((n_sb * n_q, D_MODEL), F32))
        scratch += [pltpu.VMEM((SAMPLE_CACHE_SLOTS, past, HEAD_W), F32),
                    pltpu.VMEM((SAMPLE_CACHE_SLOTS, past, HEAD_W), F32),
                    pltpu.SemaphoreType.DMA((2, SAMPLE_CACHE_SLOTS))]
    outs = pl.pallas_call(
        functools.partial(_hgrn_kernel, n_chunks, side),
        grid=(n_batch // n_group, n_chunks),
        in_specs=in_specs,
        out_specs=out_specs,
        out_shape=out_shape,
        scratch_shapes=scratch,
        compiler_params=_cparams("arbitrary", "arbitrary"),
        name="hgrn",
    )(*operands)
    return (outs[0].reshape(n_batch * n_chunks * CHUNK, D_MODEL),) + tuple(outs[1:])


def _head_rmsnorm(x, g):
    outs = []
    for h in range(HEADS):
        xh = x[:, h * HEAD_W:(h + 1) * HEAD_W]
        ms = jnp.mean(xh * xh, axis=-1, keepdims=True)
        outs.append(xh * lax.rsqrt(ms + NORM_EPS) * g)
    return jnp.concatenate(outs, axis=-1)


def _merge_kernel(lam_init, ao_ref, ro_ref, rg_ref, ga_ref, gr_ref, x_ref, gs_ref, grn_ref,
                  woa_ref, wor_ref, wout_ref, gffn_ref, wr_ref, br_ref, cnt_in_ref,
                  x1_ref, xn_ref, idx_ref, prob_ref, rank_ref, cnt_ref, run_ref):
    i = pl.program_id(0)
    tm = x_ref.shape[0]

    @pl.when(i == 0)
    def _():
        run_ref[...] = cnt_in_ref[...]

    part = tm // MERGE_PARTS

    def rows(p):
        return slice(p * part, (p + 1) * part)

    def norm_inputs(p, _):
        an = (_head_rmsnorm(ao_ref[rows(p), :], gs_ref[...]) * (1.0 - lam_init)).astype(BF16)
        rn = (_head_rmsnorm(ro_ref[rows(p), :], grn_ref[...]) * jax.nn.silu(rg_ref[rows(p), :])).astype(BF16)
        return an, rn

    def branch_projections(p, st):
        an, rn = st
        return _dot(an, woa_ref[...]), _dot(rn, wor_ref[...])

    def gate(p, st):
        a, r = st
        return (jax.nn.sigmoid(ga_ref[rows(p), :]) * a + jax.nn.sigmoid(gr_ref[rows(p), :]) * r).astype(BF16)

    def residual(p, mixed):
        x1 = x_ref[rows(p), :] + _dot(mixed, wout_ref[...])
        x1_ref[rows(p), :] = x1
        return x1

    def ffn_norm_router(p, x1):
        ms = jnp.mean(x1 * x1, axis=-1, keepdims=True)
        xn = x1 * lax.rsqrt(ms + NORM_EPS) * gffn_ref[...]
        for s in range(ROW_SUB):
            xn_ref[pl.ds(p * part * ROW_SUB + s, part, stride=ROW_SUB), :] = xn[:, s * HEAD_W:(s + 1) * HEAD_W]
        return _dot(xn.astype(BF16), wr_ref[...])

    stages = (norm_inputs, branch_projections, gate, residual, ffn_norm_router)
    state = [None] * MERGE_PARTS
    for stage in stages:
        for p in range(MERGE_PARTS):
            state[p] = stage(p, state[p])
    logits = jnp.concatenate(state, axis=0) + br_ref[...]
    lane = lax.broadcasted_iota(jnp.int32, logits.shape, 1).astype(F32)
    sel = jnp.zeros(logits.shape, F32)
    work = logits
    tops, idxs = [], []
    for _ in range(TOP_K):
        m = jnp.max(work, axis=-1, keepdims=True)
        idx = jnp.min(jnp.where(work == m, lane, float(N_EXPERTS)), axis=-1, keepdims=True)
        hit = lane == idx
        sel = jnp.where(hit, 1.0, sel)
        work = jnp.where(hit, -jnp.inf, work)
        tops.append(m)
        idxs.append(idx)
    es = [jnp.exp(v - tops[0]) for v in tops]
    inv = 1.0 / (es[0] + es[1] + es[2] + es[3])

    trow = lax.broadcasted_iota(jnp.int32, (tm, tm), 0)
    tcol = lax.broadcasted_iota(jnp.int32, (tm, tm), 1)
    before = jnp.where(trow > tcol, 1.0, 0.0).astype(BF16)
    ranks = _dot(before, sel.astype(BF16)) + run_ref[...]
    k_lane = lax.broadcasted_iota(jnp.int32, (tm, TOP_K), 1)
    idx_out = jnp.zeros((tm, TOP_K), F32)
    prob_out = jnp.zeros((tm, TOP_K), F32)
    rank_out = jnp.zeros((tm, TOP_K), F32)
    for k in range(TOP_K):
        rk = jnp.sum(jnp.where(lane == idxs[k], ranks, 0.0), axis=-1, keepdims=True)
        idx_out = jnp.where(k_lane == k, idxs[k], idx_out)
        prob_out = jnp.where(k_lane == k, es[k] * inv, prob_out)
        rank_out = jnp.where(k_lane == k, rk, rank_out)
    idx_ref[...] = idx_out.astype(jnp.int32)
    prob_ref[...] = prob_out
    rank_ref[...] = rank_out.astype(jnp.int32)
    run = run_ref[...] + jnp.sum(sel, axis=0, keepdims=True)
    run_ref[...] = run
    cnt_ref[...] = run


def _merge(lam_init, ao, ro, z, x, g_subln, g_rec, woa, wor, wout, g_ffn, w_router, b_router, cnt_in):
    t = x.shape[0]
    tm = MERGE_TM
    row = lambda i: (i, 0)
    fixed = lambda i: (0, 0)
    tok = pl.BlockSpec((tm, D_MODEL), row)
    wspec = pl.BlockSpec((D_MODEL, D_MODEL), fixed)
    narrow = pl.BlockSpec((tm, TOP_K), row)
    return pl.pallas_call(
        functools.partial(_merge_kernel, lam_init),
        grid=(t // tm,),
        in_specs=[
            tok, tok,
            pl.BlockSpec((tm, D_MODEL), lambda i: (i, ZC_RG)),
            pl.BlockSpec((tm, D_MODEL), lambda i: (i, ZC_GA)),
            pl.BlockSpec((tm, D_MODEL), lambda i: (i, ZC_GR)),
            tok,
            pl.BlockSpec((1, HEAD_W), fixed), pl.BlockSpec((1, HEAD_W), fixed),
            wspec, wspec, wspec,
            pl.BlockSpec((1, D_MODEL), fixed),
            pl.BlockSpec((D_MODEL, N_EXPERTS), fixed),
            pl.BlockSpec((1, N_EXPERTS), fixed),
            pl.BlockSpec((1, N_EXPERTS), fixed),
        ],
        out_specs=[tok, pl.BlockSpec((tm * ROW_SUB, HEAD_W), row), narrow, narrow, narrow,
                   pl.BlockSpec((1, N_EXPERTS), fixed)],
        out_shape=[
            jax.ShapeDtypeStruct((t, D_MODEL), F32),
            jax.ShapeDtypeStruct((t * ROW_SUB, HEAD_W), F32),
            jax.ShapeDtypeStruct((t, TOP_K), jnp.int32),
            jax.ShapeDtypeStruct((t, TOP_K), F32),
            jax.ShapeDtypeStruct((t, TOP_K), jnp.int32),
            jax.ShapeDtypeStruct((1, N_EXPERTS), F32),
        ],
        scratch_shapes=[pltpu.VMEM((1, N_EXPERTS), F32)],
        compiler_params=_cparams("arbitrary"),
        name="merge",
    )(ao, ro, z, z, z, x, g_subln, g_rec, woa, wor, wout, g_ffn, w_router, b_router, cnt_in)


def _row_copy(src, dst, sem):
    return pltpu.make_async_copy(src, dst, sem)


def _dispatch_kernel(n_first, n_steps, pos_ref, xa_hbm, xb_hbm, xs_hbm, stage, load_sem, copy_sem):
    i = pl.program_id(0)
    tm = ROW_TM
    rows = tm * ROW_SUB
    slot = i % DISPATCH_SLOTS

    def row_tile(off):
        return pl.ds(pl.multiple_of(off, ROW_SUB), ROW_SUB)

    def load(t, sl):
        @pl.when(t < n_first)
        def _():
            _row_copy(xa_hbm.at[pl.ds(pl.multiple_of(t * rows, rows), rows)], stage.at[sl], load_sem.at[sl]).start()

        @pl.when(t >= n_first)
        def _():
            _row_copy(xb_hbm.at[pl.ds(pl.multiple_of((t - n_first) * rows, rows), rows)], stage.at[sl],
                      load_sem.at[sl]).start()

    def drain(parity):
        for k in range(TOP_K):
            _row_copy(stage.at[0], xs_hbm.at[pl.ds(0, rows)], copy_sem.at[parity]).wait()

    @pl.when(i == 0)
    def _():
        load(i, slot)

    @pl.when(i + 1 < n_steps)
    def _():
        load(i + 1, (i + 1) % DISPATCH_SLOTS)

    _row_copy(xa_hbm.at[pl.ds(0, rows)], stage.at[slot], load_sem.at[slot]).wait()

    def start(g, carry):
        r0 = g * ROW_COPY_UNROLL
        n = ROW_COPY_UNROLL * TOP_K
        dst = [pos_ref[0, 0, r0 * TOP_K + j] for j in range(n)]
        for j in range(n):
            _row_copy(stage.at[slot, row_tile((r0 + j // TOP_K) * ROW_SUB)],
                      xs_hbm.at[row_tile(dst[j])], copy_sem.at[i % 2]).start()
        return carry

    lax.fori_loop(0, tm // ROW_COPY_UNROLL, start, 0)

    @pl.when(i > 0)
    def _():
        drain((i - 1) % 2)

    @pl.when(i == n_steps - 1)
    def _():
        drain(i % 2)


def _dispatch(pos, xn_a, xn_b):
    tm = ROW_TM
    n_a, n_b = xn_a.shape[0] // (tm * ROW_SUB), xn_b.shape[0] // (tm * ROW_SUB)
    pos3 = pos.reshape(n_a + n_b, 1, tm * TOP_K)
    return pl.pallas_call(
        functools.partial(_dispatch_kernel, n_a, n_a + n_b),
        grid=(n_a + n_b,),
        in_specs=[
            pl.BlockSpec((1, 1, tm * TOP_K), lambda i: (i, 0, 0), memory_space=pltpu.SMEM),
            pl.BlockSpec(memory_space=pl.ANY),
            pl.BlockSpec(memory_space=pl.ANY),
        ],
        out_specs=pl.BlockSpec(memory_space=pl.ANY),
        out_shape=jax.ShapeDtypeStruct((pos.size * ROW_SUB, HEAD_W), F32),
        scratch_shapes=[
            pltpu.VMEM((DISPATCH_SLOTS, tm * ROW_SUB, HEAD_W), F32),
            pltpu.SemaphoreType.DMA((DISPATCH_SLOTS,)),
            pltpu.SemaphoreType.DMA((2,)),
        ],
        compiler_params=_cparams("arbitrary"),
        name="dispatch",
    )(pos3, xn_a, xn_b)


def _experts_kernel(tile_ref, exp_ref, lo_ref, hi_ref, fresh_ref, par_ref, nxt_ref, more_ref,
                    xs_ref, wg_hbm, bg_ref, wu_hbm, bu_ref, wd_hbm, bd_ref, o_ref,
                    wg_s, wu_s, wd_s, x_s, acc_s, wf_s, wsem):
    w = pl.program_id(0)
    lo = lo_ref[w]
    hi = hi_ref[w]
    tm = x_s.shape[0]

    def lanes(s):
        return slice(s * HEAD_W, (s + 1) * HEAD_W)

    def sub_rows(s):
        return pl.ds(s, tm, stride=ROW_SUB)

    def weight_copies(e, sl):
        return [pltpu.make_async_copy(src.at[e], wf_s.at[sl, j], wsem.at[sl])
                for j, src in enumerate((wg_hbm, wu_hbm, wd_hbm))]

    @pl.when(w == 0)
    def _():
        for c in weight_copies(exp_ref[0], par_ref[0]):
            c.start()
        acc_s[...] = jnp.zeros(acc_s.shape, F32)

    @pl.when(fresh_ref[w] == 1)
    def _():
        sl = par_ref[w]
        for c in weight_copies(exp_ref[w], sl):
            c.wait()

        @pl.when(more_ref[w] == 1)
        def _():
            for c in weight_copies(nxt_ref[w], 1 - sl):
                c.start()

        wg_s[...] = wf_s[sl, 0].astype(BF16)
        wu_s[...] = wf_s[sl, 1].astype(BF16)
        wd_s[...] = wf_s[sl, 2].astype(BF16)

    @pl.when(hi > lo)
    def _():
        for s in range(ROW_SUB):
            x_s[:, lanes(s)] = xs_ref[sub_rows(s), :].astype(BF16)
        x = x_s[...]
        out = bd_ref[0]
        def gate_up(c):
            ff = slice(c, c + MOE_FF_SLICE)
            return _dot(x, wg_s[:, ff]) + bg_ref[0, :, ff], _dot(x, wu_s[:, ff]) + bu_ref[0, :, ff]

        slices = range(0, D_MODEL, MOE_FF_SLICE)
        gus = [gate_up(c) for c in slices]
        acts = []
        for gate, up in gus:
            gate = jnp.minimum(gate, SWIGLU_LIMIT)
            up = jnp.clip(up, -SWIGLU_LIMIT, SWIGLU_LIMIT)
            acts.append((gate * jax.nn.sigmoid(SWIGLU_ALPHA * gate) * (up + 1.0)).astype(BF16))
        for c, act in zip(slices, acts):
            out = out + _dot(act, wd_s[c:c + MOE_FF_SLICE, :])
        row = lax.broadcasted_iota(jnp.int32, (tm, 1), 0)
        merged = jnp.where((row >= lo) & (row < hi), out, acc_s[...])
        acc_s[...] = merged
        for s in range(ROW_SUB):
            o_ref[sub_rows(s), :] = merged[:, lanes(s)]


def _experts(plan, xs, w_gate, b_gate, w_up, b_up, w_down, b_down):
    tm = MOE_TM
    n_items = plan[0].shape[0]
    rows = lambda w, tile, *_: (tile[w], 0)
    wspec = pl.BlockSpec(memory_space=pl.ANY)
    bspec = pl.BlockSpec((1, 1, D_MODEL), lambda w, tile, exp, *_: (exp[w], 0, 0))
    grid_spec = pltpu.PrefetchScalarGridSpec(
        num_scalar_prefetch=len(plan),
        grid=(n_items,),
        in_specs=[pl.BlockSpec((tm * ROW_SUB, HEAD_W), rows), wspec, bspec, wspec, bspec, wspec, bspec],
        out_specs=pl.BlockSpec((tm * ROW_SUB, HEAD_W), rows),
        scratch_shapes=[pltpu.VMEM((D_MODEL, D_MODEL), BF16)] * 3 + [
            pltpu.VMEM((tm, D_MODEL), BF16),
            pltpu.VMEM((tm, D_MODEL), F32),
            pltpu.VMEM((2, 3, D_MODEL, D_MODEL), F32),
            pltpu.SemaphoreType.DMA((2,)),
        ],
    )
    return pl.pallas_call(
        _experts_kernel,
        grid_spec=grid_spec,
        out_shape=jax.ShapeDtypeStruct(xs.shape, F32),
        compiler_params=_cparams("arbitrary"),
        name="experts",
    )(*plan, xs, w_gate, b_gate.reshape(N_EXPERTS, 1, D_MODEL), w_up, b_up.reshape(N_EXPERTS, 1, D_MODEL),
      w_down, b_down.reshape(N_EXPERTS, 1, D_MODEL))


def _expert_plan(counts, n_rows):
    tm = MOE_TM
    n_tiles = n_rows // tm
    n_items = n_tiles + N_EXPERTS - 1
    ends = jnp.cumsum(counts)
    starts = ends - counts
    first_tile = starts // tm
    last_tile = jnp.maximum(ends - 1, 0) // tm
    items = jnp.where(counts > 0, last_tile - first_tile + 1, 0)
    item_end = jnp.cumsum(items)
    item_start = item_end - items
    total = item_end[-1]
    w = jnp.arange(n_items, dtype=jnp.int32)
    wc = jnp.minimum(w, total - 1)
    exp = jnp.sum((item_end[None, :] <= wc[:, None]).astype(jnp.int32), axis=1)
    ids = jnp.arange(N_EXPERTS, dtype=jnp.int32)
    mine = exp[:, None] == ids[None, :]

    def of_item(per_expert):
        return jnp.sum(jnp.where(mine, per_expert[None, :], 0), axis=1).astype(jnp.int32)

    tile = of_item(first_tile) + wc - of_item(item_start)
    lo = jnp.maximum(of_item(starts), tile * tm) - tile * tm
    hi = jnp.minimum(of_item(ends), (tile + 1) * tm) - tile * tm
    valid = w < total
    lo = jnp.where(valid, lo, 0).astype(jnp.int32)
    hi = jnp.where(valid, hi, 0).astype(jnp.int32)
    prev_exp = jnp.concatenate([jnp.full((1,), -1, jnp.int32), exp[:-1]])
    fresh = (valid & (exp != prev_exp)).astype(jnp.int32)
    later = jnp.where((counts > 0)[None, :] & (ids[None, :] > ids[:, None]), ids[None, :], N_EXPERTS)
    next_exp = jnp.min(later, axis=1)
    parity = ((jnp.cumsum((counts > 0).astype(jnp.int32)) - 1) % 2).astype(jnp.int32)
    next_of_item = of_item(next_exp)
    more = (next_of_item < N_EXPERTS).astype(jnp.int32)
    nxt = jnp.where(more == 1, next_of_item, exp).astype(jnp.int32)
    return tile, exp, lo, hi, fresh, of_item(parity), nxt, more


def _combine_kernel(n_steps, pos_ref, pos_next_ref, x1_ref, prob_ref, gfin_ref, ys_ref, y_ref, buf_ref, sem):
    i = pl.program_id(0)
    tm = x1_ref.shape[0]
    slot = i % 2

    def row_tile(off):
        return pl.ds(pl.multiple_of(off, ROW_SUB), ROW_SUB)

    def gather(p_ref, sl):
        def start(g, carry):
            r0 = g * ROW_COPY_UNROLL
            n = ROW_COPY_UNROLL * TOP_K
            src = [p_ref[0, 0, r0 * TOP_K + j] for j in range(n)]
            for j in range(n):
                _row_copy(ys_ref.at[row_tile(src[j])],
                          buf_ref.at[sl, j % TOP_K, row_tile((r0 + j // TOP_K) * ROW_SUB)], sem.at[sl]).start()
            return carry

        lax.fori_loop(0, tm // ROW_COPY_UNROLL, start, 0)

    @pl.when(i == 0)
    def _():
        gather(pos_ref, slot)

    @pl.when(i + 1 < n_steps)
    def _():
        gather(pos_next_ref, 1 - slot)

    for k in range(TOP_K):
        _row_copy(ys_ref.at[pl.ds(0, tm * ROW_SUB)], buf_ref.at[slot, k], sem.at[slot]).wait()

    prob = prob_ref[...]
    parts = []
    for s in range(ROW_SUB):
        part = x1_ref[:, s * HEAD_W:(s + 1) * HEAD_W]
        for k in range(TOP_K):
            part = part + prob[:, k:k + 1] * buf_ref[slot, k, pl.ds(s, tm, stride=ROW_SUB), :]
        parts.append(part)
    x2 = jnp.concatenate(parts, axis=-1)
    ms = jnp.mean(x2 * x2, axis=-1, keepdims=True)
    y_ref[...] = x2 * lax.rsqrt(ms + NORM_EPS) * gfin_ref[...]


def _combine(pos, x1, prob, g_final, ys):
    t = x1.shape[0]
    tm = ROW_TM
    n = t // tm
    pos3 = pos.reshape(n, 1, tm * TOP_K)
    return pl.pallas_call(
        functools.partial(_combine_kernel, n),
        grid=(n,),
        in_specs=[
            pl.BlockSpec((1, 1, tm * TOP_K), lambda i: (i, 0, 0), memory_space=pltpu.SMEM),
            pl.BlockSpec((1, 1, tm * TOP_K), lambda i: (jnp.minimum(i + 1, n - 1), 0, 0), memory_space=pltpu.SMEM),
            pl.BlockSpec((tm, D_MODEL), lambda i: (i, 0)),
            pl.BlockSpec((tm, TOP_K), lambda i: (i, 0)),
            pl.BlockSpec((1, D_MODEL), lambda i: (0, 0)),
            pl.BlockSpec(memory_space=pl.ANY),
        ],
        out_specs=pl.BlockSpec((tm, D_MODEL), lambda i: (i, 0)),
        out_shape=jax.ShapeDtypeStruct((t, D_MODEL), F32),
        scratch_shapes=[pltpu.VMEM((2, TOP_K, tm * ROW_SUB, HEAD_W), F32), pltpu.SemaphoreType.DMA((2,))],
        compiler_params=_cparams("arbitrary"),
        name="combine",
    )(pos3, pos3, x1, prob, g_final, ys)


def _lambda_init(layer):
    return 0.8 - 0.6 * math.exp(-0.3 * layer)


def kernel(x_prompt, x_sample, cache_k, cache_v, state_rec, g_mix, w_in, lambda_q1, lambda_k1, lambda_q2, lambda_k2, g_subln, w_o_attn, lb_logits, g_rec_norm, w_o_rec, w_out, g_ffn, w_router, b_router, w_gate, b_gate, w_up, b_up, w_down, b_down, g_final):
    nb, seq, _ = x_prompt.shape
    nbd, ld, _ = x_sample.shape
    past = cache_k.shape[2]
    tp, ts = nb * seq, nbd * ld
    lam_init = _lambda_init(0)
    lam = (jnp.exp(jnp.sum(lambda_q1[0].astype(F32) * lambda_k1[0].astype(F32)))
           - jnp.exp(jnp.sum(lambda_q2[0].astype(F32) * lambda_k2[0].astype(F32))) + lam_init).reshape(1)

    w_in_b = w_in[0].astype(BF16)
    woa, wor, wout = w_o_attn[0].astype(BF16), w_o_rec[0].astype(BF16), w_out[0].astype(BF16)
    wr = w_router[0].astype(BF16)
    g_mix2, g_ffn2, g_fin2 = g_mix[0].reshape(1, -1), g_ffn[0].reshape(1, -1), g_final.reshape(1, -1)
    gs2, grn2 = g_subln[0].reshape(1, -1), g_rec_norm[0].reshape(1, -1)
    br2 = b_router[0].reshape(1, -1)

    cos_p, sin_p = _rope_tables(jnp.arange(seq, dtype=jnp.int32))
    pos_s = past + (jnp.arange(PROJ_TM, dtype=jnp.int32) % ld)
    cos_s, sin_s = _rope_tables(pos_s)

    xp = x_prompt.reshape(tp, D_MODEL)
    xs_in = x_sample.reshape(ts, D_MODEL)
    zp, k_p, v_p, kb_p, vb_p = _project(xp, g_mix2, w_in_b, cos_p, sin_p, lb_logits)
    zs, k_s, v_s, kb_s, vb_s = _project(xs_in, g_mix2, w_in_b, cos_s, sin_s, lb_logits)

    ao_p = _attn_prompt(lam, zp, kb_p, vb_p, nb, seq)
    ro_p, st_p, ao_s = _hgrn(zp, jnp.zeros((nb, HEADS, HEAD_W, HEAD_W), F32), nb, seq // CHUNK, HGRN_GROUP_PROMPT,
                             sample=(lam, zs, kb_s, vb_s, cache_k[0], cache_v[0], nbd, ld))
    ro_s, st_s = _hgrn(zs, state_rec[0], nbd, ld // CHUNK, HGRN_GROUP_SAMPLE)

    merge_w = (gs2, grn2, woa, wor, wout, g_ffn2, wr, br2)
    zero_cnt = jnp.zeros((1, N_EXPERTS), F32)
    x1_p, xn_p, idx_p, prob_p, rank_p, cnt_p = _merge(lam_init, ao_p, ro_p, zp, xp, *merge_w, zero_cnt)
    x1_s, xn_s, idx_s, prob_s, rank_s, cnt = _merge(lam_init, ao_s, ro_s, zs, xs_in, *merge_w, cnt_p)

    counts = cnt[0].astype(jnp.int32)
    starts = jnp.cumsum(counts) - counts
    experts = jnp.arange(N_EXPERTS, dtype=jnp.int32)

    def positions(idx, rank):
        return (jnp.sum(jnp.where(idx[..., None] == experts, starts, 0), axis=-1) + rank) * ROW_SUB

    pos_p = positions(idx_p, rank_p)
    pos_s2 = positions(idx_s, rank_s)
    n_rows = (tp + ts) * TOP_K
    plan = _expert_plan(counts, n_rows)

    xs_sorted = _dispatch(jnp.concatenate([pos_p, pos_s2], axis=0), xn_p, xn_s)
    ys = _experts(plan, xs_sorted, w_gate[0], b_gate[0], w_up[0], b_up[0], w_down[0], b_down[0])
    y_p = _combine(pos_p, x1_p, prob_p, g_fin2, ys)
    y_s = _combine(pos_s2, x1_s, prob_s, g_fin2, ys)

    return (y_p.reshape(nb, seq, D_MODEL), y_s.reshape(nbd, ld, D_MODEL),
            k_p.reshape(1, nb, seq, HEADS, HEAD_W), v_p.reshape(1, nb, seq, HEADS, HEAD_W), st_p[None],
            k_s.reshape(1, nbd, ld, HEADS, HEAD_W), v_s.reshape(1, nbd, ld, HEADS, HEAD_W), st_s[None])
```

```python
import functools
import math

import jax
import jax.numpy as jnp
from jax import lax
from jax.experimental import pallas as pl
from jax.experimental.pallas import tpu as pltpu

F32 = jnp.float32
BF16 = jnp.bfloat16

D_MODEL = 1024
CHUNK = 64
HEADS = 8
HEAD_W = 128
ATTN_HEAD_DIM = 64
ROPE_THETA = 10000.0
Q_SCALE = ATTN_HEAD_DIM ** -0.5 * math.log2(math.e)
N_EXPERTS = 32
TOP_K = 4
SWIGLU_ALPHA = 1.702
SWIGLU_LIMIT = 7.0
NORM_EPS = 1e-6
N_COL_BLOCKS = 9
COL_Q, COL_K, COL_V, COL_F = range(4)
N_Z_BLOCKS = 7
ZC_Q, ZC_F, ZC_RQ, ZC_RI, ZC_RG, ZC_GA, ZC_GR = range(7)
PROJ_TM = 256

ATTN_TQ = 256
CACHE_COPY_PARTS = 4
SAMPLE_CACHE_SLOTS = 4
HGRN_GROUP_PROMPT = 4
HGRN_GROUP_SAMPLE = 4
MERGE_TM = 512
MERGE_PARTS = 2
MOE_TM = 256
MOE_FF_SLICE = 256
ROW_TM = 256
ROW_COPY_UNROLL = 4
DISPATCH_SLOTS = 3
ROW_SUB = D_MODEL // HEAD_W
VMEM_LIMIT = 56 * 1024 * 1024


def _cparams(*sem):
    return pltpu.CompilerParams(dimension_semantics=sem, vmem_limit_bytes=VMEM_LIMIT)


def _dot(a, b):
    return jnp.dot(a, b, preferred_element_type=F32)


def _dot_nt(a, b):
    return lax.dot_general(a, b, (((1,), (1,)), ((), ())), preferred_element_type=F32)


def _proj_kernel(x_ref, g_ref, w_ref, cos_ref, sin_ref, lbl_ref,
                 z_ref, kout_ref, vout_ref, kb_ref, vb_ref):
    tm = x_ref.shape[0]
    x = x_ref[...]
    ms = jnp.mean(x * x, axis=-1, keepdims=True)
    hn = (x * lax.rsqrt(ms + NORM_EPS) * g_ref[...]).astype(BF16)

    def col(j):
        return _dot(hn, w_ref[:, j * D_MODEL:(j + 1) * D_MODEL])

    def zcols(c):
        return slice(c * D_MODEL, (c + 1) * D_MODEL)

    def head(h):
        return slice(h * HEAD_W, (h + 1) * HEAD_W)

    def head_rows(h):
        return pl.ds(h, tm, stride=HEADS)

    def rope(zh):
        lane = lax.broadcasted_iota(jnp.int32, zh.shape, 1)
        first = (lane % ATTN_HEAD_DIM) < (ATTN_HEAD_DIM // 2)
        partner = jnp.where(first, pltpu.roll(zh, HEAD_W - ATTN_HEAD_DIM // 2, 1),
                            pltpu.roll(zh, ATTN_HEAD_DIM // 2, 1))
        return zh * cos_ref[...] + partner * sin_ref[...]

    zq = col(COL_Q)
    for h in range(HEADS):
        z_ref[:, head(h)] = rope(zq[:, head(h)]) * Q_SCALE

    zk = col(COL_K)
    for h in range(HEADS):
        kh = rope(zk[:, head(h)])
        kout_ref[head_rows(h), :] = kh
        kb_ref[:, head(h)] = kh.astype(BF16)

    zv = col(COL_V)
    for h in range(HEADS):
        vout_ref[head_rows(h), :] = zv[:, head(h)]
    vb_ref[...] = zv.astype(BF16)

    lbl = lbl_ref[...]
    e = jnp.exp(lbl - jnp.max(lbl, axis=0, keepdims=True))
    lb = e[0:1] / jnp.sum(e, axis=0, keepdims=True)
    z_ref[:, zcols(ZC_F)] = lb + (1.0 - lb) * jax.nn.sigmoid(col(COL_F))

    for j in range(COL_F + 1, N_COL_BLOCKS):
        z_ref[:, zcols(j - COL_V)] = col(j)


def _project(x, g, w_bf16, cos_t, sin_t, lb_logits):
    t = x.shape[0]
    tm = PROJ_TM
    n_tab = cos_t.shape[0] // tm
    tok = lambda i: (i, 0)
    fixed = lambda i: (0, 0)
    return pl.pallas_call(
        _proj_kernel,
        grid=(t // tm,),
        in_specs=[
            pl.BlockSpec((tm, D_MODEL), tok),
            pl.BlockSpec((1, D_MODEL), fixed),
            pl.BlockSpec((D_MODEL, N_COL_BLOCKS * D_MODEL), fixed, pipeline_mode=pl.Buffered(1)),
            pl.BlockSpec((tm, HEAD_W), lambda i: (i % n_tab, 0)),
            pl.BlockSpec((tm, HEAD_W), lambda i: (i % n_tab, 0)),
            pl.BlockSpec((2, D_MODEL), fixed),
        ],
        out_specs=[
            pl.BlockSpec((tm, N_Z_BLOCKS * D_MODEL), tok),
            pl.BlockSpec((tm * HEADS, HEAD_W), tok),
            pl.BlockSpec((tm * HEADS, HEAD_W), tok),
            pl.BlockSpec((tm, D_MODEL), tok),
            pl.BlockSpec((tm, D_MODEL), tok),
        ],
        out_shape=[
            jax.ShapeDtypeStruct((t, N_Z_BLOCKS * D_MODEL), F32),
            jax.ShapeDtypeStruct((t * HEADS, HEAD_W), F32),
            jax.ShapeDtypeStruct((t * HEADS, HEAD_W), F32),
            jax.ShapeDtypeStruct((t, D_MODEL), BF16),
            jax.ShapeDtypeStruct((t, D_MODEL), BF16),
        ],
        compiler_params=_cparams("arbitrary"),
        name="proj",
    )(x, g, w_bf16, cos_t, sin_t, lb_logits)


def _rope_tables(pos):
    half = ATTN_HEAD_DIM // 2
    inv = jnp.power(ROPE_THETA, -2.0 * jnp.arange(half, dtype=F32) / ATTN_HEAD_DIM)
    ang = pos.astype(F32)[:, None] * inv[None, :]
    cos = jnp.tile(jnp.cos(ang), (1, HEAD_W // half))
    sin = jnp.sin(ang)
    sin = jnp.tile(jnp.concatenate([-sin, sin], axis=-1), (1, HEAD_W // ATTN_HEAD_DIM))
    return cos, sin


def _split_components(q):
    lane = lax.broadcasted_iota(jnp.int32, q.shape, 1)
    q1 = jnp.where(lane < ATTN_HEAD_DIM, q, 0.0)
    q2 = jnp.where(lane >= ATTN_HEAD_DIM, q, 0.0)
    return jnp.concatenate([q1, q2], axis=0).astype(BF16)


def _attn_prompt_kernel(lam_ref, q_ref, k_ref, v_ref, o_ref):
    lam = lam_ref[0]
    tq = ATTN_TQ
    seq = k_ref.shape[0]
    k = k_ref[...]
    v = v_ref[...]
    row = lax.broadcasted_iota(jnp.int32, (2 * tq, tq), 0)
    col = lax.broadcasted_iota(jnp.int32, (2 * tq, tq), 1)
    visible = (col // CHUNK) <= ((row % tq) // CHUNK)
    n_tiles = seq // tq

    def scores(qi):
        lo = qi * tq
        qq = _split_components(q_ref[lo:lo + tq, :])
        s_d = jnp.where(visible, _dot_nt(qq, k[lo:lo + tq]), -jnp.inf)
        return s_d, (_dot_nt(qq, k[:lo]) if qi > 0 else None)

    def weighted_values(qi, w_d, w_m):
        lo = qi * tq
        o = _dot(w_d, v[lo:lo + tq])
        if qi > 0:
            o = o + _dot(w_m, v[:lo])
        o_ref[lo:lo + tq, :] = o

    ahead = scores(0)
    behind = None
    for qi in range(n_tiles):
        s_d, s_m = ahead
        if qi + 1 < n_tiles:
            ahead = scores(qi + 1)
        if behind is not None:
            weighted_values(*behind)
        m = jnp.max(s_d, axis=-1, keepdims=True)
        if qi > 0:
            m = jnp.maximum(m, jnp.max(s_m, axis=-1, keepdims=True))
            p_m = jnp.exp2(s_m - m)
        p_d = jnp.exp2(s_d - m)
        l = jnp.sum(p_d, axis=-1, keepdims=True)
        if qi > 0:
            l = l + jnp.sum(p_m, axis=-1, keepdims=True)
        r = 1.0 / l
        r1 = r[:tq]
        r2 = lam * r[tq:]
        w_d = (p_d[:tq] * r1 - p_d[tq:] * r2).astype(BF16)
        w_m = (p_m[:tq] * r1 - p_m[tq:] * r2).astype(BF16) if qi > 0 else None
        behind = (qi, w_d, w_m)
    weighted_values(*behind)


def _attn_prompt(lam, z, kb, vb, n_batch, seq):
    return pl.pallas_call(
        _attn_prompt_kernel,
        grid=(n_batch, HEADS),
        in_specs=[
            pl.BlockSpec(memory_space=pltpu.SMEM),
            pl.BlockSpec((seq, HEAD_W), lambda b, h: (b, ZC_Q * HEADS + h)),
            pl.BlockSpec((seq, HEAD_W), lambda b, h: (b, h)),
            pl.BlockSpec((seq, HEAD_W), lambda b, h: (b, h)),
        ],
        out_specs=pl.BlockSpec((seq, HEAD_W), lambda b, h: (b, h)),
        out_shape=jax.ShapeDtypeStruct((n_batch * seq, D_MODEL), F32),
        compiler_params=_cparams("arbitrary", "arbitrary"),
        name="attn_prompt",
    )(lam, z, kb, vb)


def _sample_attention(lam, qs, kns, vns, ck_refs, cv_refs):
    n_q = qs[0].shape[0]
    qqs = [_split_components(q) for q in qs]
    scores = [(_dot_nt(qq, ck[...].astype(BF16)), _dot_nt(qq, kn)) for qq, ck, kn in zip(qqs, ck_refs, kns)]
    weights = []
    for s_c, s_n in scores:
        m = jnp.maximum(jnp.max(s_c, axis=-1, keepdims=True), jnp.max(s_n, axis=-1, keepdims=True))
        p_c = jnp.exp2(s_c - m)
        p_n = jnp.exp2(s_n - m)
        r = 1.0 / (jnp.sum(p_c, axis=-1, keepdims=True) + jnp.sum(p_n, axis=-1, keepdims=True))
        r1 = r[:n_q]
        r2 = lam * r[n_q:]
        weights.append(((p_c[:n_q] * r1 - p_c[n_q:] * r2).astype(BF16), (p_n[:n_q] * r1 - p_n[n_q:] * r2).astype(BF16)))
    return [_dot(w_c, cv[...].astype(BF16)) + _dot(w_n, vn) for (w_c, w_n), cv, vn in zip(weights, cv_refs, vns)]


def _split3(x):
    h1 = x.astype(BF16)
    r1 = x - h1.astype(F32)
    h2 = r1.astype(BF16)
    h3 = (r1 - h2.astype(F32)).astype(BF16)
    return h1, h2, h3


def _when_step(step, n_steps, which):
    def deco(body):
        if n_steps == 1:
            body()
        else:
            pl.when(step == which)(body)
    return deco


def _hgrn_kernel(n_chunks, side, *refs):
    if side is None:
        f_ref, q_ref, i_ref, s0_ref, o_ref, sout_ref, st_ref = refs
    else:
        (f_ref, q_ref, i_ref, s0_ref, lam_ref, qs_ref, kns_ref, vns_ref, ck_hbm, cv_hbm,
         o_ref, sout_ref, os_ref, st_ref, kbuf, vbuf, sem) = refs
    c = pl.program_id(1)
    n_group = f_ref.shape[0]

    @_when_step(c, n_chunks, 0)
    def _():
        for g in range(n_group):
            for h in range(HEADS):
                st_ref[g, h] = s0_ref[g, h].T

    if side is not None:
        pairs_per_step, n_pairs = side
        step = pl.program_id(0) * n_chunks + c

        def cache_copies(pair, sl):
            bb, hh = pair // HEADS, pair % HEADS
            part = kbuf.shape[1] // CACHE_COPY_PARTS
            copies = []
            for j in range(CACHE_COPY_PARTS):
                keys = pl.ds(j * part, part)
                copies.append(pltpu.make_async_copy(ck_hbm.at[bb, keys, hh, :], kbuf.at[sl, keys], sem.at[0, sl]))
                copies.append(pltpu.make_async_copy(cv_hbm.at[bb, keys, hh, :], vbuf.at[sl, keys], sem.at[1, sl]))
            return copies

        n_slots = kbuf.shape[0]
        duo = n_slots // 2

        @pl.when(step == 0)
        def _():
            for p in range(duo):
                for cp in cache_copies(p, p):
                    cp.start()

        def sample_duo(d):
            first = step * pairs_per_step + d * duo
            js = [d * duo + t for t in range(duo)]
            slots = [j % n_slots for j in js]

            @pl.when(first + duo < n_pairs)
            def _():
                for t in range(duo):
                    for cp in cache_copies(first + duo + t, (js[t] + duo) % n_slots):
                        cp.start()

            for t in range(duo):
                for cp in cache_copies(first + t, slots[t]):
                    cp.wait()
            lanes = [slice(j * HEAD_W, (j + 1) * HEAD_W) for j in js]
            outs = _sample_attention(lam_ref[0], [qs_ref[:, sl] for sl in lanes], [kns_ref[:, sl] for sl in lanes],
                                     [vns_ref[:, sl] for sl in lanes], [kbuf.at[sl] for sl in slots],
                                     [vbuf.at[sl] for sl in slots])
            for sl, o in zip(lanes, outs):
                os_ref[:, sl] = o

        duos_per_step = pairs_per_step // duo
    else:
        duos_per_step = 0

        def sample_duo(d):
            del d

    def sample_duos(half):
        for d in range(half * duos_per_step // 2, (half + 1) * duos_per_step // 2):
            sample_duo(d)

    row = lax.broadcasted_iota(jnp.int32, (CHUNK, CHUNK), 0)
    col = lax.broadcasted_iota(jnp.int32, (CHUNK, CHUNK), 1)
    causal = row >= col
    tril = jnp.where(causal, 1.0, 0.0).astype(BF16)
    heads = [slice(h * HEAD_W, (h + 1) * HEAD_W) for h in range(HEADS)]
    groups = range(n_group)
    sample_duos(0)
    fs = [f_ref[g] for g in groups]
    splits = [_split3(jnp.log(f)) for f in fs]
    bs = [_dot(tril, h1) + _dot(tril, h2) + _dot(tril, h3) for h1, h2, h3 in splits]
    scaled = []
    for g in groups:
        b = bs[g]
        b_last = b[CHUNK - 1:CHUNK, :]
        rk = 1.0 - fs[g]
        v = i_ref[g]
        scaled.append(((q_ref[g] * jnp.exp(b)).astype(BF16), (rk * jnp.exp(-b)).astype(BF16),
                       (rk * jnp.exp(b_last - b)).astype(BF16), v, v.astype(BF16), jnp.exp(b_last)))
    first = []
    for g in groups:
        qd_b, kd_b, kl_b, v, v_b, eb_last = scaled[g]
        states = [st_ref[g, h] for h in range(HEADS)]
        a_all = [_dot_nt(qd_b[:, sl], kd_b[:, sl]) for sl in heads]
        o_state = [_dot_nt(qd_b[:, sl], st.astype(BF16)) for sl, st in zip(heads, states)]
        s_upd = [_dot(v[:, sl].T.astype(BF16), kl_b[:, sl]) for sl in heads]
        first.append((states, a_all, o_state, s_upd))
    sample_duos(1)
    for g in groups:
        v_b, eb_last = scaled[g][4], scaled[g][5]
        states, a_all, o_state, s_upd = first[g]
        for h, sl in enumerate(heads):
            a = jnp.where(causal, a_all[h], 0.0).astype(BF16)
            o_ref[g, :, sl] = o_state[h] + _dot(a, v_b[:, sl])
            st_ref[g, h] = states[h] * eb_last[:, sl] + s_upd[h]

    @_when_step(c, n_chunks, n_chunks - 1)
    def _():
        for g in range(n_group):
            for h in range(HEADS):
                sout_ref[g, h] = st_ref[g, h].T


def _hgrn(z, s0, n_batch, n_chunks, n_group, sample=None):
    z3 = z.reshape(n_batch, n_chunks * CHUNK, N_Z_BLOCKS * D_MODEL)
    n_steps = (n_batch // n_group) * n_chunks

    def zblock(zc):
        return pl.BlockSpec((n_group, CHUNK, D_MODEL), lambda b, c: (b, c, zc))

    state = pl.BlockSpec((n_group, HEADS, HEAD_W, HEAD_W), lambda b, c: (b, 0, 0, 0))
    in_specs = [zblock(ZC_F), zblock(ZC_RQ), zblock(ZC_RI), state]
    operands = [z3, z3, z3, s0]
    out_specs = [pl.BlockSpec((n_group, CHUNK, D_MODEL), lambda b, c: (b, c, 0)), state]
    out_shape = [
        jax.ShapeDtypeStruct((n_batch, n_chunks * CHUNK, D_MODEL), F32),
        jax.ShapeDtypeStruct((n_batch, HEADS, HEAD_W, HEAD_W), F32),
    ]
    scratch = [pltpu.VMEM((n_group, HEADS, HEAD_W, HEAD_W), F32)]
    side = None
    if sample is not None:
        lam, zs, kb_s, vb_s, cache_k, cache_v, n_sb, n_q = sample
        n_pairs = n_sb * HEADS
        pps = n_pairs // n_steps
        assert pps * n_steps == n_pairs and pps % SAMPLE_CACHE_SLOTS == 0 and HEADS % pps == 0
        col_groups = HEADS // pps
        past = cache_k.shape[1]
        side = (pps, n_pairs)

        def srow(b, c):
            return (b * n_chunks + c) // col_groups

        def scol(b, c):
            return (b * n_chunks + c) % col_groups

        in_specs += [
            pl.BlockSpec(memory_space=pltpu.SMEM),
            pl.BlockSpec((n_q, pps * HEAD_W), lambda b, c: (srow(b, c), ZC_Q * col_groups + scol(b, c))),
            pl.BlockSpec((n_q, pps * HEAD_W), lambda b, c: (srow(b, c), scol(b, c))),
            pl.BlockSpec((n_q, pps * HEAD_W), lambda b, c: (srow(b, c), scol(b, c))),
            pl.BlockSpec(memory_space=pl.ANY), pl.BlockSpec(memory_space=pl.ANY),
        ]
        operands += [lam, zs, kb_s, vb_s, cache_k, cache_v]
        out_specs.append(pl.BlockSpec((n_q, pps * HEAD_W), lambda b, c: (srow(b, c), scol(b, c))))
        out_shape.append(jax.ShapeDtypeStruct((n_sb * n_q, D_MODEL), F32))
        scratch += [pltpu.VMEM((SAMPLE_CACHE_SLOTS, past, HEAD_W), F32),
                    pltpu.VMEM((SAMPLE_CACHE_SLOTS, past, HEAD_W), F32),
                    pltpu.SemaphoreType.DMA((2, SAMPLE_CACHE_SLOTS))]
    outs = pl.pallas_call(
        functools.partial(_hgrn_kernel, n_chunks, side),
        grid=(n_batch // n_group, n_chunks),
        in_specs=in_specs,
        out_specs=out_specs,
        out_shape=out_shape,
        scratch_shapes=scratch,
        compiler_params=_cparams("arbitrary", "arbitrary"),
        name="hgrn",
    )(*operands)
    return (outs[0].reshape(n_batch * n_chunks * CHUNK, D_MODEL),) + tuple(outs[1:])


def _head_rmsnorm(x, g):
    outs = []
    for h in range(HEADS):
        xh = x[:, h * HEAD_W:(h + 1) * HEAD_W]
        ms = jnp.mean(xh * xh, axis=-1, keepdims=True)
        outs.append(xh * lax.rsqrt(ms + NORM_EPS) * g)
    return jnp.concatenate(outs, axis=-1)


def _merge_kernel(lam_init, ao_ref, ro_ref, rg_ref, ga_ref, gr_ref, x_ref, gs_ref, grn_ref,
                  woa_ref, wor_ref, wout_ref, gffn_ref, wr_ref, br_ref, cnt_in_ref,
                  x1_ref, xn_ref, idx_ref, prob_ref, rank_ref, cnt_ref, run_ref):
    i = pl.program_id(0)
    tm = x_ref.shape[0]

    @pl.when(i == 0)
    def _():
        run_ref[...] = cnt_in_ref[...]

    part = tm // MERGE_PARTS

    def rows(p):
        return slice(p * part, (p + 1) * part)

    def norm_inputs(p, _):
        an = (_head_rmsnorm(ao_ref[rows(p), :], gs_ref[...]) * (1.0 - lam_init)).astype(BF16)
        rn = (_head_rmsnorm(ro_ref[rows(p), :], grn_ref[...]) * jax.nn.silu(rg_ref[rows(p), :])).astype(BF16)
        return an, rn

    def branch_projections(p, st):
        an, rn = st
        return _dot(an, woa_ref[...]), _dot(rn, wor_ref[...])

    def gate(p, st):
        a, r = st
        return (jax.nn.sigmoid(ga_ref[rows(p), :]) * a + jax.nn.sigmoid(gr_ref[rows(p), :]) * r).astype(BF16)

    def residual(p, mixed):
        x1 = x_ref[rows(p), :] + _dot(mixed, wout_ref[...])
        x1_ref[rows(p), :] = x1
        return x1

    def ffn_norm_router(p, x1):
        ms = jnp.mean(x1 * x1, axis=-1, keepdims=True)
        xn = x1 * lax.rsqrt(ms + NORM_EPS) * gffn_ref[...]
        for s in range(ROW_SUB):
            xn_ref[pl.ds(p * part * ROW_SUB + s, part, stride=ROW_SUB), :] = xn[:, s * HEAD_W:(s + 1) * HEAD_W]
        return _dot(xn.astype(BF16), wr_ref[...])

    stages = (norm_inputs, branch_projections, gate, residual, ffn_norm_router)
    state = [None] * MERGE_PARTS
    for stage in stages:
        for p in range(MERGE_PARTS):
            state[p] = stage(p, state[p])
    logits = jnp.concatenate(state, axis=0) + br_ref[...]
    lane = lax.broadcasted_iota(jnp.int32, logits.shape, 1).astype(F32)
    sel = jnp.zeros(logits.shape, F32)
    work = logits
    tops, idxs = [], []
    for _ in range(TOP_K):
        m = jnp.max(work, axis=-1, keepdims=True)
        idx = jnp.min(jnp.where(work == m, lane, float(N_EXPERTS)), axis=-1, keepdims=True)
        hit = lane == idx
        sel = jnp.where(hit, 1.0, sel)
        work = jnp.where(hit, -jnp.inf, work)
        tops.append(m)
        idxs.append(idx)
    es = [jnp.exp(v - tops[0]) for v in tops]
    inv = 1.0 / (es[0] + es[1] + es[2] + es[3])

    trow = lax.broadcasted_iota(jnp.int32, (tm, tm), 0)
    tcol = lax.broadcasted_iota(jnp.int32, (tm, tm), 1)
    before = jnp.where(trow > tcol, 1.0, 0.0).astype(BF16)
    ranks = _dot(before, sel.astype(BF16)) + run_ref[...]
    k_lane = lax.broadcasted_iota(jnp.int32, (tm, TOP_K), 1)
    idx_out = jnp.zeros((tm, TOP_K), F32)
    prob_out = jnp.zeros((tm, TOP_K), F32)
    rank_out = jnp.zeros((tm, TOP_K), F32)
    for k in range(TOP_K):
        rk = jnp.sum(jnp.where(lane == idxs[k], ranks, 0.0), axis=-1, keepdims=True)
        idx_out = jnp.where(k_lane == k, idxs[k], idx_out)
        prob_out = jnp.where(k_lane == k, es[k] * inv, prob_out)
        rank_out = jnp.where(k_lane == k, rk, rank_out)
    idx_ref[...] = idx_out.astype(jnp.int32)
    prob_ref[...] = prob_out
    rank_ref[...] = rank_out.astype(jnp.int32)
    run = run_ref[...] + jnp.sum(sel, axis=0, keepdims=True)
    run_ref[...] = run
    cnt_ref[...] = run


def _merge(lam_init, ao, ro, z, x, g_subln, g_rec, woa, wor, wout, g_ffn, w_router, b_router, cnt_in):
    t = x.shape[0]
    tm = MERGE_TM
    row = lambda i: (i, 0)
    fixed = lambda i: (0, 0)
    tok = pl.BlockSpec((tm, D_MODEL), row)
    wspec = pl.BlockSpec((D_MODEL, D_MODEL), fixed)
    narrow = pl.BlockSpec((tm, TOP_K), row)
    return pl.pallas_call(
        functools.partial(_merge_kernel, lam_init),
        grid=(t // tm,),
        in_specs=[
            tok, tok,
            pl.BlockSpec((tm, D_MODEL), lambda i: (i, ZC_RG)),
            pl.BlockSpec((tm, D_MODEL), lambda i: (i, ZC_GA)),
            pl.BlockSpec((tm, D_MODEL), lambda i: (i, ZC_GR)),
            tok,
            pl.BlockSpec((1, HEAD_W), fixed), pl.BlockSpec((1, HEAD_W), fixed),
            wspec, wspec, wspec,
            pl.BlockSpec((1, D_MODEL), fixed),
            pl.BlockSpec((D_MODEL, N_EXPERTS), fixed),
            pl.BlockSpec((1, N_EXPERTS), fixed),
            pl.BlockSpec((1, N_EXPERTS), fixed),
        ],
        out_specs=[tok, pl.BlockSpec((tm * ROW_SUB, HEAD_W), row), narrow, narrow, narrow,
                   pl.BlockSpec((1, N_EXPERTS), fixed)],
        out_shape=[
            jax.ShapeDtypeStruct((t, D_MODEL), F32),
            jax.ShapeDtypeStruct((t * ROW_SUB, HEAD_W), F32),
            jax.ShapeDtypeStruct((t, TOP_K), jnp.int32),
            jax.ShapeDtypeStruct((t, TOP_K), F32),
            jax.ShapeDtypeStruct((t, TOP_K), jnp.int32),
            jax.ShapeDtypeStruct((1, N_EXPERTS), F32),
        ],
        scratch_shapes=[pltpu.VMEM((1, N_EXPERTS), F32)],
        compiler_params=_cparams("arbitrary"),
        name="merge",
    )(ao, ro, z, z, z, x, g_subln, g_rec, woa, wor, wout, g_ffn, w_router, b_router, cnt_in)


def _row_copy(src, dst, sem):
    return pltpu.make_async_copy(src, dst, sem)


def _dispatch_kernel(n_first, n_steps, pos_ref, xa_hbm, xb_hbm, xs_hbm, stage, load_sem, copy_sem):
    i = pl.program_id(0)
    tm = ROW_TM
    rows = tm * ROW_SUB
    slot = i % DISPATCH_SLOTS

    def row_tile(off):
        return pl.ds(pl.multiple_of(off, ROW_SUB), ROW_SUB)

    def load(t, sl):
        @pl.when(t < n_first)
        def _():
            _row_copy(xa_hbm.at[pl.ds(pl.multiple_of(t * rows, rows), rows)], stage.at[sl], load_sem.at[sl]).start()

        @pl.when(t >= n_first)
        def _():
            _row_copy(xb_hbm.at[pl.ds(pl.multiple_of((t - n_first) * rows, rows), rows)], stage.at[sl],
                      load_sem.at[sl]).start()

    def drain(parity):
        for k in range(TOP_K):
            _row_copy(stage.at[0], xs_hbm.at[pl.ds(0, rows)], copy_sem.at[parity]).wait()

    @pl.when(i == 0)
    def _():
        load(i, slot)

    @pl.when(i + 1 < n_steps)
    def _():
        load(i + 1, (i + 1) % DISPATCH_SLOTS)

    _row_copy(xa_hbm.at[pl.ds(0, rows)], stage.at[slot], load_sem.at[slot]).wait()

    def start(g, carry):
        r0 = g * ROW_COPY_UNROLL
        n = ROW_COPY_UNROLL * TOP_K
        dst = [pos_ref[0, 0, r0 * TOP_K + j] for j in range(n)]
        for j in range(n):
            _row_copy(stage.at[slot, row_tile((r0 + j // TOP_K) * ROW_SUB)],
                      xs_hbm.at[row_tile(dst[j])], copy_sem.at[i % 2]).start()
        return carry

    lax.fori_loop(0, tm // ROW_COPY_UNROLL, start, 0)

    @pl.when(i > 0)
    def _():
        drain((i - 1) % 2)

    @pl.when(i == n_steps - 1)
    def _():
        drain(i % 2)


def _dispatch(pos, xn_a, xn_b):
    tm = ROW_TM
    n_a, n_b = xn_a.shape[0] // (tm * ROW_SUB), xn_b.shape[0] // (tm * ROW_SUB)
    pos3 = pos.reshape(n_a + n_b, 1, tm * TOP_K)
    return pl.pallas_call(
        functools.partial(_dispatch_kernel, n_a, n_a + n_b),
        grid=(n_a + n_b,),
        in_specs=[
            pl.BlockSpec((1, 1, tm * TOP_K), lambda i: (i, 0, 0), memory_space=pltpu.SMEM),
            pl.BlockSpec(memory_space=pl.ANY),
            pl.BlockSpec(memory_space=pl.ANY),
        ],
        out_specs=pl.BlockSpec(memory_space=pl.ANY),
        out_shape=jax.ShapeDtypeStruct((pos.size * ROW_SUB, HEAD_W), F32),
        scratch_shapes=[
            pltpu.VMEM((DISPATCH_SLOTS, tm * ROW_SUB, HEAD_W), F32),
            pltpu.SemaphoreType.DMA((DISPATCH_SLOTS,)),
            pltpu.SemaphoreType.DMA((2,)),
        ],
        compiler_params=_cparams("arbitrary"),
        name="dispatch",
    )(pos3, xn_a, xn_b)


def _experts_kernel(tile_ref, exp_ref, lo_ref, hi_ref, fresh_ref, par_ref, nxt_ref, more_ref,
                    xs_ref, wg_hbm, bg_ref, wu_hbm, bu_ref, wd_hbm, bd_ref, o_ref,
                    wg_s, wu_s, wd_s, x_s, acc_s, wf_s, wsem):
    w = pl.program_id(0)
    lo = lo_ref[w]
    hi = hi_ref[w]
    tm = x_s.shape[0]

    def lanes(s):
        return slice(s * HEAD_W, (s + 1) * HEAD_W)

    def sub_rows(s):
        return pl.ds(s, tm, stride=ROW_SUB)

    def weight_copies(e, sl):
        return [pltpu.make_async_copy(src.at[e], wf_s.at[sl, j], wsem.at[sl])
                for j, src in enumerate((wg_hbm, wu_hbm, wd_hbm))]

    @pl.when(w == 0)
    def _():
        for c in weight_copies(exp_ref[0], par_ref[0]):
            c.start()
        acc_s[...] = jnp.zeros(acc_s.shape, F32)

    @pl.when(fresh_ref[w] == 1)
    def _():
        sl = par_ref[w]
        for c in weight_copies(exp_ref[w], sl):
            c.wait()

        @pl.when(more_ref[w] == 1)
        def _():
            for c in weight_copies(nxt_ref[w], 1 - sl):
                c.start()

        wg_s[...] = wf_s[sl, 0].astype(BF16)
        wu_s[...] = wf_s[sl, 1].astype(BF16)
        wd_s[...] = wf_s[sl, 2].astype(BF16)

    @pl.when(hi > lo)
    def _():
        for s in range(ROW_SUB):
            x_s[:, lanes(s)] = xs_ref[sub_rows(s), :].astype(BF16)
        x = x_s[...]
        out = bd_ref[0]
        def gate_up(c):
            ff = slice(c, c + MOE_FF_SLICE)
            return _dot(x, wg_s[:, ff]) + bg_ref[0, :, ff], _dot(x, wu_s[:, ff]) + bu_ref[0, :, ff]

        slices = range(0, D_MODEL, MOE_FF_SLICE)
        gus = [gate_up(c) for c in slices]
        acts = []
        for gate, up in gus:
            gate = jnp.minimum(gate, SWIGLU_LIMIT)
            up = jnp.clip(up, -SWIGLU_LIMIT, SWIGLU_LIMIT)
            acts.append((gate * jax.nn.sigmoid(SWIGLU_ALPHA * gate) * (up + 1.0)).astype(BF16))
        for c, act in zip(slices, acts):
            out = out + _dot(act, wd_s[c:c + MOE_FF_SLICE, :])
        row = lax.broadcasted_iota(jnp.int32, (tm, 1), 0)
        merged = jnp.where((row >= lo) & (row < hi), out, acc_s[...])
        acc_s[...] = merged
        for s in range(ROW_SUB):
            o_ref[sub_rows(s), :] = merged[:, lanes(s)]


def _experts(plan, xs, w_gate, b_gate, w_up, b_up, w_down, b_down):
    tm = MOE_TM
    n_items = plan[0].shape[0]
    rows = lambda w, tile, *_: (tile[w], 0)
    wspec = pl.BlockSpec(memory_space=pl.ANY)
    bspec = pl.BlockSpec((1, 1, D_MODEL), lambda w, tile, exp, *_: (exp[w], 0, 0))
    grid_spec = pltpu.PrefetchScalarGridSpec(
        num_scalar_prefetch=len(plan),
        grid=(n_items,),
        in_specs=[pl.BlockSpec((tm * ROW_SUB, HEAD_W), rows), wspec, bspec, wspec, bspec, wspec, bspec],
        out_specs=pl.BlockSpec((tm * ROW_SUB, HEAD_W), rows),
        scratch_shapes=[pltpu.VMEM((D_MODEL, D_MODEL), BF16)] * 3 + [
            pltpu.VMEM((tm, D_MODEL), BF16),
            pltpu.VMEM((tm, D_MODEL), F32),
            pltpu.VMEM((2, 3, D_MODEL, D_MODEL), F32),
            pltpu.SemaphoreType.DMA((2,)),
        ],
    )
    return pl.pallas_call(
        _experts_kernel,
        grid_spec=grid_spec,
        out_shape=jax.ShapeDtypeStruct(xs.shape, F32),
        compiler_params=_cparams("arbitrary"),
        name="experts",
    )(*plan, xs, w_gate, b_gate.reshape(N_EXPERTS, 1, D_MODEL), w_up, b_up.reshape(N_EXPERTS, 1, D_MODEL),
      w_down, b_down.reshape(N_EXPERTS, 1, D_MODEL))


def _expert_plan(counts, n_rows):
    tm = MOE_TM
    n_tiles = n_rows // tm
    n_items = n_tiles + N_EXPERTS - 1
    ends = jnp.cumsum(counts)
    starts = ends - counts
    first_tile = starts // tm
    last_tile = jnp.maximum(ends - 1, 0) // tm
    items = jnp.where(counts > 0, last_tile - first_tile + 1, 0)
    item_end = jnp.cumsum(items)
    item_start = item_end - items
    total = item_end[-1]
    w = jnp.arange(n_items, dtype=jnp.int32)
    wc = jnp.minimum(w, total - 1)
    exp = jnp.sum((item_end[None, :] <= wc[:, None]).astype(jnp.int32), axis=1)
    ids = jnp.arange(N_EXPERTS, dtype=jnp.int32)
    mine = exp[:, None] == ids[None, :]

    def of_item(per_expert):
        return jnp.sum(jnp.where(mine, per_expert[None, :], 0), axis=1).astype(jnp.int32)

    tile = of_item(first_tile) + wc - of_item(item_start)
    lo = jnp.maximum(of_item(starts), tile * tm) - tile * tm
    hi = jnp.minimum(of_item(ends), (tile + 1) * tm) - tile * tm
    valid = w < total
    lo = jnp.where(valid, lo, 0).astype(jnp.int32)
    hi = jnp.where(valid, hi, 0).astype(jnp.int32)
    prev_exp = jnp.concatenate([jnp.full((1,), -1, jnp.int32), exp[:-1]])
    fresh = (valid & (exp != prev_exp)).astype(jnp.int32)
    later = jnp.where((counts > 0)[None, :] & (ids[None, :] > ids[:, None]), ids[None, :], N_EXPERTS)
    next_exp = jnp.min(later, axis=1)
    parity = ((jnp.cumsum((counts > 0).astype(jnp.int32)) - 1) % 2).astype(jnp.int32)
    next_of_item = of_item(next_exp)
    more = (next_of_item < N_EXPERTS).astype(jnp.int32)
    nxt = jnp.where(more == 1, next_of_item, exp).astype(jnp.int32)
    return tile, exp, lo, hi, fresh, of_item(parity), nxt, more


def _combine_kernel(n_steps, pos_ref, pos_next_ref, x1_ref, prob_ref, gfin_ref, ys_ref, y_ref, buf_ref, sem):
    i = pl.program_id(0)
    tm = x1_ref.shape[0]
    slot = i % 2

    def row_tile(off):
        return pl.ds(pl.multiple_of(off, ROW_SUB), ROW_SUB)

    def gather(p_ref, sl):
        def start(g, carry):
            r0 = g * ROW_COPY_UNROLL
            n = ROW_COPY_UNROLL * TOP_K
            src = [p_ref[0, 0, r0 * TOP_K + j] for j in range(n)]
            for j in range(n):
                _row_copy(ys_ref.at[row_tile(src[j])],
                          buf_ref.at[sl, j % TOP_K, row_tile((r0 + j // TOP_K) * ROW_SUB)], sem.at[sl]).start()
            return carry

        lax.fori_loop(0, tm // ROW_COPY_UNROLL, start, 0)

    @pl.when(i == 0)
    def _():
        gather(pos_ref, slot)

    @pl.when(i + 1 < n_steps)
    def _():
        gather(pos_next_ref, 1 - slot)

    for k in range(TOP_K):
        _row_copy(ys_ref.at[pl.ds(0, tm * ROW_SUB)], buf_ref.at[slot, k], sem.at[slot]).wait()

    prob = prob_ref[...]
    parts = []
    for s in range(ROW_SUB):
        part = x1_ref[:, s * HEAD_W:(s + 1) * HEAD_W]
        for k in range(TOP_K):
            part = part + prob[:, k:k + 1] * buf_ref[slot, k, pl.ds(s, tm, stride=ROW_SUB), :]
        parts.append(part)
    x2 = jnp.concatenate(parts, axis=-1)
    ms = jnp.mean(x2 * x2, axis=-1, keepdims=True)
    y_ref[...] = x2 * lax.rsqrt(ms + NORM_EPS) * gfin_ref[...]


def _combine(pos, x1, prob, g_final, ys):
    t = x1.shape[0]
    tm = ROW_TM
    n = t // tm
    pos3 = pos.reshape(n, 1, tm * TOP_K)
    return pl.pallas_call(
        functools.partial(_combine_kernel, n),
        grid=(n,),
        in_specs=[
            pl.BlockSpec((1, 1, tm * TOP_K), lambda i: (i, 0, 0), memory_space=pltpu.SMEM),
            pl.BlockSpec((1, 1, tm * TOP_K), lambda i: (jnp.minimum(i + 1, n - 1), 0, 0), memory_space=pltpu.SMEM),
            pl.BlockSpec((tm, D_MODEL), lambda i: (i, 0)),
            pl.BlockSpec((tm, TOP_K), lambda i: (i, 0)),
            pl.BlockSpec((1, D_MODEL), lambda i: (0, 0)),
            pl.BlockSpec(memory_space=pl.ANY),
        ],
        out_specs=pl.BlockSpec((tm, D_MODEL), lambda i: (i, 0)),
        out_shape=jax.ShapeDtypeStruct((t, D_MODEL), F32),
        scratch_shapes=[pltpu.VMEM((2, TOP_K, tm * ROW_SUB, HEAD_W), F32), pltpu.SemaphoreType.DMA((2,))],
        compiler_params=_cparams("arbitrary"),
        name="combine",
    )(pos3, pos3, x1, prob, g_final, ys)


def _lambda_init(layer):
    return 0.8 - 0.6 * math.exp(-0.3 * layer)


def kernel(x_prompt, x_sample, cache_k, cache_v, state_rec, g_mix, w_in, lambda_q1, lambda_k1, lambda_q2, lambda_k2, g_subln, w_o_attn, lb_logits, g_rec_norm, w_o_rec, w_out, g_ffn, w_router, b_router, w_gate, b_gate, w_up, b_up, w_down, b_down, g_final):
    nb, seq, _ = x_prompt.shape
    nbd, ld, _ = x_sample.shape
    past = cache_k.shape[2]
    tp, ts = nb * seq, nbd * ld
    lam_init = _lambda_init(0)
    lam = (jnp.exp(jnp.sum(lambda_q1[0].astype(F32) * lambda_k1[0].astype(F32)))
           - jnp.exp(jnp.sum(lambda_q2[0].astype(F32) * lambda_k2[0].astype(F32))) + lam_init).reshape(1)

    w_in_b = w_in[0].astype(BF16)
    woa, wor, wout = w_o_attn[0].astype(BF16), w_o_rec[0].astype(BF16), w_out[0].astype(BF16)
    wr = w_router[0].astype(BF16)
    g_mix2, g_ffn2, g_fin2 = g_mix[0].reshape(1, -1), g_ffn[0].reshape(1, -1), g_final.reshape(1, -1)
    gs2, grn2 = g_subln[0].reshape(1, -1), g_rec_norm[0].reshape(1, -1)
    br2 = b_router[0].reshape(1, -1)

    cos_p, sin_p = _rope_tables(jnp.arange(seq, dtype=jnp.int32))
    pos_s = past + (jnp.arange(PROJ_TM, dtype=jnp.int32) % ld)
    cos_s, sin_s = _rope_tables(pos_s)

    xp = x_prompt.reshape(tp, D_MODEL)
    xs_in = x_sample.reshape(ts, D_MODEL)
    zp, k_p, v_p, kb_p, vb_p = _project(xp, g_mix2, w_in_b, cos_p, sin_p, lb_logits)
    zs, k_s, v_s, kb_s, vb_s = _project(xs_in, g_mix2, w_in_b, cos_s, sin_s, lb_logits)

    ao_p = _attn_prompt(lam, zp, kb_p, vb_p, nb, seq)
    ro_p, st_p, ao_s = _hgrn(zp, jnp.zeros((nb, HEADS, HEAD_W, HEAD_W), F32), nb, seq // CHUNK, HGRN_GROUP_PROMPT,
                             sample=(lam, zs, kb_s, vb_s, cache_k[0], cache_v[0], nbd, ld))
    ro_s, st_s = _hgrn(zs, state_rec[0], nbd, ld // CHUNK, HGRN_GROUP_SAMPLE)

    merge_w = (gs2, grn2, woa, wor, wout, g_ffn2, wr, br2)
    zero_cnt = jnp.zeros((1, N_EXPERTS), F32)
    x1_p, xn_p, idx_p, prob_p, rank_p, cnt_p = _merge(lam_init, ao_p, ro_p, zp, xp, *merge_w, zero_cnt)
    x1_s, xn_s, idx_s, prob_s, rank_s, cnt = _merge(lam_init, ao_s, ro_s, zs, xs_in, *merge_w, cnt_p)

    counts = cnt[0].astype(jnp.int32)
    starts = jnp.cumsum(counts) - counts
    experts = jnp.arange(N_EXPERTS, dtype=jnp.int32)

    def positions(idx, rank):
        return (jnp.sum(jnp.where(idx[..., None] == experts, starts, 0), axis=-1) + rank) * ROW_SUB

    pos_p = positions(idx_p, rank_p)
    pos_s2 = positions(idx_s, rank_s)
    n_rows = (tp + ts) * TOP_K
    plan = _expert_plan(counts, n_rows)

    xs_sorted = _dispatch(jnp.concatenate([pos_p, pos_s2], axis=0), xn_p, xn_s)
    ys = _experts(plan, xs_sorted, w_gate[0], b_gate[0], w_up[0], b_up[0], w_down[0], b_down[0])
    y_p = _combine(pos_p, x1_p, prob_p, g_fin2, ys)
    y_s = _combine(pos_s2, x1_s, prob_s, g_fin2, ys)

    return (y_p.reshape(nb, seq, D_MODEL), y_s.reshape(nbd, ld, D_MODEL),
            k_p.reshape(1, nb, seq, HEADS, HEAD_W), v_p.reshape(1, nb, seq, HEADS, HEAD_W), st_p[None],
            k_s.reshape(1, nbd, ld, HEADS, HEAD_W), v_s.reshape(1, nbd, ld, HEADS, HEAD_W), st_s[None])
```

```python
import functools
import math

import jax
import jax.numpy as jnp
from jax import lax
from jax.experimental import pallas as pl
from jax.experimental.pallas import tpu as pltpu

F32 = jnp.float32
BF16 = jnp.bfloat16

D_MODEL = 1024
CHUNK = 64
HEADS = 8
HEAD_W = 128
ATTN_HEAD_DIM = 64
ROPE_THETA = 10000.0
Q_SCALE = ATTN_HEAD_DIM ** -0.5 * math.log2(math.e)
N_EXPERTS = 32
TOP_K = 4
SWIGLU_ALPHA = 1.702
SWIGLU_LIMIT = 7.0
NORM_EPS = 1e-6
N_COL_BLOCKS = 9
COL_Q, COL_K, COL_V, COL_F = range(4)
N_Z_BLOCKS = 7
ZC_Q, ZC_F, ZC_RQ, ZC_RI, ZC_RG, ZC_GA, ZC_GR = range(7)
PROJ_TM = 256

ATTN_TQ = 256
ATTN_TILE_GROUP = 4
CACHE_COPY_PARTS = 4
SAMPLE_CACHE_SLOTS = 4
HGRN_GROUP_PROMPT = 4
HGRN_GROUP_SAMPLE = 4
MERGE_TM = 512
MERGE_PARTS = 2
MOE_TM = 256
MOE_FF_SLICE = 512
ROW_TM = 256
ROW_COPY_UNROLL = 4
DISPATCH_SLOTS = 3
ROW_SUB = D_MODEL // HEAD_W
VMEM_LIMIT = 56 * 1024 * 1024


def _cparams(*sem):
    return pltpu.CompilerParams(dimension_semantics=sem, vmem_limit_bytes=VMEM_LIMIT)


def _dot(a, b):
    return jnp.dot(a, b, preferred_element_type=F32)


def _dot_nt(a, b):
    return lax.dot_general(a, b, (((1,), (1,)), ((), ())), preferred_element_type=F32)


def _proj_kernel(x_ref, g_ref, w_ref, cos_ref, sin_ref, lbl_ref,
                 z_ref, kout_ref, vout_ref, kb_ref, vb_ref):
    tm = x_ref.shape[0]
    x = x_ref[...]
    ms = jnp.mean(x * x, axis=-1, keepdims=True)
    hn = (x * lax.rsqrt(ms + NORM_EPS) * g_ref[...]).astype(BF16)

    def col(j):
        return _dot(hn, w_ref[:, j * D_MODEL:(j + 1) * D_MODEL])

    def zcols(c):
        return slice(c * D_MODEL, (c + 1) * D_MODEL)

    def head(h):
        return slice(h * HEAD_W, (h + 1) * HEAD_W)

    def head_rows(h):
        return pl.ds(h, tm, stride=HEADS)

    def rope(zh):
        lane = lax.broadcasted_iota(jnp.int32, zh.shape, 1)
        first = (lane % ATTN_HEAD_DIM) < (ATTN_HEAD_DIM // 2)
        partner = jnp.where(first, pltpu.roll(zh, HEAD_W - ATTN_HEAD_DIM // 2, 1),
                            pltpu.roll(zh, ATTN_HEAD_DIM // 2, 1))
        return zh * cos_ref[...] + partner * sin_ref[...]

    zq = col(COL_Q)
    for h in range(HEADS):
        z_ref[:, head(h)] = rope(zq[:, head(h)]) * Q_SCALE

    zk = col(COL_K)
    for h in range(HEADS):
        kh = rope(zk[:, head(h)])
        kout_ref[head_rows(h), :] = kh
        kb_ref[:, head(h)] = kh.astype(BF16)

    zv = col(COL_V)
    for h in range(HEADS):
        vout_ref[head_rows(h), :] = zv[:, head(h)]
    vb_ref[...] = zv.astype(BF16)

    lbl = lbl_ref[...]
    e = jnp.exp(lbl - jnp.max(lbl, axis=0, keepdims=True))
    lb = e[0:1] / jnp.sum(e, axis=0, keepdims=True)
    z_ref[:, zcols(ZC_F)] = lb + (1.0 - lb) * jax.nn.sigmoid(col(COL_F))

    for j in range(COL_F + 1, N_COL_BLOCKS):
        z_ref[:, zcols(j - COL_V)] = col(j)


def _project(x, g, w_bf16, cos_t, sin_t, lb_logits):
    t = x.shape[0]
    tm = PROJ_TM
    n_tab = cos_t.shape[0] // tm
    tok = lambda i: (i, 0)
    fixed = lambda i: (0, 0)
    return pl.pallas_call(
        _proj_kernel,
        grid=(t // tm,),
        in_specs=[
            pl.BlockSpec((tm, D_MODEL), tok),
            pl.BlockSpec((1, D_MODEL), fixed),
            pl.BlockSpec((D_MODEL, N_COL_BLOCKS * D_MODEL), fixed, pipeline_mode=pl.Buffered(1)),
            pl.BlockSpec((tm, HEAD_W), lambda i: (i % n_tab, 0)),
            pl.BlockSpec((tm, HEAD_W), lambda i: (i % n_tab, 0)),
            pl.BlockSpec((2, D_MODEL), fixed),
        ],
        out_specs=[
            pl.BlockSpec((tm, N_Z_BLOCKS * D_MODEL), tok),
            pl.BlockSpec((tm * HEADS, HEAD_W), tok),
            pl.BlockSpec((tm * HEADS, HEAD_W), tok),
            pl.BlockSpec((tm, D_MODEL), tok),
            pl.BlockSpec((tm, D_MODEL), tok),
        ],
        out_shape=[
            jax.ShapeDtypeStruct((t, N_Z_BLOCKS * D_MODEL), F32),
            jax.ShapeDtypeStruct((t * HEADS, HEAD_W), F32),
            jax.ShapeDtypeStruct((t * HEADS, HEAD_W), F32),
            jax.ShapeDtypeStruct((t, D_MODEL), BF16),
            jax.ShapeDtypeStruct((t, D_MODEL), BF16),
        ],
        compiler_params=_cparams("arbitrary"),
        name="proj",
    )(x, g, w_bf16, cos_t, sin_t, lb_logits)


def _rope_tables(pos):
    half = ATTN_HEAD_DIM // 2
    inv = jnp.power(ROPE_THETA, -2.0 * jnp.arange(half, dtype=F32) / ATTN_HEAD_DIM)
    ang = pos.astype(F32)[:, None] * inv[None, :]
    cos = jnp.tile(jnp.cos(ang), (1, HEAD_W // half))
    sin = jnp.sin(ang)
    sin = jnp.tile(jnp.concatenate([-sin, sin], axis=-1), (1, HEAD_W // ATTN_HEAD_DIM))
    return cos, sin


def _split_components(q):
    lane = lax.broadcasted_iota(jnp.int32, q.shape, 1)
    q1 = jnp.where(lane < ATTN_HEAD_DIM, q, 0.0)
    q2 = jnp.where(lane >= ATTN_HEAD_DIM, q, 0.0)
    return jnp.concatenate([q1, q2], axis=0).astype(BF16)


def _attn_prompt_kernel(lam_ref, q_ref, k_ref, v_ref, o_ref):
    lam = lam_ref[0]
    tq = ATTN_TQ
    seq = k_ref.shape[0]
    k = k_ref[...]
    v = v_ref[...]
    row = lax.broadcasted_iota(jnp.int32, (2 * tq, tq), 0)
    col = lax.broadcasted_iota(jnp.int32, (2 * tq, tq), 1)
    visible = (col // CHUNK) <= ((row % tq) // CHUNK)
    n_tiles = seq // tq

    def scores(qi):
        lo = qi * tq
        qq = _split_components(q_ref[lo:lo + tq, :])
        s_d = jnp.where(visible, _dot_nt(qq, k[lo:lo + tq]), -jnp.inf)
        return s_d, (_dot_nt(qq, k[:lo]) if qi > 0 else None)

    def weighted_values(qi, w_d, w_m):
        lo = qi * tq
        o = _dot(w_d, v[lo:lo + tq])
        if qi > 0:
            o = o + _dot(w_m, v[:lo])
        o_ref[lo:lo + tq, :] = o

    def softmax_weights(qi, s_d, s_m):
        m = jnp.max(s_d, axis=-1, keepdims=True)
        if qi > 0:
            m = jnp.maximum(m, jnp.max(s_m, axis=-1, keepdims=True))
            p_m = jnp.exp2(s_m - m)
        p_d = jnp.exp2(s_d - m)
        l = jnp.sum(p_d, axis=-1, keepdims=True)
        if qi > 0:
            l = l + jnp.sum(p_m, axis=-1, keepdims=True)
        r = 1.0 / l
        r1 = r[:tq]
        r2 = lam * r[tq:]
        w_d = (p_d[:tq] * r1 - p_d[tq:] * r2).astype(BF16)
        w_m = (p_m[:tq] * r1 - p_m[tq:] * r2).astype(BF16) if qi > 0 else None
        return w_d, w_m

    group = ATTN_TILE_GROUP
    ahead = [scores(qi) for qi in range(group)]
    for base in range(0, n_tiles, group):
        current = ahead
        if base + group < n_tiles:
            ahead = [scores(base + group + t) for t in range(group)]
        weights = [softmax_weights(base + t, *current[t]) for t in range(group)]
        for t in range(group):
            weighted_values(base + t, *weights[t])


def _attn_prompt(lam, z, kb, vb, n_batch, seq):
    return pl.pallas_call(
        _attn_prompt_kernel,
        grid=(n_batch, HEADS),
        in_specs=[
            pl.BlockSpec(memory_space=pltpu.SMEM),
            pl.BlockSpec((seq, HEAD_W), lambda b, h: (b, ZC_Q * HEADS + h)),
            pl.BlockSpec((seq, HEAD_W), lambda b, h: (b, h)),
            pl.BlockSpec((seq, HEAD_W), lambda b, h: (b, h)),
        ],
        out_specs=pl.BlockSpec((seq, HEAD_W), lambda b, h: (b, h)),
        out_shape=jax.ShapeDtypeStruct((n_batch * seq, D_MODEL), F32),
        compiler_params=_cparams("arbitrary", "arbitrary"),
        name="attn_prompt",
    )(lam, z, kb, vb)


def _sample_attention(lam, qs, kns, vns, ck_refs, cv_refs):
    n_q = qs[0].shape[0]
    qqs = [_split_components(q) for q in qs]
    scores = [(_dot_nt(qq, ck[...].astype(BF16)), _dot_nt(qq, kn)) for qq, ck, kn in zip(qqs, ck_refs, kns)]
    weights = []
    for s_c, s_n in scores:
        m = jnp.maximum(jnp.max(s_c, axis=-1, keepdims=True), jnp.max(s_n, axis=-1, keepdims=True))
        p_c = jnp.exp2(s_c - m)
        p_n = jnp.exp2(s_n - m)
        r = 1.0 / (jnp.sum(p_c, axis=-1, keepdims=True) + jnp.sum(p_n, axis=-1, keepdims=True))
        r1 = r[:n_q]
        r2 = lam * r[n_q:]
        weights.append(((p_c[:n_q] * r1 - p_c[n_q:] * r2).astype(BF16), (p_n[:n_q] * r1 - p_n[n_q:] * r2).astype(BF16)))
    return [_dot(w_c, cv[...].astype(BF16)) + _dot(w_n, vn) for (w_c, w_n), cv, vn in zip(weights, cv_refs, vns)]


def _split3(x):
    h1 = x.astype(BF16)
    r1 = x - h1.astype(F32)
    h2 = r1.astype(BF16)
    h3 = (r1 - h2.astype(F32)).astype(BF16)
    return h1, h2, h3


def _when_step(step, n_steps, which):
    def deco(body):
        if n_steps == 1:
            body()
        else:
            pl.when(step == which)(body)
    return deco


def _hgrn_kernel(n_chunks, side, *refs):
    if side is None:
        f_ref, q_ref, i_ref, s0_ref, o_ref, sout_ref, st_ref = refs
    else:
        (f_ref, q_ref, i_ref, s0_ref, lam_ref, qs_ref, kns_ref, vns_ref, ck_hbm, cv_hbm,
         o_ref, sout_ref, os_ref, st_ref, kbuf, vbuf, sem) = refs
    c = pl.program_id(1)
    n_group = f_ref.shape[0]

    @_when_step(c, n_chunks, 0)
    def _():
        for g in range(n_group):
            for h in range(HEADS):
                st_ref[g, h] = s0_ref[g, h].T

    if side is not None:
        pairs_per_step, n_pairs = side
        step = pl.program_id(0) * n_chunks + c

        def cache_copies(pair, sl):
            bb, hh = pair // HEADS, pair % HEADS
            part = kbuf.shape[1] // CACHE_COPY_PARTS
            copies = []
            for j in range(CACHE_COPY_PARTS):
                keys = pl.ds(j * part, part)
                copies.append(pltpu.make_async_copy(ck_hbm.at[bb, keys, hh, :], kbuf.at[sl, keys], sem.at[0, sl]))
                copies.append(pltpu.make_async_copy(cv_hbm.at[bb, keys, hh, :], vbuf.at[sl, keys], sem.at[1, sl]))
            return copies

        n_slots = kbuf.shape[0]
        duo = n_slots // 2

        @pl.when(step == 0)
        def _():
            for p in range(duo):
                for cp in cache_copies(p, p):
                    cp.start()

        def sample_duo(d):
            first = step * pairs_per_step + d * duo
            js = [d * duo + t for t in range(duo)]
            slots = [j % n_slots for j in js]

            @pl.when(first + duo < n_pairs)
            def _():
                for t in range(duo):
                    for cp in cache_copies(first + duo + t, (js[t] + duo) % n_slots):
                        cp.start()

            for t in range(duo):
                for cp in cache_copies(first + t, slots[t]):
                    cp.wait()
            lanes = [slice(j * HEAD_W, (j + 1) * HEAD_W) for j in js]
            outs = _sample_attention(lam_ref[0], [qs_ref[:, sl] for sl in lanes], [kns_ref[:, sl] for sl in lanes],
                                     [vns_ref[:, sl] for sl in lanes], [kbuf.at[sl] for sl in slots],
                                     [vbuf.at[sl] for sl in slots])
            for sl, o in zip(lanes, outs):
                os_ref[:, sl] = o

        duos_per_step = pairs_per_step // duo
    else:
        duos_per_step = 0

        def sample_duo(d):
            del d

    def sample_duos(half):
        for d in range(half * duos_per_step // 2, (half + 1) * duos_per_step // 2):
            sample_duo(d)

    row = lax.broadcasted_iota(jnp.int32, (CHUNK, CHUNK), 0)
    col = lax.broadcasted_iota(jnp.int32, (CHUNK, CHUNK), 1)
    causal = row >= col
    tril = jnp.where(causal, 1.0, 0.0).astype(BF16)
    heads = [slice(h * HEAD_W, (h + 1) * HEAD_W) for h in range(HEADS)]
    groups = range(n_group)
    sample_duos(0)
    fs = [f_ref[g] for g in groups]
    splits = [_split3(jnp.log(f)) for f in fs]
    bs = [_dot(tril, h1) + _dot(tril, h2) + _dot(tril, h3) for h1, h2, h3 in splits]
    scaled = []
    for g in groups:
        b = bs[g]
        b_last = b[CHUNK - 1:CHUNK, :]
        rk = 1.0 - fs[g]
        v = i_ref[g]
        scaled.append(((q_ref[g] * jnp.exp(b)).astype(BF16), (rk * jnp.exp(-b)).astype(BF16),
                       (rk * jnp.exp(b_last - b)).astype(BF16), v, v.astype(BF16), jnp.exp(b_last)))
    first = []
    for g in groups:
        qd_b, kd_b, kl_b, v, v_b, eb_last = scaled[g]
        states = [st_ref[g, h] for h in range(HEADS)]
        a_all = [_dot_nt(qd_b[:, sl], kd_b[:, sl]) for sl in heads]
        o_state = [_dot_nt(qd_b[:, sl], st.astype(BF16)) for sl, st in zip(heads, states)]
        s_upd = [_dot(v[:, sl].T.astype(BF16), kl_b[:, sl]) for sl in heads]
        first.append((states, a_all, o_state, s_upd))
    sample_duos(1)
    for g in groups:
        v_b, eb_last = scaled[g][4], scaled[g][5]
        states, a_all, o_state, s_upd = first[g]
        for h, sl in enumerate(heads):
            a = jnp.where(causal, a_all[h], 0.0).astype(BF16)
            o_ref[g, :, sl] = o_state[h] + _dot(a, v_b[:, sl])
            st_ref[g, h] = states[h] * eb_last[:, sl] + s_upd[h]

    @_when_step(c, n_chunks, n_chunks - 1)
    def _():
        for g in range(n_group):
            for h in range(HEADS):
                sout_ref[g, h] = st_ref[g, h].T


def _hgrn(z, s0, n_batch, n_chunks, n_group, sample=None):
    z3 = z.reshape(n_batch, n_chunks * CHUNK, N_Z_BLOCKS * D_MODEL)
    n_steps = (n_batch // n_group) * n_chunks

    def zblock(zc):
        return pl.BlockSpec((n_group, CHUNK, D_MODEL), lambda b, c: (b, c, zc))

    state = pl.BlockSpec((n_group, HEADS, HEAD_W, HEAD_W), lambda b, c: (b, 0, 0, 0))
    in_specs = [zblock(ZC_F), zblock(ZC_RQ), zblock(ZC_RI), state]
    operands = [z3, z3, z3, s0]
    out_specs = [pl.BlockSpec((n_group, CHUNK, D_MODEL), lambda b, c: (b, c, 0)), state]
    out_shape = [
        jax.ShapeDtypeStruct((n_batch, n_chunks * CHUNK, D_MODEL), F32),
        jax.ShapeDtypeStruct((n_batch, HEADS, HEAD_W, HEAD_W), F32),
    ]
    scratch = [pltpu.VMEM((n_group, HEADS, HEAD_W, HEAD_W), F32)]
    side = None
    if sample is not None:
        lam, zs, kb_s, vb_s, cache_k, cache_v, n_sb, n_q = sample
        n_pairs = n_sb * HEADS
        pps = n_pairs // n_steps
        assert pps * n_steps == n_pairs and pps % SAMPLE_CACHE_SLOTS == 0 and HEADS % pps == 0
        col_groups = HEADS // pps
        past = cache_k.shape[1]
        side = (pps, n_pairs)

        def srow(b, c):
            return (b * n_chunks + c) // col_groups

        def scol(b, c):
            return (b * n_chunks + c) % col_groups

        in_specs += [
            pl.BlockSpec(memory_space=pltpu.SMEM),
            pl.BlockSpec((n_q, pps * HEAD_W), lambda b, c: (srow(b, c), ZC_Q * col_groups + scol(b, c))),
            pl.BlockSpec((n_q, pps * HEAD_W), lambda b, c: (srow(b, c), scol(b, c))),
            pl.BlockSpec((n_q, pps * HEAD_W), lambda b, c: (srow(b, c), scol(b, c))),
            pl.BlockSpec(memory_space=pl.ANY), pl.BlockSpec(memory_space=pl.ANY),
        ]
        operands += [lam, zs, kb_s, vb_s, cache_k, cache_v]
        out_specs.append(pl.BlockSpec((n_q, pps * HEAD_W), lambda b, c: (srow(b, c), scol(b, c))))
        out_shape.append(jax.ShapeDtypeStruct((n_sb * n_q, D_MODEL), F32))
        scratch += [pltpu.VMEM((SAMPLE_CACHE_SLOTS, past, HEAD_W), F32),
                    pltpu.VMEM((SAMPLE_CACHE_SLOTS, past, HEAD_W), F32),
                    pltpu.SemaphoreType.DMA((2, SAMPLE_CACHE_SLOTS))]
    outs = pl.pallas_call(
        functools.partial(_hgrn_kernel, n_chunks, side),
        grid=(n_batch // n_group, n_chunks),
        in_specs=in_specs,
        out_specs=out_specs,
        out_shape=out_shape,
        scratch_shapes=scratch,
        compiler_params=_cparams("arbitrary", "arbitrary"),
        name="hgrn",
    )(*operands)
    return (outs[0].reshape(n_batch * n_chunks * CHUNK, D_MODEL),) + tuple(outs[1:])


def _head_rmsnorm(x, g):
    outs = []
    for h in range(HEADS):
        xh = x[:, h * HEAD_W:(h + 1) * HEAD_W]
        ms = jnp.mean(xh * xh, axis=-1, keepdims=True)
        outs.append(xh * lax.rsqrt(ms + NORM_EPS) * g)
    return jnp.concatenate(outs, axis=-1)


def _merge_kernel(lam_init, ao_ref, ro_ref, rg_ref, ga_ref, gr_ref, x_ref, gs_ref, grn_ref,
                  woa_ref, wor_ref, wout_ref, gffn_ref, wr_ref, br_ref, cnt_in_ref,
                  x1_ref, xn_ref, idx_ref, prob_ref, rank_ref, cnt_ref, run_ref):
    i = pl.program_id(0)
    tm = x_ref.shape[0]

    @pl.when(i == 0)
    def _():
        run_ref[...] = cnt_in_ref[...]

    part = tm // MERGE_PARTS

    def rows(p):
        return slice(p * part, (p + 1) * part)

    def norm_inputs(p, _):
        an = (_head_rmsnorm(ao_ref[rows(p), :], gs_ref[...]) * (1.0 - lam_init)).astype(BF16)
        rn = (_head_rmsnorm(ro_ref[rows(p), :], grn_ref[...]) * jax.nn.silu(rg_ref[rows(p), :])).astype(BF16)
        return an, rn

    def branch_projections(p, st):
        an, rn = st
        return _dot(an, woa_ref[...]), _dot(rn, wor_ref[...])

    def gate(p, st):
        a, r = st
        return (jax.nn.sigmoid(ga_ref[rows(p), :]) * a + jax.nn.sigmoid(gr_ref[rows(p), :]) * r).astype(BF16)

    def residual(p, mixed):
        x1 = x_ref[rows(p), :] + _dot(mixed, wout_ref[...])
        x1_ref[rows(p), :] = x1
        return x1

    def ffn_norm_router(p, x1):
        ms = jnp.mean(x1 * x1, axis=-1, keepdims=True)
        xn = x1 * lax.rsqrt(ms + NORM_EPS) * gffn_ref[...]
        for s in range(ROW_SUB):
            xn_ref[pl.ds(p * part * ROW_SUB + s, part, stride=ROW_SUB), :] = xn[:, s * HEAD_W:(s + 1) * HEAD_W]
        return _dot(xn.astype(BF16), wr_ref[...])

    stages = (norm_inputs, branch_projections, gate, residual, ffn_norm_router)
    state = [None] * MERGE_PARTS
    for stage in stages:
        for p in range(MERGE_PARTS):
            state[p] = stage(p, state[p])
    logits = jnp.concatenate(state, axis=0) + br_ref[...]
    lane = lax.broadcasted_iota(jnp.int32, logits.shape, 1).astype(F32)
    sel = jnp.zeros(logits.shape, F32)
    work = logits
    tops, idxs = [], []
    for _ in range(TOP_K):
        m = jnp.max(work, axis=-1, keepdims=True)
        idx = jnp.min(jnp.where(work == m, lane, float(N_EXPERTS)), axis=-1, keepdims=True)
        hit = lane == idx
        sel = jnp.where(hit, 1.0, sel)
        work = jnp.where(hit, -jnp.inf, work)
        tops.append(m)
        idxs.append(idx)
    es = [jnp.exp(v - tops[0]) for v in tops]
    inv = 1.0 / (es[0] + es[1] + es[2] + es[3])

    trow = lax.broadcasted_iota(jnp.int32, (tm, tm), 0)
    tcol = lax.broadcasted_iota(jnp.int32, (tm, tm), 1)
    before = jnp.where(trow > tcol, 1.0, 0.0).astype(BF16)
    ranks = _dot(before, sel.astype(BF16)) + run_ref[...]
    k_lane = lax.broadcasted_iota(jnp.int32, (tm, TOP_K), 1)
    idx_out = jnp.zeros((tm, TOP_K), F32)
    prob_out = jnp.zeros((tm, TOP_K), F32)
    rank_out = jnp.zeros((tm, TOP_K), F32)
    for k in range(TOP_K):
        rk = jnp.sum(jnp.where(lane == idxs[k], ranks, 0.0), axis=-1, keepdims=True)
        idx_out = jnp.where(k_lane == k, idxs[k], idx_out)
        prob_out = jnp.where(k_lane == k, es[k] * inv, prob_out)
        rank_out = jnp.where(k_lane == k, rk, rank_out)
    idx_ref[...] = idx_out.astype(jnp.int32)
    prob_ref[...] = prob_out
    rank_ref[...] = rank_out.astype(jnp.int32)
    run = run_ref[...] + jnp.sum(sel, axis=0, keepdims=True)
    run_ref[...] = run
    cnt_ref[...] = run


def _merge(lam_init, ao, ro, z, x, g_subln, g_rec, woa, wor, wout, g_ffn, w_router, b_router, cnt_in):
    t = x.shape[0]
    tm = MERGE_TM
    row = lambda i: (i, 0)
    fixed = lambda i: (0, 0)
    tok = pl.BlockSpec((tm, D_MODEL), row)
    wspec = pl.BlockSpec((D_MODEL, D_MODEL), fixed)
    narrow = pl.BlockSpec((tm, TOP_K), row)
    return pl.pallas_call(
        functools.partial(_merge_kernel, lam_init),
        grid=(t // tm,),
        in_specs=[
            tok, tok,
            pl.BlockSpec((tm, D_MODEL), lambda i: (i, ZC_RG)),
            pl.BlockSpec((tm, D_MODEL), lambda i: (i, ZC_GA)),
            pl.BlockSpec((tm, D_MODEL), lambda i: (i, ZC_GR)),
            tok,
            pl.BlockSpec((1, HEAD_W), fixed), pl.BlockSpec((1, HEAD_W), fixed),
            wspec, wspec, wspec,
            pl.BlockSpec((1, D_MODEL), fixed),
            pl.BlockSpec((D_MODEL, N_EXPERTS), fixed),
            pl.BlockSpec((1, N_EXPERTS), fixed),
            pl.BlockSpec((1, N_EXPERTS), fixed),
        ],
        out_specs=[tok, pl.BlockSpec((tm * ROW_SUB, HEAD_W), row), narrow, narrow, narrow,
                   pl.BlockSpec((1, N_EXPERTS), fixed)],
        out_shape=[
            jax.ShapeDtypeStruct((t, D_MODEL), F32),
            jax.ShapeDtypeStruct((t * ROW_SUB, HEAD_W), F32),
            jax.ShapeDtypeStruct((t, TOP_K), jnp.int32),
            jax.ShapeDtypeStruct((t, TOP_K), F32),
            jax.ShapeDtypeStruct((t, TOP_K), jnp.int32),
            jax.ShapeDtypeStruct((1, N_EXPERTS), F32),
        ],
        scratch_shapes=[pltpu.VMEM((1, N_EXPERTS), F32)],
        compiler_params=_cparams("arbitrary"),
        name="merge",
    )(ao, ro, z, z, z, x, g_subln, g_rec, woa, wor, wout, g_ffn, w_router, b_router, cnt_in)


def _row_copy(src, dst, sem):
    return pltpu.make_async_copy(src, dst, sem)


def _dispatch_kernel(n_first, n_steps, pos_ref, xa_hbm, xb_hbm, xs_hbm, stage, load_sem, copy_sem):
    i = pl.program_id(0)
    tm = ROW_TM
    rows = tm * ROW_SUB
    slot = i % DISPATCH_SLOTS

    def row_tile(off):
        return pl.ds(pl.multiple_of(off, ROW_SUB), ROW_SUB)

    def load(t, sl):
        @pl.when(t < n_first)
        def _():
            _row_copy(xa_hbm.at[pl.ds(pl.multiple_of(t * rows, rows), rows)], stage.at[sl], load_sem.at[sl]).start()

        @pl.when(t >= n_first)
        def _():
            _row_copy(xb_hbm.at[pl.ds(pl.multiple_of((t - n_first) * rows, rows), rows)], stage.at[sl],
                      load_sem.at[sl]).start()

    def drain(parity):
        for k in range(TOP_K):
            _row_copy(stage.at[0], xs_hbm.at[pl.ds(0, rows)], copy_sem.at[parity]).wait()

    @pl.when(i == 0)
    def _():
        load(i, slot)

    @pl.when(i + 1 < n_steps)
    def _():
        load(i + 1, (i + 1) % DISPATCH_SLOTS)

    _row_copy(xa_hbm.at[pl.ds(0, rows)], stage.at[slot], load_sem.at[slot]).wait()

    def start(g, carry):
        r0 = g * ROW_COPY_UNROLL
        n = ROW_COPY_UNROLL * TOP_K
        dst = [pos_ref[0, 0, r0 * TOP_K + j] for j in range(n)]
        for j in range(n):
            _row_copy(stage.at[slot, row_tile((r0 + j // TOP_K) * ROW_SUB)],
                      xs_hbm.at[row_tile(dst[j])], copy_sem.at[i % 2]).start()
        return carry

    lax.fori_loop(0, tm // ROW_COPY_UNROLL, start, 0)

    @pl.when(i > 0)
    def _():
        drain((i - 1) % 2)

    @pl.when(i == n_steps - 1)
    def _():
        drain(i % 2)


def _dispatch(pos, xn_a, xn_b):
    tm = ROW_TM
    n_a, n_b = xn_a.shape[0] // (tm * ROW_SUB), xn_b.shape[0] // (tm * ROW_SUB)
    pos3 = pos.reshape(n_a + n_b, 1, tm * TOP_K)
    return pl.pallas_call(
        functools.partial(_dispatch_kernel, n_a, n_a + n_b),
        grid=(n_a + n_b,),
        in_specs=[
            pl.BlockSpec((1, 1, tm * TOP_K), lambda i: (i, 0, 0), memory_space=pltpu.SMEM),
            pl.BlockSpec(memory_space=pl.ANY),
            pl.BlockSpec(memory_space=pl.ANY),
        ],
        out_specs=pl.BlockSpec(memory_space=pl.ANY),
        out_shape=jax.ShapeDtypeStruct((pos.size * ROW_SUB, HEAD_W), F32),
        scratch_shapes=[
            pltpu.VMEM((DISPATCH_SLOTS, tm * ROW_SUB, HEAD_W), F32),
            pltpu.SemaphoreType.DMA((DISPATCH_SLOTS,)),
            pltpu.SemaphoreType.DMA((2,)),
        ],
        compiler_params=_cparams("arbitrary"),
        name="dispatch",
    )(pos3, xn_a, xn_b)


def _experts_kernel(tile_ref, exp_ref, lo_ref, hi_ref, fresh_ref, par_ref, nxt_ref, more_ref,
                    xs_ref, wg_hbm, bg_ref, wu_hbm, bu_ref, wd_hbm, bd_ref, o_ref,
                    wg_s, wu_s, wd_s, x_s, acc_s, wf_s, wsem):
    w = pl.program_id(0)
    lo = lo_ref[w]
    hi = hi_ref[w]
    tm = x_s.shape[0]

    def lanes(s):
        return slice(s * HEAD_W, (s + 1) * HEAD_W)

    def sub_rows(s):
        return pl.ds(s, tm, stride=ROW_SUB)

    def weight_copies(e, sl):
        return [pltpu.make_async_copy(src.at[e], wf_s.at[sl, j], wsem.at[sl])
                for j, src in enumerate((wg_hbm, wu_hbm, wd_hbm))]

    @pl.when(w == 0)
    def _():
        for c in weight_copies(exp_ref[0], par_ref[0]):
            c.start()
        acc_s[...] = jnp.zeros(acc_s.shape, F32)

    @pl.when(fresh_ref[w] == 1)
    def _():
        sl = par_ref[w]
        for c in weight_copies(exp_ref[w], sl):
            c.wait()

        @pl.when(more_ref[w] == 1)
        def _():
            for c in weight_copies(nxt_ref[w], 1 - sl):
                c.start()

        wg_s[...] = wf_s[sl, 0].astype(BF16)
        wu_s[...] = wf_s[sl, 1].astype(BF16)
        wd_s[...] = wf_s[sl, 2].astype(BF16)

    @pl.when(hi > lo)
    def _():
        for s in range(ROW_SUB):
            x_s[:, lanes(s)] = xs_ref[sub_rows(s), :].astype(BF16)
        x = x_s[...]
        out = bd_ref[0]
        def gate_up(c):
            ff = slice(c, c + MOE_FF_SLICE)
            return _dot(x, wg_s[:, ff]) + bg_ref[0, :, ff], _dot(x, wu_s[:, ff]) + bu_ref[0, :, ff]

        slices = range(0, D_MODEL, MOE_FF_SLICE)
        gus = [gate_up(c) for c in slices]
        acts = []
        for gate, up in gus:
            gate = jnp.minimum(gate, SWIGLU_LIMIT)
            up = jnp.clip(up, -SWIGLU_LIMIT, SWIGLU_LIMIT)
            acts.append((gate * jax.nn.sigmoid(SWIGLU_ALPHA * gate) * (up + 1.0)).astype(BF16))
        for c, act in zip(slices, acts):
            out = out + _dot(act, wd_s[c:c + MOE_FF_SLICE, :])
        row = lax.broadcasted_iota(jnp.int32, (tm, 1), 0)
        merged = jnp.where((row >= lo) & (row < hi), out, acc_s[...])
        acc_s[...] = merged
        for s in range(ROW_SUB):
            o_ref[sub_rows(s), :] = merged[:, lanes(s)]


def _experts(plan, xs, w_gate, b_gate, w_up, b_up, w_down, b_down):
    tm = MOE_TM
    n_items = plan[0].shape[0]
    rows = lambda w, tile, *_: (tile[w], 0)
    wspec = pl.BlockSpec(memory_space=pl.ANY)
    bspec = pl.BlockSpec((1, 1, D_MODEL), lambda w, tile, exp, *_: (exp[w], 0, 0))
    grid_spec = pltpu.PrefetchScalarGridSpec(
        num_scalar_prefetch=len(plan),
        grid=(n_items,),
        in_specs=[pl.BlockSpec((tm * ROW_SUB, HEAD_W), rows), wspec, bspec, wspec, bspec, wspec, bspec],
        out_specs=pl.BlockSpec((tm * ROW_SUB, HEAD_W), rows),
        scratch_shapes=[pltpu.VMEM((D_MODEL, D_MODEL), BF16)] * 3 + [
            pltpu.VMEM((tm, D_MODEL), BF16),
            pltpu.VMEM((tm, D_MODEL), F32),
            pltpu.VMEM((2, 3, D_MODEL, D_MODEL), F32),
            pltpu.SemaphoreType.DMA((2,)),
        ],
    )
    return pl.pallas_call(
        _experts_kernel,
        grid_spec=grid_spec,
        out_shape=jax.ShapeDtypeStruct(xs.shape, F32),
        compiler_params=_cparams("arbitrary"),
        name="experts",
    )(*plan, xs, w_gate, b_gate.reshape(N_EXPERTS, 1, D_MODEL), w_up, b_up.reshape(N_EXPERTS, 1, D_MODEL),
      w_down, b_down.reshape(N_EXPERTS, 1, D_MODEL))


def _expert_plan(counts, n_rows):
    tm = MOE_TM
    n_tiles = n_rows // tm
    n_items = n_tiles + N_EXPERTS - 1
    ends = jnp.cumsum(counts)
    starts = ends - counts
    first_tile = starts // tm
    last_tile = jnp.maximum(ends - 1, 0) // tm
    items = jnp.where(counts > 0, last_tile - first_tile + 1, 0)
    item_end = jnp.cumsum(items)
    item_start = item_end - items
    total = item_end[-1]
    w = jnp.arange(n_items, dtype=jnp.int32)
    wc = jnp.minimum(w, total - 1)
    exp = jnp.sum((item_end[None, :] <= wc[:, None]).astype(jnp.int32), axis=1)
    ids = jnp.arange(N_EXPERTS, dtype=jnp.int32)
    mine = exp[:, None] == ids[None, :]

    def of_item(per_expert):
        return jnp.sum(jnp.where(mine, per_expert[None, :], 0), axis=1).astype(jnp.int32)

    tile = of_item(first_tile) + wc - of_item(item_start)
    lo = jnp.maximum(of_item(starts), tile * tm) - tile * tm
    hi = jnp.minimum(of_item(ends), (tile + 1) * tm) - tile * tm
    valid = w < total
    lo = jnp.where(valid, lo, 0).astype(jnp.int32)
    hi = jnp.where(valid, hi, 0).astype(jnp.int32)
    prev_exp = jnp.concatenate([jnp.full((1,), -1, jnp.int32), exp[:-1]])
    fresh = (valid & (exp != prev_exp)).astype(jnp.int32)
    later = jnp.where((counts > 0)[None, :] & (ids[None, :] > ids[:, None]), ids[None, :], N_EXPERTS)
    next_exp = jnp.min(later, axis=1)
    parity = ((jnp.cumsum((counts > 0).astype(jnp.int32)) - 1) % 2).astype(jnp.int32)
    next_of_item = of_item(next_exp)
    more = (next_of_item < N_EXPERTS).astype(jnp.int32)
    nxt = jnp.where(more == 1, next_of_item, exp).astype(jnp.int32)
    return tile, exp, lo, hi, fresh, of_item(parity), nxt, more


def _combine_kernel(n_steps, pos_ref, pos_next_ref, x1_ref, prob_ref, gfin_ref, ys_ref, y_ref, buf_ref, sem):
    i = pl.program_id(0)
    tm = x1_ref.shape[0]
    slot = i % 2

    def row_tile(off):
        return pl.ds(pl.multiple_of(off, ROW_SUB), ROW_SUB)

    def gather(p_ref, sl):
        def start(g, carry):
            r0 = g * ROW_COPY_UNROLL
            n = ROW_COPY_UNROLL * TOP_K
            src = [p_ref[0, 0, r0 * TOP_K + j] for j in range(n)]
            for j in range(n):
                _row_copy(ys_ref.at[row_tile(src[j])],
                          buf_ref.at[sl, j % TOP_K, row_tile((r0 + j // TOP_K) * ROW_SUB)], sem.at[sl]).start()
            return carry

        lax.fori_loop(0, tm // ROW_COPY_UNROLL, start, 0)

    @pl.when(i == 0)
    def _():
        gather(pos_ref, slot)

    @pl.when(i + 1 < n_steps)
    def _():
        gather(pos_next_ref, 1 - slot)

    for k in range(TOP_K):
        _row_copy(ys_ref.at[pl.ds(0, tm * ROW_SUB)], buf_ref.at[slot, k], sem.at[slot]).wait()

    prob = prob_ref[...]
    parts = []
    for s in range(ROW_SUB):
        part = x1_ref[:, s * HEAD_W:(s + 1) * HEAD_W]
        for k in range(TOP_K):
            part = part + prob[:, k:k + 1] * buf_ref[slot, k, pl.ds(s, tm, stride=ROW_SUB), :]
        parts.append(part)
    x2 = jnp.concatenate(parts, axis=-1)
    ms = jnp.mean(x2 * x2, axis=-1, keepdims=True)
    y_ref[...] = x2 * lax.rsqrt(ms + NORM_EPS) * gfin_ref[...]


def _combine(pos, x1, prob, g_final, ys):
    t = x1.shape[0]
    tm = ROW_TM
    n = t // tm
    pos3 = pos.reshape(n, 1, tm * TOP_K)
    return pl.pallas_call(
        functools.partial(_combine_kernel, n),
        grid=(n,),
        in_specs=[
            pl.BlockSpec((1, 1, tm * TOP_K), lambda i: (i, 0, 0), memory_space=pltpu.SMEM),
            pl.BlockSpec((1, 1, tm * TOP_K), lambda i: (jnp.minimum(i + 1, n - 1), 0, 0), memory_space=pltpu.SMEM),
            pl.BlockSpec((tm, D_MODEL), lambda i: (i, 0)),
            pl.BlockSpec((tm, TOP_K), lambda i: (i, 0)),
            pl.BlockSpec((1, D_MODEL), lambda i: (0, 0)),
            pl.BlockSpec(memory_space=pl.ANY),
        ],
        out_specs=pl.BlockSpec((tm, D_MODEL), lambda i: (i, 0)),
        out_shape=jax.ShapeDtypeStruct((t, D_MODEL), F32),
        scratch_shapes=[pltpu.VMEM((2, TOP_K, tm * ROW_SUB, HEAD_W), F32), pltpu.SemaphoreType.DMA((2,))],
        compiler_params=_cparams("arbitrary"),
        name="combine",
    )(pos3, pos3, x1, prob, g_final, ys)


def _lambda_init(layer):
    return 0.8 - 0.6 * math.exp(-0.3 * layer)


def kernel(x_prompt, x_sample, cache_k, cache_v, state_rec, g_mix, w_in, lambda_q1, lambda_k1, lambda_q2, lambda_k2, g_subln, w_o_attn, lb_logits, g_rec_norm, w_o_rec, w_out, g_ffn, w_router, b_router, w_gate, b_gate, w_up, b_up, w_down, b_down, g_final):
    nb, seq, _ = x_prompt.shape
    nbd, ld, _ = x_sample.shape
    past = cache_k.shape[2]
    tp, ts = nb * seq, nbd * ld
    lam_init = _lambda_init(0)
    lam = (jnp.exp(jnp.sum(lambda_q1[0].astype(F32) * lambda_k1[0].astype(F32)))
           - jnp.exp(jnp.sum(lambda_q2[0].astype(F32) * lambda_k2[0].astype(F32))) + lam_init).reshape(1)

    w_in_b = w_in[0].astype(BF16)
    woa, wor, wout = w_o_attn[0].astype(BF16), w_o_rec[0].astype(BF16), w_out[0].astype(BF16)
    wr = w_router[0].astype(BF16)
    g_mix2, g_ffn2, g_fin2 = g_mix[0].reshape(1, -1), g_ffn[0].reshape(1, -1), g_final.reshape(1, -1)
    gs2, grn2 = g_subln[0].reshape(1, -1), g_rec_norm[0].reshape(1, -1)
    br2 = b_router[0].reshape(1, -1)

    cos_p, sin_p = _rope_tables(jnp.arange(seq, dtype=jnp.int32))
    pos_s = past + (jnp.arange(PROJ_TM, dtype=jnp.int32) % ld)
    cos_s, sin_s = _rope_tables(pos_s)

    xp = x_prompt.reshape(tp, D_MODEL)
    xs_in = x_sample.reshape(ts, D_MODEL)
    zp, k_p, v_p, kb_p, vb_p = _project(xp, g_mix2, w_in_b, cos_p, sin_p, lb_logits)
    zs, k_s, v_s, kb_s, vb_s = _project(xs_in, g_mix2, w_in_b, cos_s, sin_s, lb_logits)

    ao_p = _attn_prompt(lam, zp, kb_p, vb_p, nb, seq)
    ro_p, st_p, ao_s = _hgrn(zp, jnp.zeros((nb, HEADS, HEAD_W, HEAD_W), F32), nb, seq // CHUNK, HGRN_GROUP_PROMPT,
                             sample=(lam, zs, kb_s, vb_s, cache_k[0], cache_v[0], nbd, ld))
    ro_s, st_s = _hgrn(zs, state_rec[0], nbd, ld // CHUNK, HGRN_GROUP_SAMPLE)

    merge_w = (gs2, grn2, woa, wor, wout, g_ffn2, wr, br2)
    zero_cnt = jnp.zeros((1, N_EXPERTS), F32)
    x1_p, xn_p, idx_p, prob_p, rank_p, cnt_p = _merge(lam_init, ao_p, ro_p, zp, xp, *merge_w, zero_cnt)
    x1_s, xn_s, idx_s, prob_s, rank_s, cnt = _merge(lam_init, ao_s, ro_s, zs, xs_in, *merge_w, cnt_p)

    counts = cnt[0].astype(jnp.int32)
    starts = jnp.cumsum(counts) - counts
    experts = jnp.arange(N_EXPERTS, dtype=jnp.int32)

    def positions(idx, rank):
        return (jnp.sum(jnp.where(idx[..., None] == experts, starts, 0), axis=-1) + rank) * ROW_SUB

    pos_p = positions(idx_p, rank_p)
    pos_s2 = positions(idx_s, rank_s)
    n_rows = (tp + ts) * TOP_K
    plan = _expert_plan(counts, n_rows)

    xs_sorted = _dispatch(jnp.concatenate([pos_p, pos_s2], axis=0), xn_p, xn_s)
    ys = _experts(plan, xs_sorted, w_gate[0], b_gate[0], w_up[0], b_up[0], w_down[0], b_down[0])
    y_p = _combine(pos_p, x1_p, prob_p, g_fin2, ys)
    y_s = _combine(pos_s2, x1_s, prob_s, g_fin2, ys)

    return (y_p.reshape(nb, seq, D_MODEL), y_s.reshape(nbd, ld, D_MODEL),
            k_p.reshape(1, nb, seq, HEADS, HEAD_W), v_p.reshape(1, nb, seq, HEADS, HEAD_W), st_p[None],
            k_s.reshape(1, nbd, ld, HEADS, HEAD_W), v_s.reshape(1, nbd, ld, HEADS, HEAD_W), st_s[None])
```

```python
import functools
import math

import jax
import jax.numpy as jnp
from jax import lax
from jax.experimental import pallas as pl
from jax.experimental.pallas import tpu as pltpu

F32 = jnp.float32
BF16 = jnp.bfloat16

D_MODEL = 1024
CHUNK = 64
HEADS = 8
HEAD_W = 128
ATTN_HEAD_DIM = 64
ROPE_THETA = 10000.0
Q_SCALE = ATTN_HEAD_DIM ** -0.5 * math.log2(math.e)
N_EXPERTS = 32
TOP_K = 4
SWIGLU_ALPHA = 1.702
SWIGLU_LIMIT = 7.0
NORM_EPS = 1e-6
N_COL_BLOCKS = 9
COL_Q, COL_K, COL_V, COL_F = range(4)
N_Z_BLOCKS = 7
ZC_Q, ZC_F, ZC_RQ, ZC_RI, ZC_RG, ZC_GA, ZC_GR = range(7)
PROJ_TM = 256

ATTN_TQ = 256
ATTN_TILE_GROUP = 4
CACHE_COPY_PARTS = 4
SAMPLE_CACHE_SLOTS = 4
HGRN_GROUP_PROMPT = 4
HGRN_GROUP_SAMPLE = 8
MERGE_TM = 512
MERGE_PARTS = 2
MOE_TM = 256
MOE_FF_SLICE = 512
ROW_TM = 512
ROW_COPY_UNROLL = 4
DISPATCH_SLOTS = 3
ROW_SUB = D_MODEL // HEAD_W
VMEM_LIMIT = 56 * 1024 * 1024


def _cparams(*sem):
    return pltpu.CompilerParams(dimension_semantics=sem, vmem_limit_bytes=VMEM_LIMIT)


def _dot(a, b):
    return jnp.dot(a, b, preferred_element_type=F32)


def _dot_nt(a, b):
    return lax.dot_general(a, b, (((1,), (1,)), ((), ())), preferred_element_type=F32)


def _proj_kernel(x_ref, g_ref, w_ref, cos_ref, sin_ref, lbl_ref,
                 z_ref, kout_ref, vout_ref, kb_ref, vb_ref):
    tm = x_ref.shape[0]
    x = x_ref[...]
    ms = jnp.mean(x * x, axis=-1, keepdims=True)
    hn = (x * lax.rsqrt(ms + NORM_EPS) * g_ref[...]).astype(BF16)

    def col(j):
        return _dot(hn, w_ref[:, j * D_MODEL:(j + 1) * D_MODEL])

    def zcols(c):
        return slice(c * D_MODEL, (c + 1) * D_MODEL)

    def head(h):
        return slice(h * HEAD_W, (h + 1) * HEAD_W)

    def head_rows(h):
        return pl.ds(h, tm, stride=HEADS)

    def rope(zh):
        lane = lax.broadcasted_iota(jnp.int32, zh.shape, 1)
        first = (lane % ATTN_HEAD_DIM) < (ATTN_HEAD_DIM // 2)
        partner = jnp.where(first, pltpu.roll(zh, HEAD_W - ATTN_HEAD_DIM // 2, 1),
                            pltpu.roll(zh, ATTN_HEAD_DIM // 2, 1))
        return zh * cos_ref[...] + partner * sin_ref[...]

    zq = col(COL_Q)
    for h in range(HEADS):
        z_ref[:, head(h)] = rope(zq[:, head(h)]) * Q_SCALE

    zk = col(COL_K)
    for h in range(HEADS):
        kh = rope(zk[:, head(h)])
        kout_ref[head_rows(h), :] = kh
        kb_ref[:, head(h)] = kh.astype(BF16)

    zv = col(COL_V)
    for h in range(HEADS):
        vout_ref[head_rows(h), :] = zv[:, head(h)]
    vb_ref[...] = zv.astype(BF16)

    lbl = lbl_ref[...]
    e = jnp.exp(lbl - jnp.max(lbl, axis=0, keepdims=True))
    lb = e[0:1] / jnp.sum(e, axis=0, keepdims=True)
    z_ref[:, zcols(ZC_F)] = lb + (1.0 - lb) * jax.nn.sigmoid(col(COL_F))

    for j in range(COL_F + 1, N_COL_BLOCKS):
        z_ref[:, zcols(j - COL_V)] = col(j)


def _project(x, g, w_bf16, cos_t, sin_t, lb_logits):
    t = x.shape[0]
    tm = PROJ_TM
    n_tab = cos_t.shape[0] // tm
    tok = lambda i: (i, 0)
    fixed = lambda i: (0, 0)
    return pl.pallas_call(
        _proj_kernel,
        grid=(t // tm,),
        in_specs=[
            pl.BlockSpec((tm, D_MODEL), tok),
            pl.BlockSpec((1, D_MODEL), fixed),
            pl.BlockSpec((D_MODEL, N_COL_BLOCKS * D_MODEL), fixed, pipeline_mode=pl.Buffered(1)),
            pl.BlockSpec((tm, HEAD_W), lambda i: (i % n_tab, 0)),
            pl.BlockSpec((tm, HEAD_W), lambda i: (i % n_tab, 0)),
            pl.BlockSpec((2, D_MODEL), fixed),
        ],
        out_specs=[
            pl.BlockSpec((tm, N_Z_BLOCKS * D_MODEL), tok),
            pl.BlockSpec((tm * HEADS, HEAD_W), tok),
            pl.BlockSpec((tm * HEADS, HEAD_W), tok),
            pl.BlockSpec((tm, D_MODEL), tok),
            pl.BlockSpec((tm, D_MODEL), tok),
        ],
        out_shape=[
            jax.ShapeDtypeStruct((t, N_Z_BLOCKS * D_MODEL), F32),
            jax.ShapeDtypeStruct((t * HEADS, HEAD_W), F32),
            jax.ShapeDtypeStruct((t * HEADS, HEAD_W), F32),
            jax.ShapeDtypeStruct((t, D_MODEL), BF16),
            jax.ShapeDtypeStruct((t, D_MODEL), BF16),
        ],
        compiler_params=_cparams("arbitrary"),
        name="proj",
    )(x, g, w_bf16, cos_t, sin_t, lb_logits)


def _rope_tables(pos):
    half = ATTN_HEAD_DIM // 2
    inv = jnp.power(ROPE_THETA, -2.0 * jnp.arange(half, dtype=F32) / ATTN_HEAD_DIM)
    ang = pos.astype(F32)[:, None] * inv[None, :]
    cos = jnp.tile(jnp.cos(ang), (1, HEAD_W // half))
    sin = jnp.sin(ang)
    sin = jnp.tile(jnp.concatenate([-sin, sin], axis=-1), (1, HEAD_W // ATTN_HEAD_DIM))
    return cos, sin


def _split_components(q):
    lane = lax.broadcasted_iota(jnp.int32, q.shape, 1)
    q1 = jnp.where(lane < ATTN_HEAD_DIM, q, 0.0)
    q2 = jnp.where(lane >= ATTN_HEAD_DIM, q, 0.0)
    return jnp.concatenate([q1, q2], axis=0).astype(BF16)


def _attn_prompt_kernel(lam_ref, q_ref, k_ref, v_ref, o_ref):
    lam = lam_ref[0]
    tq = ATTN_TQ
    seq = k_ref.shape[0]
    k = k_ref[...]
    v = v_ref[...]
    row = lax.broadcasted_iota(jnp.int32, (2 * tq, tq), 0)
    col = lax.broadcasted_iota(jnp.int32, (2 * tq, tq), 1)
    visible = (col // CHUNK) <= ((row % tq) // CHUNK)
    n_tiles = seq // tq

    def scores(qi):
        lo = qi * tq
        qq = _split_components(q_ref[lo:lo + tq, :])
        s_d = jnp.where(visible, _dot_nt(qq, k[lo:lo + tq]), -jnp.inf)
        return s_d, (_dot_nt(qq, k[:lo]) if qi > 0 else None)

    def weighted_values(qi, w_d, w_m):
        lo = qi * tq
        o = _dot(w_d, v[lo:lo + tq])
        if qi > 0:
            o = o + _dot(w_m, v[:lo])
        o_ref[lo:lo + tq, :] = o

    def softmax_weights(qi, s_d, s_m):
        m = jnp.max(s_d, axis=-1, keepdims=True)
        if qi > 0:
            m = jnp.maximum(m, jnp.max(s_m, axis=-1, keepdims=True))
            p_m = jnp.exp2(s_m - m)
        p_d = jnp.exp2(s_d - m)
        l = jnp.sum(p_d, axis=-1, keepdims=True)
        if qi > 0:
            l = l + jnp.sum(p_m, axis=-1, keepdims=True)
        r = 1.0 / l
        r1 = r[:tq]
        r2 = lam * r[tq:]
        w_d = (p_d[:tq] * r1 - p_d[tq:] * r2).astype(BF16)
        w_m = (p_m[:tq] * r1 - p_m[tq:] * r2).astype(BF16) if qi > 0 else None
        return w_d, w_m

    group = ATTN_TILE_GROUP
    ahead = [scores(qi) for qi in range(group)]
    for base in range(0, n_tiles, group):
        current = ahead
        if base + group < n_tiles:
            ahead = [scores(base + group + t) for t in range(group)]
        weights = [softmax_weights(base + t, *current[t]) for t in range(group)]
        for t in range(group):
            weighted_values(base + t, *weights[t])


def _attn_prompt(lam, z, kb, vb, n_batch, seq):
    return pl.pallas_call(
        _attn_prompt_kernel,
        grid=(n_batch, HEADS),
        in_specs=[
            pl.BlockSpec(memory_space=pltpu.SMEM),
            pl.BlockSpec((seq, HEAD_W), lambda b, h: (b, ZC_Q * HEADS + h)),
            pl.BlockSpec((seq, HEAD_W), lambda b, h: (b, h)),
            pl.BlockSpec((seq, HEAD_W), lambda b, h: (b, h)),
        ],
        out_specs=pl.BlockSpec((seq, HEAD_W), lambda b, h: (b, h)),
        out_shape=jax.ShapeDtypeStruct((n_batch * seq, D_MODEL), F32),
        compiler_params=_cparams("arbitrary", "arbitrary"),
        name="attn_prompt",
    )(lam, z, kb, vb)


def _sample_attention(lam, qs, kns, vns, ck_refs, cv_refs):
    n_q = qs[0].shape[0]
    qqs = [_split_components(q) for q in qs]
    scores = [(_dot_nt(qq, ck[...].astype(BF16)), _dot_nt(qq, kn)) for qq, ck, kn in zip(qqs, ck_refs, kns)]
    weights = []
    for s_c, s_n in scores:
        m = jnp.maximum(jnp.max(s_c, axis=-1, keepdims=True), jnp.max(s_n, axis=-1, keepdims=True))
        p_c = jnp.exp2(s_c - m)
        p_n = jnp.exp2(s_n - m)
        r = 1.0 / (jnp.sum(p_c, axis=-1, keepdims=True) + jnp.sum(p_n, axis=-1, keepdims=True))
        r1 = r[:n_q]
        r2 = lam * r[n_q:]
        weights.append(((p_c[:n_q] * r1 - p_c[n_q:] * r2).astype(BF16), (p_n[:n_q] * r1 - p_n[n_q:] * r2).astype(BF16)))
    return [_dot(w_c, cv[...].astype(BF16)) + _dot(w_n, vn) for (w_c, w_n), cv, vn in zip(weights, cv_refs, vns)]


def _split3(x):
    h1 = x.astype(BF16)
    r1 = x - h1.astype(F32)
    h2 = r1.astype(BF16)
    h3 = (r1 - h2.astype(F32)).astype(BF16)
    return h1, h2, h3


def _when_step(step, n_steps, which):
    def deco(body):
        if n_steps == 1:
            body()
        else:
            pl.when(step == which)(body)
    return deco


def _hgrn_kernel(n_chunks, side, *refs):
    if side is None:
        f_ref, q_ref, i_ref, s0_ref, o_ref, sout_ref, st_ref = refs
    else:
        (f_ref, q_ref, i_ref, s0_ref, lam_ref, qs_ref, kns_ref, vns_ref, ck_hbm, cv_hbm,
         o_ref, sout_ref, os_ref, st_ref, kbuf, vbuf, sem) = refs
    c = pl.program_id(1)
    n_group = f_ref.shape[0]

    @_when_step(c, n_chunks, 0)
    def _():
        for g in range(n_group):
            for h in range(HEADS):
                st_ref[g, h] = s0_ref[g, h].T

    if side is not None:
        pairs_per_step, n_pairs = side
        step = pl.program_id(0) * n_chunks + c

        def cache_copies(pair, sl):
            bb, hh = pair // HEADS, pair % HEADS
            part = kbuf.shape[1] // CACHE_COPY_PARTS
            copies = []
            for j in range(CACHE_COPY_PARTS):
                keys = pl.ds(j * part, part)
                copies.append(pltpu.make_async_copy(ck_hbm.at[bb, keys, hh, :], kbuf.at[sl, keys], sem.at[0, sl]))
                copies.append(pltpu.make_async_copy(cv_hbm.at[bb, keys, hh, :], vbuf.at[sl, keys], sem.at[1, sl]))
            return copies

        n_slots = kbuf.shape[0]
        duo = n_slots // 2

        @pl.when(step == 0)
        def _():
            for p in range(duo):
                for cp in cache_copies(p, p):
                    cp.start()

        def sample_duo(d):
            first = step * pairs_per_step + d * duo
            js = [d * duo + t for t in range(duo)]
            slots = [j % n_slots for j in js]

            @pl.when(first + duo < n_pairs)
            def _():
                for t in range(duo):
                    for cp in cache_copies(first + duo + t, (js[t] + duo) % n_slots):
                        cp.start()

            for t in range(duo):
                for cp in cache_copies(first + t, slots[t]):
                    cp.wait()
            lanes = [slice(j * HEAD_W, (j + 1) * HEAD_W) for j in js]
            outs = _sample_attention(lam_ref[0], [qs_ref[:, sl] for sl in lanes], [kns_ref[:, sl] for sl in lanes],
                                     [vns_ref[:, sl] for sl in lanes], [kbuf.at[sl] for sl in slots],
                                     [vbuf.at[sl] for sl in slots])
            for sl, o in zip(lanes, outs):
                os_ref[:, sl] = o

        duos_per_step = pairs_per_step // duo
    else:
        duos_per_step = 0

        def sample_duo(d):
            del d

    def sample_duos(half):
        for d in range(half * duos_per_step // 2, (half + 1) * duos_per_step // 2):
            sample_duo(d)

    row = lax.broadcasted_iota(jnp.int32, (CHUNK, CHUNK), 0)
    col = lax.broadcasted_iota(jnp.int32, (CHUNK, CHUNK), 1)
    causal = row >= col
    tril = jnp.where(causal, 1.0, 0.0).astype(BF16)
    heads = [slice(h * HEAD_W, (h + 1) * HEAD_W) for h in range(HEADS)]
    groups = range(n_group)
    sample_duos(0)
    fs = [f_ref[g] for g in groups]
    splits = [_split3(jnp.log(f)) for f in fs]
    bs = [_dot(tril, h1) + _dot(tril, h2) + _dot(tril, h3) for h1, h2, h3 in splits]
    scaled = []
    for g in groups:
        b = bs[g]
        b_last = b[CHUNK - 1:CHUNK, :]
        rk = 1.0 - fs[g]
        v = i_ref[g]
        scaled.append(((q_ref[g] * jnp.exp(b)).astype(BF16), (rk * jnp.exp(-b)).astype(BF16),
                       (rk * jnp.exp(b_last - b)).astype(BF16), v, v.astype(BF16), jnp.exp(b_last)))
    first = []
    for g in groups:
        qd_b, kd_b, kl_b, v, v_b, eb_last = scaled[g]
        states = [st_ref[g, h] for h in range(HEADS)]
        a_all = [_dot_nt(qd_b[:, sl], kd_b[:, sl]) for sl in heads]
        o_state = [_dot_nt(qd_b[:, sl], st.astype(BF16)) for sl, st in zip(heads, states)]
        s_upd = [_dot(v[:, sl].T.astype(BF16), kl_b[:, sl]) for sl in heads]
        first.append((states, a_all, o_state, s_upd))
    sample_duos(1)
    for g in groups:
        v_b, eb_last = scaled[g][4], scaled[g][5]
        states, a_all, o_state, s_upd = first[g]
        for h, sl in enumerate(heads):
            a = jnp.where(causal, a_all[h], 0.0).astype(BF16)
            o_ref[g, :, sl] = o_state[h] + _dot(a, v_b[:, sl])
            st_ref[g, h] = states[h] * eb_last[:, sl] + s_upd[h]

    @_when_step(c, n_chunks, n_chunks - 1)
    def _():
        for g in range(n_group):
            for h in range(HEADS):
                sout_ref[g, h] = st_ref[g, h].T


def _hgrn(z, s0, n_batch, n_chunks, n_group, sample=None):
    z3 = z.reshape(n_batch, n_chunks * CHUNK, N_Z_BLOCKS * D_MODEL)
    n_steps = (n_batch // n_group) * n_chunks

    def zblock(zc):
        return pl.BlockSpec((n_group, CHUNK, D_MODEL), lambda b, c: (b, c, zc))

    state = pl.BlockSpec((n_group, HEADS, HEAD_W, HEAD_W), lambda b, c: (b, 0, 0, 0))
    in_specs = [zblock(ZC_F), zblock(ZC_RQ), zblock(ZC_RI), state]
    operands = [z3, z3, z3, s0]
    out_specs = [pl.BlockSpec((n_group, CHUNK, D_MODEL), lambda b, c: (b, c, 0)), state]
    out_shape = [
        jax.ShapeDtypeStruct((n_batch, n_chunks * CHUNK, D_MODEL), F32),
        jax.ShapeDtypeStruct((n_batch, HEADS, HEAD_W, HEAD_W), F32),
    ]
    scratch = [pltpu.VMEM((n_group, HEADS, HEAD_W, HEAD_W), F32)]
    side = None
    if sample is not None:
        lam, zs, kb_s, vb_s, cache_k, cache_v, n_sb, n_q = sample
        n_pairs = n_sb * HEADS
        pps = n_pairs // n_steps
        assert pps * n_steps == n_pairs and pps % SAMPLE_CACHE_SLOTS == 0 and HEADS % pps == 0
        col_groups = HEADS // pps
        past = cache_k.shape[1]
        side = (pps, n_pairs)

        def srow(b, c):
            return (b * n_chunks + c) // col_groups

        def scol(b, c):
            return (b * n_chunks + c) % col_groups

        in_specs += [
            pl.BlockSpec(memory_space=pltpu.SMEM),
            pl.BlockSpec((n_q, pps * HEAD_W), lambda b, c: (srow(b, c), ZC_Q * col_groups + scol(b, c))),
            pl.BlockSpec((n_q, pps * HEAD_W), lambda b, c: (srow(b, c), scol(b, c))),
            pl.BlockSpec((n_q, pps * HEAD_W), lambda b, c: (srow(b, c), scol(b, c))),
            pl.BlockSpec(memory_space=pl.ANY), pl.BlockSpec(memory_space=pl.ANY),
        ]
        operands += [lam, zs, kb_s, vb_s, cache_k, cache_v]
        out_specs.append(pl.BlockSpec((n_q, pps * HEAD_W), lambda b, c: (srow(b, c), scol(b, c))))
        out_shape.append(jax.ShapeDtypeStruct((n_sb * n_q, D_MODEL), F32))
        scratch += [pltpu.VMEM((SAMPLE_CACHE_SLOTS, past, HEAD_W), F32),
                    pltpu.VMEM((SAMPLE_CACHE_SLOTS, past, HEAD_W), F32),
                    pltpu.SemaphoreType.DMA((2, SAMPLE_CACHE_SLOTS))]
    outs = pl.pallas_call(
        functools.partial(_hgrn_kernel, n_chunks, side),
        grid=(n_batch // n_group, n_chunks),
        in_specs=in_specs,
        out_specs=out_specs,
        out_shape=out_shape,
        scratch_shapes=scratch,
        compiler_params=_cparams("arbitrary", "arbitrary"),
        name="hgrn",
    )(*operands)
    return (outs[0].reshape(n_batch * n_chunks * CHUNK, D_MODEL),) + tuple(outs[1:])


def _head_rmsnorm(x, g):
    outs = []
    for h in range(HEADS):
        xh = x[:, h * HEAD_W:(h + 1) * HEAD_W]
        ms = jnp.mean(xh * xh, axis=-1, keepdims=True)
        outs.append(xh * lax.rsqrt(ms + NORM_EPS) * g)
    return jnp.concatenate(outs, axis=-1)


def _merge_kernel(lam_init, ao_ref, ro_ref, rg_ref, ga_ref, gr_ref, x_ref, gs_ref, grn_ref,
                  woa_ref, wor_ref, wout_ref, gffn_ref, wr_ref, br_ref, cnt_in_ref,
                  x1_ref, xn_ref, idx_ref, prob_ref, rank_ref, cnt_ref, run_ref):
    i = pl.program_id(0)
    tm = x_ref.shape[0]

    @pl.when(i == 0)
    def _():
        run_ref[...] = cnt_in_ref[...]

    part = tm // MERGE_PARTS

    def rows(p):
        return slice(p * part, (p + 1) * part)

    def norm_inputs(p, _):
        an = (_head_rmsnorm(ao_ref[rows(p), :], gs_ref[...]) * (1.0 - lam_init)).astype(BF16)
        rn = (_head_rmsnorm(ro_ref[rows(p), :], grn_ref[...]) * jax.nn.silu(rg_ref[rows(p), :])).astype(BF16)
        return an, rn

    def branch_projections(p, st):
        an, rn = st
        return _dot(an, woa_ref[...]), _dot(rn, wor_ref[...])

    def gate(p, st):
        a, r = st
        return (jax.nn.sigmoid(ga_ref[rows(p), :]) * a + jax.nn.sigmoid(gr_ref[rows(p), :]) * r).astype(BF16)

    def residual(p, mixed):
        x1 = x_ref[rows(p), :] + _dot(mixed, wout_ref[...])
        x1_ref[rows(p), :] = x1
        return x1

    def ffn_norm_router(p, x1):
        ms = jnp.mean(x1 * x1, axis=-1, keepdims=True)
        xn = x1 * lax.rsqrt(ms + NORM_EPS) * gffn_ref[...]
        for s in range(ROW_SUB):
            xn_ref[pl.ds(p * part * ROW_SUB + s, part, stride=ROW_SUB), :] = xn[:, s * HEAD_W:(s + 1) * HEAD_W]
        return _dot(xn.astype(BF16), wr_ref[...])

    stages = (norm_inputs, branch_projections, gate, residual, ffn_norm_router)
    state = [None] * MERGE_PARTS
    for stage in stages:
        for p in range(MERGE_PARTS):
            state[p] = stage(p, state[p])
    logits = jnp.concatenate(state, axis=0) + br_ref[...]
    lane = lax.broadcasted_iota(jnp.int32, logits.shape, 1).astype(F32)
    sel = jnp.zeros(logits.shape, F32)
    work = logits
    tops, idxs = [], []
    for _ in range(TOP_K):
        m = jnp.max(work, axis=-1, keepdims=True)
        idx = jnp.min(jnp.where(work == m, lane, float(N_EXPERTS)), axis=-1, keepdims=True)
        hit = lane == idx
        sel = jnp.where(hit, 1.0, sel)
        work = jnp.where(hit, -jnp.inf, work)
        tops.append(m)
        idxs.append(idx)
    es = [jnp.exp(v - tops[0]) for v in tops]
    inv = 1.0 / (es[0] + es[1] + es[2] + es[3])

    trow = lax.broadcasted_iota(jnp.int32, (tm, tm), 0)
    tcol = lax.broadcasted_iota(jnp.int32, (tm, tm), 1)
    before = jnp.where(trow > tcol, 1.0, 0.0).astype(BF16)
    ranks = _dot(before, sel.astype(BF16)) + run_ref[...]
    k_lane = lax.broadcasted_iota(jnp.int32, (tm, TOP_K), 1)
    idx_out = jnp.zeros((tm, TOP_K), F32)
    prob_out = jnp.zeros((tm, TOP_K), F32)
    rank_out = jnp.zeros((tm, TOP_K), F32)
    for k in range(TOP_K):
        rk = jnp.sum(jnp.where(lane == idxs[k], ranks, 0.0), axis=-1, keepdims=True)
        idx_out = jnp.where(k_lane == k, idxs[k], idx_out)
        prob_out = jnp.where(k_lane == k, es[k] * inv, prob_out)
        rank_out = jnp.where(k_lane == k, rk, rank_out)
    idx_ref[...] = idx_out.astype(jnp.int32)
    prob_ref[...] = prob_out
    rank_ref[...] = rank_out.astype(jnp.int32)
    run = run_ref[...] + jnp.sum(sel, axis=0, keepdims=True)
    run_ref[...] = run
    cnt_ref[...] = run


def _merge(lam_init, ao, ro, z, x, g_subln, g_rec, woa, wor, wout, g_ffn, w_router, b_router, cnt_in):
    t = x.shape[0]
    tm = MERGE_TM
    row = lambda i: (i, 0)
    fixed = lambda i: (0, 0)
    tok = pl.BlockSpec((tm, D_MODEL), row)
    wspec = pl.BlockSpec((D_MODEL, D_MODEL), fixed)
    narrow = pl.BlockSpec((tm, TOP_K), row)
    return pl.pallas_call(
        functools.partial(_merge_kernel, lam_init),
        grid=(t // tm,),
        in_specs=[
            tok, tok,
            pl.BlockSpec((tm, D_MODEL), lambda i: (i, ZC_RG)),
            pl.BlockSpec((tm, D_MODEL), lambda i: (i, ZC_GA)),
            pl.BlockSpec((tm, D_MODEL), lambda i: (i, ZC_GR)),
            tok,
            pl.BlockSpec((1, HEAD_W), fixed), pl.BlockSpec((1, HEAD_W), fixed),
            wspec, wspec, wspec,
            pl.BlockSpec((1, D_MODEL), fixed),
            pl.BlockSpec((D_MODEL, N_EXPERTS), fixed),
            pl.BlockSpec((1, N_EXPERTS), fixed),
            pl.BlockSpec((1, N_EXPERTS), fixed),
        ],
        out_specs=[tok, pl.BlockSpec((tm * ROW_SUB, HEAD_W), row), narrow, narrow, narrow,
                   pl.BlockSpec((1, N_EXPERTS), fixed)],
        out_shape=[
            jax.ShapeDtypeStruct((t, D_MODEL), F32),
            jax.ShapeDtypeStruct((t * ROW_SUB, HEAD_W), F32),
            jax.ShapeDtypeStruct((t, TOP_K), jnp.int32),
            jax.ShapeDtypeStruct((t, TOP_K), F32),
            jax.ShapeDtypeStruct((t, TOP_K), jnp.int32),
            jax.ShapeDtypeStruct((1, N_EXPERTS), F32),
        ],
        scratch_shapes=[pltpu.VMEM((1, N_EXPERTS), F32)],
        compiler_params=_cparams("arbitrary"),
        name="merge",
    )(ao, ro, z, z, z, x, g_subln, g_rec, woa, wor, wout, g_ffn, w_router, b_router, cnt_in)


def _row_copy(src, dst, sem):
    return pltpu.make_async_copy(src, dst, sem)


def _dispatch_kernel(n_first, n_steps, pos_ref, xa_hbm, xb_hbm, xs_hbm, stage, load_sem, copy_sem):
    i = pl.program_id(0)
    tm = ROW_TM
    rows = tm * ROW_SUB
    slot = i % DISPATCH_SLOTS

    def row_tile(off):
        return pl.ds(pl.multiple_of(off, ROW_SUB), ROW_SUB)

    def load(t, sl):
        @pl.when(t < n_first)
        def _():
            _row_copy(xa_hbm.at[pl.ds(pl.multiple_of(t * rows, rows), rows)], stage.at[sl], load_sem.at[sl]).start()

        @pl.when(t >= n_first)
        def _():
            _row_copy(xb_hbm.at[pl.ds(pl.multiple_of((t - n_first) * rows, rows), rows)], stage.at[sl],
                      load_sem.at[sl]).start()

    def drain(parity):
        for k in range(TOP_K):
            _row_copy(stage.at[0], xs_hbm.at[pl.ds(0, rows)], copy_sem.at[parity]).wait()

    @pl.when(i == 0)
    def _():
        load(i, slot)

    @pl.when(i + 1 < n_steps)
    def _():
        load(i + 1, (i + 1) % DISPATCH_SLOTS)

    _row_copy(xa_hbm.at[pl.ds(0, rows)], stage.at[slot], load_sem.at[slot]).wait()

    def start(g, carry):
        r0 = g * ROW_COPY_UNROLL
        n = ROW_COPY_UNROLL * TOP_K
        dst = [pos_ref[0, 0, r0 * TOP_K + j] for j in range(n)]
        for j in range(n):
            _row_copy(stage.at[slot, row_tile((r0 + j // TOP_K) * ROW_SUB)],
                      xs_hbm.at[row_tile(dst[j])], copy_sem.at[i % 2]).start()
        return carry

    lax.fori_loop(0, tm // ROW_COPY_UNROLL, start, 0)

    @pl.when(i > 0)
    def _():
        drain((i - 1) % 2)

    @pl.when(i == n_steps - 1)
    def _():
        drain(i % 2)


def _dispatch(pos, xn_a, xn_b):
    tm = ROW_TM
    n_a, n_b = xn_a.shape[0] // (tm * ROW_SUB), xn_b.shape[0] // (tm * ROW_SUB)
    pos3 = pos.reshape(n_a + n_b, 1, tm * TOP_K)
    return pl.pallas_call(
        functools.partial(_dispatch_kernel, n_a, n_a + n_b),
        grid=(n_a + n_b,),
        in_specs=[
            pl.BlockSpec((1, 1, tm * TOP_K), lambda i: (i, 0, 0), memory_space=pltpu.SMEM),
            pl.BlockSpec(memory_space=pl.ANY),
            pl.BlockSpec(memory_space=pl.ANY),
        ],
        out_specs=pl.BlockSpec(memory_space=pl.ANY),
        out_shape=jax.ShapeDtypeStruct((pos.size * ROW_SUB, HEAD_W), F32),
        scratch_shapes=[
            pltpu.VMEM((DISPATCH_SLOTS, tm * ROW_SUB, HEAD_W), F32),
            pltpu.SemaphoreType.DMA((DISPATCH_SLOTS,)),
            pltpu.SemaphoreType.DMA((2,)),
        ],
        compiler_params=_cparams("arbitrary"),
        name="dispatch",
    )(pos3, xn_a, xn_b)


def _experts_kernel(tile_ref, exp_ref, lo_ref, hi_ref, fresh_ref, par_ref, nxt_ref, more_ref,
                    xs_ref, wg_hbm, bg_ref, wu_hbm, bu_ref, wd_hbm, bd_ref, o_ref,
                    wg_s, wu_s, wd_s, x_s, acc_s, wf_s, wsem):
    w = pl.program_id(0)
    lo = lo_ref[w]
    hi = hi_ref[w]
    tm = x_s.shape[0]

    def lanes(s):
        return slice(s * HEAD_W, (s + 1) * HEAD_W)

    def sub_rows(s):
        return pl.ds(s, tm, stride=ROW_SUB)

    def weight_copies(e, sl):
        return [pltpu.make_async_copy(src.at[e], wf_s.at[sl, j], wsem.at[sl])
                for j, src in enumerate((wg_hbm, wu_hbm, wd_hbm))]

    @pl.when(w == 0)
    def _():
        for c in weight_copies(exp_ref[0], par_ref[0]):
            c.start()
        acc_s[...] = jnp.zeros(acc_s.shape, F32)

    @pl.when(fresh_ref[w] == 1)
    def _():
        sl = par_ref[w]
        for c in weight_copies(exp_ref[w], sl):
            c.wait()

        @pl.when(more_ref[w] == 1)
        def _():
            for c in weight_copies(nxt_ref[w], 1 - sl):
                c.start()

        wg_s[...] = wf_s[sl, 0].astype(BF16)
        wu_s[...] = wf_s[sl, 1].astype(BF16)
        wd_s[...] = wf_s[sl, 2].astype(BF16)

    @pl.when(hi > lo)
    def _():
        for s in range(ROW_SUB):
            x_s[:, lanes(s)] = xs_ref[sub_rows(s), :].astype(BF16)
        x = x_s[...]
        out = bd_ref[0]
        def gate_up(c):
            ff = slice(c, c + MOE_FF_SLICE)
            return _dot(x, wg_s[:, ff]) + bg_ref[0, :, ff], _dot(x, wu_s[:, ff]) + bu_ref[0, :, ff]

        slices = range(0, D_MODEL, MOE_FF_SLICE)
        gus = [gate_up(c) for c in slices]
        acts = []
        for gate, up in gus:
            gate = jnp.minimum(gate, SWIGLU_LIMIT)
            up = jnp.clip(up, -SWIGLU_LIMIT, SWIGLU_LIMIT)
            acts.append((gate * jax.nn.sigmoid(SWIGLU_ALPHA * gate) * (up + 1.0)).astype(BF16))
        for c, act in zip(slices, acts):
            out = out + _dot(act, wd_s[c:c + MOE_FF_SLICE, :])
        row = lax.broadcasted_iota(jnp.int32, (tm, 1), 0)
        merged = jnp.where((row >= lo) & (row < hi), out, acc_s[...])
        acc_s[...] = merged
        for s in range(ROW_SUB):
            o_ref[sub_rows(s), :] = merged[:, lanes(s)]


def _experts(plan, xs, w_gate, b_gate, w_up, b_up, w_down, b_down):
    tm = MOE_TM
    n_items = plan[0].shape[0]
    rows = lambda w, tile, *_: (tile[w], 0)
    wspec = pl.BlockSpec(memory_space=pl.ANY)
    bspec = pl.BlockSpec((1, 1, D_MODEL), lambda w, tile, exp, *_: (exp[w], 0, 0))
    grid_spec = pltpu.PrefetchScalarGridSpec(
        num_scalar_prefetch=len(plan),
        grid=(n_items,),
        in_specs=[pl.BlockSpec((tm * ROW_SUB, HEAD_W), rows), wspec, bspec, wspec, bspec, wspec, bspec],
        out_specs=pl.BlockSpec((tm * ROW_SUB, HEAD_W), rows),
        scratch_shapes=[pltpu.VMEM((D_MODEL, D_MODEL), BF16)] * 3 + [
            pltpu.VMEM((tm, D_MODEL), BF16),
            pltpu.VMEM((tm, D_MODEL), F32),
            pltpu.VMEM((2, 3, D_MODEL, D_MODEL), F32),
            pltpu.SemaphoreType.DMA((2,)),
        ],
    )
    return pl.pallas_call(
        _experts_kernel,
        grid_spec=grid_spec,
        out_shape=jax.ShapeDtypeStruct(xs.shape, F32),
        compiler_params=_cparams("arbitrary"),
        name="experts",
    )(*plan, xs, w_gate, b_gate.reshape(N_EXPERTS, 1, D_MODEL), w_up, b_up.reshape(N_EXPERTS, 1, D_MODEL),
      w_down, b_down.reshape(N_EXPERTS, 1, D_MODEL))


def _expert_plan(counts, n_rows):
    tm = MOE_TM
    n_tiles = n_rows // tm
    n_items = n_tiles + N_EXPERTS - 1
    ends = jnp.cumsum(counts)
    starts = ends - counts
    first_tile = starts // tm
    last_tile = jnp.maximum(ends - 1, 0) // tm
    items = jnp.where(counts > 0, last_tile - first_tile + 1, 0)
    item_end = jnp.cumsum(items)
    item_start = item_end - items
    total = item_end[-1]
    w = jnp.arange(n_items, dtype=jnp.int32)
    wc = jnp.minimum(w, total - 1)
    exp = jnp.sum((item_end[None, :] <= wc[:, None]).astype(jnp.int32), axis=1)
    ids = jnp.arange(N_EXPERTS, dtype=jnp.int32)
    mine = exp[:, None] == ids[None, :]

    def of_item(per_expert):
        return jnp.sum(jnp.where(mine, per_expert[None, :], 0), axis=1).astype(jnp.int32)

    tile = of_item(first_tile) + wc - of_item(item_start)
    lo = jnp.maximum(of_item(starts), tile * tm) - tile * tm
    hi = jnp.minimum(of_item(ends), (tile + 1) * tm) - tile * tm
    valid = w < total
    lo = jnp.where(valid, lo, 0).astype(jnp.int32)
    hi = jnp.where(valid, hi, 0).astype(jnp.int32)
    prev_exp = jnp.concatenate([jnp.full((1,), -1, jnp.int32), exp[:-1]])
    fresh = (valid & (exp != prev_exp)).astype(jnp.int32)
    later = jnp.where((counts > 0)[None, :] & (ids[None, :] > ids[:, None]), ids[None, :], N_EXPERTS)
    next_exp = jnp.min(later, axis=1)
    parity = ((jnp.cumsum((counts > 0).astype(jnp.int32)) - 1) % 2).astype(jnp.int32)
    next_of_item = of_item(next_exp)
    more = (next_of_item < N_EXPERTS).astype(jnp.int32)
    nxt = jnp.where(more == 1, next_of_item, exp).astype(jnp.int32)
    return tile, exp, lo, hi, fresh, of_item(parity), nxt, more


def _combine_kernel(n_steps, pos_ref, pos_next_ref, x1_ref, prob_ref, gfin_ref, ys_ref, y_ref, buf_ref, sem):
    i = pl.program_id(0)
    tm = x1_ref.shape[0]
    slot = i % 2

    def row_tile(off):
        return pl.ds(pl.multiple_of(off, ROW_SUB), ROW_SUB)

    def gather(p_ref, sl):
        def start(g, carry):
            r0 = g * ROW_COPY_UNROLL
            n = ROW_COPY_UNROLL * TOP_K
            src = [p_ref[0, 0, r0 * TOP_K + j] for j in range(n)]
            for j in range(n):
                _row_copy(ys_ref.at[row_tile(src[j])],
                          buf_ref.at[sl, j % TOP_K, row_tile((r0 + j // TOP_K) * ROW_SUB)], sem.at[sl]).start()
            return carry

        lax.fori_loop(0, tm // ROW_COPY_UNROLL, start, 0)

    @pl.when(i == 0)
    def _():
        gather(pos_ref, slot)

    @pl.when(i + 1 < n_steps)
    def _():
        gather(pos_next_ref, 1 - slot)

    for k in range(TOP_K):
        _row_copy(ys_ref.at[pl.ds(0, tm * ROW_SUB)], buf_ref.at[slot, k], sem.at[slot]).wait()

    prob = prob_ref[...]
    parts = []
    for s in range(ROW_SUB):
        part = x1_ref[:, s * HEAD_W:(s + 1) * HEAD_W]
        for k in range(TOP_K):
            part = part + prob[:, k:k + 1] * buf_ref[slot, k, pl.ds(s, tm, stride=ROW_SUB), :]
        parts.append(part)
    x2 = jnp.concatenate(parts, axis=-1)
    ms = jnp.mean(x2 * x2, axis=-1, keepdims=True)
    y_ref[...] = x2 * lax.rsqrt(ms + NORM_EPS) * gfin_ref[...]


def _combine(pos, x1, prob, g_final, ys):
    t = x1.shape[0]
    tm = ROW_TM
    n = t // tm
    pos3 = pos.reshape(n, 1, tm * TOP_K)
    return pl.pallas_call(
        functools.partial(_combine_kernel, n),
        grid=(n,),
        in_specs=[
            pl.BlockSpec((1, 1, tm * TOP_K), lambda i: (i, 0, 0), memory_space=pltpu.SMEM),
            pl.BlockSpec((1, 1, tm * TOP_K), lambda i: (jnp.minimum(i + 1, n - 1), 0, 0), memory_space=pltpu.SMEM),
            pl.BlockSpec((tm, D_MODEL), lambda i: (i, 0)),
            pl.BlockSpec((tm, TOP_K), lambda i: (i, 0)),
            pl.BlockSpec((1, D_MODEL), lambda i: (0, 0)),
            pl.BlockSpec(memory_space=pl.ANY),
        ],
        out_specs=pl.BlockSpec((tm, D_MODEL), lambda i: (i, 0)),
        out_shape=jax.ShapeDtypeStruct((t, D_MODEL), F32),
        scratch_shapes=[pltpu.VMEM((2, TOP_K, tm * ROW_SUB, HEAD_W), F32), pltpu.SemaphoreType.DMA((2,))],
        compiler_params=_cparams("arbitrary"),
        name="combine",
    )(pos3, pos3, x1, prob, g_final, ys)


def _lambda_init(layer):
    return 0.8 - 0.6 * math.exp(-0.3 * layer)


def kernel(x_prompt, x_sample, cache_k, cache_v, state_rec, g_mix, w_in, lambda_q1, lambda_k1, lambda_q2, lambda_k2, g_subln, w_o_attn, lb_logits, g_rec_norm, w_o_rec, w_out, g_ffn, w_router, b_router, w_gate, b_gate, w_up, b_up, w_down, b_down, g_final):
    nb, seq, _ = x_prompt.shape
    nbd, ld, _ = x_sample.shape
    past = cache_k.shape[2]
    tp, ts = nb * seq, nbd * ld
    lam_init = _lambda_init(0)
    lam = (jnp.exp(jnp.sum(lambda_q1[0].astype(F32) * lambda_k1[0].astype(F32)))
           - jnp.exp(jnp.sum(lambda_q2[0].astype(F32) * lambda_k2[0].astype(F32))) + lam_init).reshape(1)

    w_in_b = w_in[0].astype(BF16)
    woa, wor, wout = w_o_attn[0].astype(BF16), w_o_rec[0].astype(BF16), w_out[0].astype(BF16)
    wr = w_router[0].astype(BF16)
    g_mix2, g_ffn2, g_fin2 = g_mix[0].reshape(1, -1), g_ffn[0].reshape(1, -1), g_final.reshape(1, -1)
    gs2, grn2 = g_subln[0].reshape(1, -1), g_rec_norm[0].reshape(1, -1)
    br2 = b_router[0].reshape(1, -1)

    cos_p, sin_p = _rope_tables(jnp.arange(seq, dtype=jnp.int32))
    pos_s = past + (jnp.arange(PROJ_TM, dtype=jnp.int32) % ld)
    cos_s, sin_s = _rope_tables(pos_s)

    xp = x_prompt.reshape(tp, D_MODEL)
    xs_in = x_sample.reshape(ts, D_MODEL)
    zp, k_p, v_p, kb_p, vb_p = _project(xp, g_mix2, w_in_b, cos_p, sin_p, lb_logits)
    zs, k_s, v_s, kb_s, vb_s = _project(xs_in, g_mix2, w_in_b, cos_s, sin_s, lb_logits)

    ao_p = _attn_prompt(lam, zp, kb_p, vb_p, nb, seq)
    ro_p, st_p, ao_s = _hgrn(zp, jnp.zeros((nb, HEADS, HEAD_W, HEAD_W), F32), nb, seq // CHUNK, HGRN_GROUP_PROMPT,
                             sample=(lam, zs, kb_s, vb_s, cache_k[0], cache_v[0], nbd, ld))
    ro_s, st_s = _hgrn(zs, state_rec[0], nbd, ld // CHUNK, HGRN_GROUP_SAMPLE)

    merge_w = (gs2, grn2, woa, wor, wout, g_ffn2, wr, br2)
    zero_cnt = jnp.zeros((1, N_EXPERTS), F32)
    x1_p, xn_p, idx_p, prob_p, rank_p, cnt_p = _merge(lam_init, ao_p, ro_p, zp, xp, *merge_w, zero_cnt)
    x1_s, xn_s, idx_s, prob_s, rank_s, cnt = _merge(lam_init, ao_s, ro_s, zs, xs_in, *merge_w, cnt_p)

    counts = cnt[0].astype(jnp.int32)
    starts = jnp.cumsum(counts) - counts
    experts = jnp.arange(N_EXPERTS, dtype=jnp.int32)

    def positions(idx, rank):
        return (jnp.sum(jnp.where(idx[..., None] == experts, starts, 0), axis=-1) + rank) * ROW_SUB

    pos_p = positions(idx_p, rank_p)
    pos_s2 = positions(idx_s, rank_s)
    n_rows = (tp + ts) * TOP_K
    plan = _expert_plan(counts, n_rows)

    xs_sorted = _dispatch(jnp.concatenate([pos_p, pos_s2], axis=0), xn_p, xn_s)
    ys = _experts(plan, xs_sorted, w_gate[0], b_gate[0], w_up[0], b_up[0], w_down[0], b_down[0])
    y_p = _combine(pos_p, x1_p, prob_p, g_fin2, ys)
    y_s = _combine(pos_s2, x1_s, prob_s, g_fin2, ys)

    return (y_p.reshape(nb, seq, D_MODEL), y_s.reshape(nbd, ld, D_MODEL),
            k_p.reshape(1, nb, seq, HEADS, HEAD_W), v_p.reshape(1, nb, seq, HEADS, HEAD_W), st_p[None],
            k_s.reshape(1, nbd, ld, HEADS, HEAD_W), v_s.reshape(1, nbd, ld, HEADS, HEAD_W), st_s[None])
```

```python
import functools
import math

import jax
import jax.numpy as jnp
from jax import lax
from jax.experimental import pallas as pl
from jax.experimental.pallas import tpu as pltpu

F32 = jnp.float32
BF16 = jnp.bfloat16

D_MODEL = 1024
CHUNK = 64
HEADS = 8
HEAD_W = 128
ATTN_HEAD_DIM = 64
ROPE_THETA = 10000.0
Q_SCALE = ATTN_HEAD_DIM ** -0.5 * math.log2(math.e)
N_EXPERTS = 32
TOP_K = 4
SWIGLU_ALPHA = 1.702
SWIGLU_LIMIT = 7.0
NORM_EPS = 1e-6
N_COL_BLOCKS = 9
COL_Q, COL_K, COL_V, COL_F = range(4)
N_Z_BLOCKS = 7
ZC_Q, ZC_F, ZC_RQ, ZC_RI, ZC_RG, ZC_GA, ZC_GR = range(7)
PROJ_TM = 256

ATTN_TQ = 256
ATTN_TILE_GROUP = 4
CACHE_COPY_PARTS = 4
SAMPLE_CACHE_SLOTS = 4
HGRN_GROUP_PROMPT = 4
HGRN_GROUP_SAMPLE = 4
MERGE_TM = 512
MERGE_PARTS = 2
MOE_TM = 256
MOE_FF_SLICE = 512
ROW_TM = 256
ROW_COPY_UNROLL = 4
DISPATCH_SLOTS = 3
ROW_SUB = D_MODEL // HEAD_W
VMEM_LIMIT = 56 * 1024 * 1024


def _cparams(*sem):
    return pltpu.CompilerParams(dimension_semantics=sem, vmem_limit_bytes=VMEM_LIMIT)


def _dot(a, b):
    return jnp.dot(a, b, preferred_element_type=F32)


def _dot_nt(a, b):
    return lax.dot_general(a, b, (((1,), (1,)), ((), ())), preferred_element_type=F32)


def _proj_kernel(x_ref, g_ref, w_ref, cos_ref, sin_ref, lbl_ref,
                 z_ref, kout_ref, vout_ref, kb_ref, vb_ref):
    tm = x_ref.shape[0]
    x = x_ref[...]
    ms = jnp.mean(x * x, axis=-1, keepdims=True)
    hn = (x * lax.rsqrt(ms + NORM_EPS) * g_ref[...]).astype(BF16)

    def col(j):
        return _dot(hn, w_ref[:, j * D_MODEL:(j + 1) * D_MODEL])

    def zcols(c):
        return slice(c * D_MODEL, (c + 1) * D_MODEL)

    def head(h):
        return slice(h * HEAD_W, (h + 1) * HEAD_W)

    def head_rows(h):
        return pl.ds(h, tm, stride=HEADS)

    def rope(zh):
        lane = lax.broadcasted_iota(jnp.int32, zh.shape, 1)
        first = (lane % ATTN_HEAD_DIM) < (ATTN_HEAD_DIM // 2)
        partner = jnp.where(first, pltpu.roll(zh, HEAD_W - ATTN_HEAD_DIM // 2, 1),
                            pltpu.roll(zh, ATTN_HEAD_DIM // 2, 1))
        return zh * cos_ref[...] + partner * sin_ref[...]

    zq = col(COL_Q)
    for h in range(HEADS):
        z_ref[:, head(h)] = rope(zq[:, head(h)]) * Q_SCALE

    zk = col(COL_K)
    for h in range(HEADS):
        kh = rope(zk[:, head(h)])
        kout_ref[head_rows(h), :] = kh
        kb_ref[:, head(h)] = kh.astype(BF16)

    zv = col(COL_V)
    for h in range(HEADS):
        vout_ref[head_rows(h), :] = zv[:, head(h)]
    vb_ref[...] = zv.astype(BF16)

    lbl = lbl_ref[...]
    e = jnp.exp(lbl - jnp.max(lbl, axis=0, keepdims=True))
    lb = e[0:1] / jnp.sum(e, axis=0, keepdims=True)
    z_ref[:, zcols(ZC_F)] = lb + (1.0 - lb) * jax.nn.sigmoid(col(COL_F))

    for j in range(COL_F + 1, N_COL_BLOCKS):
        z_ref[:, zcols(j - COL_V)] = col(j)


def _project(x, g, w_bf16, cos_t, sin_t, lb_logits):
    t = x.shape[0]
    tm = PROJ_TM
    n_tab = cos_t.shape[0] // tm
    tok = lambda i: (i, 0)
    fixed = lambda i: (0, 0)
    return pl.pallas_call(
        _proj_kernel,
        grid=(t // tm,),
        in_specs=[
            pl.BlockSpec((tm, D_MODEL), tok),
            pl.BlockSpec((1, D_MODEL), fixed),
            pl.BlockSpec((D_MODEL, N_COL_BLOCKS * D_MODEL), fixed, pipeline_mode=pl.Buffered(1)),
            pl.BlockSpec((tm, HEAD_W), lambda i: (i % n_tab, 0)),
            pl.BlockSpec((tm, HEAD_W), lambda i: (i % n_tab, 0)),
            pl.BlockSpec((2, D_MODEL), fixed),
        ],
        out_specs=[
            pl.BlockSpec((tm, N_Z_BLOCKS * D_MODEL), tok),
            pl.BlockSpec((tm * HEADS, HEAD_W), tok),
            pl.BlockSpec((tm * HEADS, HEAD_W), tok),
            pl.BlockSpec((tm, D_MODEL), tok),
            pl.BlockSpec((tm, D_MODEL), tok),
        ],
        out_shape=[
            jax.ShapeDtypeStruct((t, N_Z_BLOCKS * D_MODEL), F32),
            jax.ShapeDtypeStruct((t * HEADS, HEAD_W), F32),
            jax.ShapeDtypeStruct((t * HEADS, HEAD_W), F32),
            jax.ShapeDtypeStruct((t, D_MODEL), BF16),
            jax.ShapeDtypeStruct((t, D_MODEL), BF16),
        ],
        compiler_params=_cparams("arbitrary"),
        name="proj",
    )(x, g, w_bf16, cos_t, sin_t, lb_logits)


def _rope_tables(pos):
    half = ATTN_HEAD_DIM // 2
    inv = jnp.power(ROPE_THETA, -2.0 * jnp.arange(half, dtype=F32) / ATTN_HEAD_DIM)
    ang = pos.astype(F32)[:, None] * inv[None, :]
    cos = jnp.tile(jnp.cos(ang), (1, HEAD_W // half))
    sin = jnp.sin(ang)
    sin = jnp.tile(jnp.concatenate([-sin, sin], axis=-1), (1, HEAD_W // ATTN_HEAD_DIM))
    return cos, sin


def _split_components(q):
    lane = lax.broadcasted_iota(jnp.int32, q.shape, 1)
    q1 = jnp.where(lane < ATTN_HEAD_DIM, q, 0.0)
    q2 = jnp.where(lane >= ATTN_HEAD_DIM, q, 0.0)
    return jnp.concatenate([q1, q2], axis=0).astype(BF16)


def _attn_prompt_kernel(lam_ref, q_ref, k_ref, v_ref, o_ref):
    lam = lam_ref[0]
    tq = ATTN_TQ
    seq = k_ref.shape[0]
    k = k_ref[...]
    v = v_ref[...]
    row = lax.broadcasted_iota(jnp.int32, (2 * tq, tq), 0)
    col = lax.broadcasted_iota(jnp.int32, (2 * tq, tq), 1)
    visible = (col // CHUNK) <= ((row % tq) // CHUNK)
    n_tiles = seq // tq

    def scores(qi):
        lo = qi * tq
        qq = _split_components(q_ref[lo:lo + tq, :])
        s_d = jnp.where(visible, _dot_nt(qq, k[lo:lo + tq]), -jnp.inf)
        return s_d, (_dot_nt(qq, k[:lo]) if qi > 0 else None)

    def weighted_values(qi, w_d, w_m):
        lo = qi * tq
        o = _dot(w_d, v[lo:lo + tq])
        if qi > 0:
            o = o + _dot(w_m, v[:lo])
        o_ref[lo:lo + tq, :] = o

    def softmax_weights(qi, s_d, s_m):
        m = jnp.max(s_d, axis=-1, keepdims=True)
        if qi > 0:
            m = jnp.maximum(m, jnp.max(s_m, axis=-1, keepdims=True))
            p_m = jnp.exp2(s_m - m)
        p_d = jnp.exp2(s_d - m)
        l = jnp.sum(p_d, axis=-1, keepdims=True)
        if qi > 0:
            l = l + jnp.sum(p_m, axis=-1, keepdims=True)
        r = 1.0 / l
        r1 = r[:tq]
        r2 = lam * r[tq:]
        w_d = (p_d[:tq] * r1 - p_d[tq:] * r2).astype(BF16)
        w_m = (p_m[:tq] * r1 - p_m[tq:] * r2).astype(BF16) if qi > 0 else None
        return w_d, w_m

    group = ATTN_TILE_GROUP
    ahead = [scores(qi) for qi in range(group)]
    for base in range(0, n_tiles, group):
        current = ahead
        if base + group < n_tiles:
            ahead = [scores(base + group + t) for t in range(group)]
        weights = [softmax_weights(base + t, *current[t]) for t in range(group)]
        for t in range(group):
            weighted_values(base + t, *weights[t])


def _attn_prompt(lam, z, kb, vb, n_batch, seq):
    return pl.pallas_call(
        _attn_prompt_kernel,
        grid=(n_batch, HEADS),
        in_specs=[
            pl.BlockSpec(memory_space=pltpu.SMEM),
            pl.BlockSpec((seq, HEAD_W), lambda b, h: (b, ZC_Q * HEADS + h)),
            pl.BlockSpec((seq, HEAD_W), lambda b, h: (b, h)),
            pl.BlockSpec((seq, HEAD_W), lambda b, h: (b, h)),
        ],
        out_specs=pl.BlockSpec((seq, HEAD_W), lambda b, h: (b, h)),
        out_shape=jax.ShapeDtypeStruct((n_batch * seq, D_MODEL), F32),
        compiler_params=_cparams("arbitrary", "arbitrary"),
        name="attn_prompt",
    )(lam, z, kb, vb)


def _sample_attention(lam, qs, kns, vns, ck_refs, cv_refs):
    n_q = qs[0].shape[0]
    qqs = [_split_components(q) for q in qs]
    scores = [(_dot_nt(qq, ck[...].astype(BF16)), _dot_nt(qq, kn)) for qq, ck, kn in zip(qqs, ck_refs, kns)]
    weights = []
    for s_c, s_n in scores:
        m = jnp.maximum(jnp.max(s_c, axis=-1, keepdims=True), jnp.max(s_n, axis=-1, keepdims=True))
        p_c = jnp.exp2(s_c - m)
        p_n = jnp.exp2(s_n - m)
        r = 1.0 / (jnp.sum(p_c, axis=-1, keepdims=True) + jnp.sum(p_n, axis=-1, keepdims=True))
        r1 = r[:n_q]
        r2 = lam * r[n_q:]
        weights.append(((p_c[:n_q] * r1 - p_c[n_q:] * r2).astype(BF16), (p_n[:n_q] * r1 - p_n[n_q:] * r2).astype(BF16)))
    return [_dot(w_c, cv[...].astype(BF16)) + _dot(w_n, vn) for (w_c, w_n), cv, vn in zip(weights, cv_refs, vns)]


def _split3(x):
    h1 = x.astype(BF16)
    r1 = x - h1.astype(F32)
    h2 = r1.astype(BF16)
    h3 = (r1 - h2.astype(F32)).astype(BF16)
    return h1, h2, h3


def _when_step(step, n_steps, which):
    def deco(body):
        if n_steps == 1:
            body()
        else:
            pl.when(step == which)(body)
    return deco


def _hgrn_kernel(n_chunks, side, *refs):
    if side is None:
        f_ref, q_ref, i_ref, s0_ref, o_ref, sout_ref, st_ref = refs
    else:
        (f_ref, q_ref, i_ref, s0_ref, lam_ref, qs_ref, kns_ref, vns_ref, ck_hbm, cv_hbm,
         o_ref, sout_ref, os_ref, st_ref, kbuf, vbuf, sem) = refs
    c = pl.program_id(1)
    n_group = f_ref.shape[0]

    @_when_step(c, n_chunks, 0)
    def _():
        for g in range(n_group):
            for h in range(HEADS):
                st_ref[g, h] = s0_ref[g, h].T

    if side is not None:
        pairs_per_step, n_pairs = side
        step = pl.program_id(0) * n_chunks + c

        def cache_copies(pair, sl):
            bb, hh = pair // HEADS, pair % HEADS
            part = kbuf.shape[1] // CACHE_COPY_PARTS
            copies = []
            for j in range(CACHE_COPY_PARTS):
                keys = pl.ds(j * part, part)
                copies.append(pltpu.make_async_copy(ck_hbm.at[bb, keys, hh, :], kbuf.at[sl, keys], sem.at[0, sl]))
                copies.append(pltpu.make_async_copy(cv_hbm.at[bb, keys, hh, :], vbuf.at[sl, keys], sem.at[1, sl]))
            return copies

        n_slots = kbuf.shape[0]
        duo = n_slots // 2

        @pl.when(step == 0)
        def _():
            for p in range(duo):
                for cp in cache_copies(p, p):
                    cp.start()

        def sample_duo(d):
            first = step * pairs_per_step + d * duo
            js = [d * duo + t for t in range(duo)]
            slots = [j % n_slots for j in js]

            @pl.when(first + duo < n_pairs)
            def _():
                for t in range(duo):
                    for cp in cache_copies(first + duo + t, (js[t] + duo) % n_slots):
                        cp.start()

            for t in range(duo):
                for cp in cache_copies(first + t, slots[t]):
                    cp.wait()
            lanes = [slice(j * HEAD_W, (j + 1) * HEAD_W) for j in js]
            outs = _sample_attention(lam_ref[0], [qs_ref[:, sl] for sl in lanes], [kns_ref[:, sl] for sl in lanes],
                                     [vns_ref[:, sl] for sl in lanes], [kbuf.at[sl] for sl in slots],
                                     [vbuf.at[sl] for sl in slots])
            for sl, o in zip(lanes, outs):
                os_ref[:, sl] = o

        duos_per_step = pairs_per_step // duo
    else:
        duos_per_step = 0

        def sample_duo(d):
            del d

    def sample_duos(half):
        for d in range(half * duos_per_step // 2, (half + 1) * duos_per_step // 2):
            sample_duo(d)

    row = lax.broadcasted_iota(jnp.int32, (CHUNK, CHUNK), 0)
    col = lax.broadcasted_iota(jnp.int32, (CHUNK, CHUNK), 1)
    causal = row >= col
    tril = jnp.where(causal, 1.0, 0.0).astype(BF16)
    heads = [slice(h * HEAD_W, (h + 1) * HEAD_W) for h in range(HEADS)]
    groups = range(n_group)
    sample_duos(0)
    fs = [f_ref[g] for g in groups]
    splits = [_split3(jnp.log(f)) for f in fs]
    bs = [_dot(tril, h1) + _dot(tril, h2) + _dot(tril, h3) for h1, h2, h3 in splits]
    scaled = []
    for g in groups:
        b = bs[g]
        b_last = b[CHUNK - 1:CHUNK, :]
        rk = 1.0 - fs[g]
        v = i_ref[g]
        scaled.append(((q_ref[g] * jnp.exp(b)).astype(BF16), (rk * jnp.exp(-b)).astype(BF16),
                       (rk * jnp.exp(b_last - b)).astype(BF16), v, v.astype(BF16), jnp.exp(b_last)))
    first = []
    for g in groups:
        qd_b, kd_b, kl_b, v, v_b, eb_last = scaled[g]
        states = [st_ref[g, h] for h in range(HEADS)]
        a_all = [_dot_nt(qd_b[:, sl], kd_b[:, sl]) for sl in heads]
        o_state = [_dot_nt(qd_b[:, sl], st.astype(BF16)) for sl, st in zip(heads, states)]
        s_upd = [_dot(v[:, sl].T.astype(BF16), kl_b[:, sl]) for sl in heads]
        first.append((states, a_all, o_state, s_upd))
    sample_duos(1)
    for g in groups:
        v_b, eb_last = scaled[g][4], scaled[g][5]
        states, a_all, o_state, s_upd = first[g]
        for h, sl in enumerate(heads):
            a = jnp.where(causal, a_all[h], 0.0).astype(BF16)
            o_ref[g, :, sl] = o_state[h] + _dot(a, v_b[:, sl])
            st_ref[g, h] = states[h] * eb_last[:, sl] + s_upd[h]

    @_when_step(c, n_chunks, n_chunks - 1)
    def _():
        for g in range(n_group):
            for h in range(HEADS):
                sout_ref[g, h] = st_ref[g, h].T


def _hgrn(z, s0, n_batch, n_chunks, n_group, sample=None):
    z3 = z.reshape(n_batch, n_chunks * CHUNK, N_Z_BLOCKS * D_MODEL)
    n_steps = (n_batch // n_group) * n_chunks

    def zblock(zc):
        return pl.BlockSpec((n_group, CHUNK, D_MODEL), lambda b, c: (b, c, zc))

    state = pl.BlockSpec((n_group, HEADS, HEAD_W, HEAD_W), lambda b, c: (b, 0, 0, 0))
    in_specs = [zblock(ZC_F), zblock(ZC_RQ), zblock(ZC_RI), state]
    operands = [z3, z3, z3, s0]
    out_specs = [pl.BlockSpec((n_group, CHUNK, D_MODEL), lambda b, c: (b, c, 0)), state]
    out_shape = [
        jax.ShapeDtypeStruct((n_batch, n_chunks * CHUNK, D_MODEL), F32),
        jax.ShapeDtypeStruct((n_batch, HEADS, HEAD_W, HEAD_W), F32),
    ]
    scratch = [pltpu.VMEM((n_group, HEADS, HEAD_W, HEAD_W), F32)]
    side = None
    if sample is not None:
        lam, zs, kb_s, vb_s, cache_k, cache_v, n_sb, n_q = sample
        n_pairs = n_sb * HEADS
        pps = n_pairs // n_steps
        assert pps * n_steps == n_pairs and pps % SAMPLE_CACHE_SLOTS == 0 and HEADS % pps == 0
        col_groups = HEADS // pps
        past = cache_k.shape[1]
        side = (pps, n_pairs)

        def srow(b, c):
            return (b * n_chunks + c) // col_groups

        def scol(b, c):
            return (b * n_chunks + c) % col_groups

        in_specs += [
            pl.BlockSpec(memory_space=pltpu.SMEM),
            pl.BlockSpec((n_q, pps * HEAD_W), lambda b, c: (srow(b, c), ZC_Q * col_groups + scol(b, c))),
            pl.BlockSpec((n_q, pps * HEAD_W), lambda b, c: (srow(b, c), scol(b, c))),
            pl.BlockSpec((n_q, pps * HEAD_W), lambda b, c: (srow(b, c), scol(b, c))),
            pl.BlockSpec(memory_space=pl.ANY), pl.BlockSpec(memory_space=pl.ANY),
        ]
        operands += [lam, zs, kb_s, vb_s, cache_k, cache_v]
        out_specs.append(pl.BlockSpec((n_q, pps * HEAD_W), lambda b, c: (srow(b, c), scol(b, c))))
        out_shape.append(jax.ShapeDtypeStruct((n_sb * n_q, D_MODEL), F32))
        scratch += [pltpu.VMEM((SAMPLE_CACHE_SLOTS, past, HEAD_W), F32),
                    pltpu.VMEM((SAMPLE_CACHE_SLOTS, past, HEAD_W), F32),
                    pltpu.SemaphoreType.DMA((2, SAMPLE_CACHE_SLOTS))]
    outs = pl.pallas_call(
        functools.partial(_hgrn_kernel, n_chunks, side),
        grid=(n_batch // n_group, n_chunks),
        in_specs=in_specs,
        out_specs=out_specs,
        out_shape=out_shape,
        scratch_shapes=scratch,
        compiler_params=_cparams("arbitrary", "arbitrary"),
        name="hgrn",
    )(*operands)
    return (outs[0].reshape(n_batch * n_chunks * CHUNK, D_MODEL),) + tuple(outs[1:])


def _head_rmsnorm(x, g):
    outs = []
    for h in range(HEADS):
        xh = x[:, h * HEAD_W:(h + 1) * HEAD_W]
        ms = jnp.mean(xh * xh, axis=-1, keepdims=True)
        outs.append(xh * lax.rsqrt(ms + NORM_EPS) * g)
    return jnp.concatenate(outs, axis=-1)


def _merge_kernel(lam_init, ao_ref, ro_ref, rg_ref, ga_ref, gr_ref, x_ref, gs_ref, grn_ref,
                  woa_ref, wor_ref, wout_ref, gffn_ref, wr_ref, br_ref, cnt_in_ref,
                  x1_ref, xn_ref, idx_ref, prob_ref, rank_ref, cnt_ref, run_ref):
    i = pl.program_id(0)
    tm = x_ref.shape[0]

    @pl.when(i == 0)
    def _():
        run_ref[...] = cnt_in_ref[...]

    part = tm // MERGE_PARTS

    def rows(p):
        return slice(p * part, (p + 1) * part)

    def norm_inputs(p, _):
        an = (_head_rmsnorm(ao_ref[rows(p), :], gs_ref[...]) * (1.0 - lam_init)).astype(BF16)
        rn = (_head_rmsnorm(ro_ref[rows(p), :], grn_ref[...]) * jax.nn.silu(rg_ref[rows(p), :])).astype(BF16)
        return an, rn

    def branch_projections(p, st):
        an, rn = st
        return _dot(an, woa_ref[...]), _dot(rn, wor_ref[...])

    def gate(p, st):
        a, r = st
        return (jax.nn.sigmoid(ga_ref[rows(p), :]) * a + jax.nn.sigmoid(gr_ref[rows(p), :]) * r).astype(BF16)

    def residual(p, mixed):
        x1 = x_ref[rows(p), :] + _dot(mixed, wout_ref[...])
        x1_ref[rows(p), :] = x1
        return x1

    def ffn_norm_router(p, x1):
        ms = jnp.mean(x1 * x1, axis=-1, keepdims=True)
        xn = x1 * lax.rsqrt(ms + NORM_EPS) * gffn_ref[...]
        for s in range(ROW_SUB):
            xn_ref[pl.ds(p * part * ROW_SUB + s, part, stride=ROW_SUB), :] = xn[:, s * HEAD_W:(s + 1) * HEAD_W]
        return _dot(xn.astype(BF16), wr_ref[...])

    stages = (norm_inputs, branch_projections, gate, residual, ffn_norm_router)
    state = [None] * MERGE_PARTS
    for stage in stages:
        for p in range(MERGE_PARTS):
            state[p] = stage(p, state[p])
    logits = jnp.concatenate(state, axis=0) + br_ref[...]
    lane = lax.broadcasted_iota(jnp.int32, logits.shape, 1).astype(F32)
    sel = jnp.zeros(logits.shape, F32)
    work = logits
    tops, idxs = [], []
    for _ in range(TOP_K):
        m = jnp.max(work, axis=-1, keepdims=True)
        idx = jnp.min(jnp.where(work == m, lane, float(N_EXPERTS)), axis=-1, keepdims=True)
        hit = lane == idx
        sel = jnp.where(hit, 1.0, sel)
        work = jnp.where(hit, -jnp.inf, work)
        tops.append(m)
        idxs.append(idx)
    es = [jnp.exp(v - tops[0]) for v in tops]
    inv = 1.0 / (es[0] + es[1] + es[2] + es[3])

    trow = lax.broadcasted_iota(jnp.int32, (tm, tm), 0)
    tcol = lax.broadcasted_iota(jnp.int32, (tm, tm), 1)
    before = jnp.where(trow > tcol, 1.0, 0.0).astype(BF16)
    ranks = _dot(before, sel.astype(BF16)) + run_ref[...]
    k_lane = lax.broadcasted_iota(jnp.int32, (tm, TOP_K), 1)
    idx_out = jnp.zeros((tm, TOP_K), F32)
    prob_out = jnp.zeros((tm, TOP_K), F32)
    rank_out = jnp.zeros((tm, TOP_K), F32)
    for k in range(TOP_K):
        rk = jnp.sum(jnp.where(lane == idxs[k], ranks, 0.0), axis=-1, keepdims=True)
        idx_out = jnp.where(k_lane == k, idxs[k], idx_out)
        prob_out = jnp.where(k_lane == k, es[k] * inv, prob_out)
        rank_out = jnp.where(k_lane == k, rk, rank_out)
    idx_ref[...] = idx_out.astype(jnp.int32)
    prob_ref[...] = prob_out
    rank_ref[...] = rank_out.astype(jnp.int32)
    run = run_ref[...] + jnp.sum(sel, axis=0, keepdims=True)
    run_ref[...] = run
    cnt_ref[...] = run


def _merge(lam_init, ao, ro, z, x, g_subln, g_rec, woa, wor, wout, g_ffn, w_router, b_router, cnt_in):
    t = x.shape[0]
    tm = MERGE_TM
    row = lambda i: (i, 0)
    fixed = lambda i: (0, 0)
    tok = pl.BlockSpec((tm, D_MODEL), row)
    wspec = pl.BlockSpec((D_MODEL, D_MODEL), fixed)
    narrow = pl.BlockSpec((tm, TOP_K), row)
    return pl.pallas_call(
        functools.partial(_merge_kernel, lam_init),
        grid=(t // tm,),
        in_specs=[
            tok, tok,
            pl.BlockSpec((tm, D_MODEL), lambda i: (i, ZC_RG)),
            pl.BlockSpec((tm, D_MODEL), lambda i: (i, ZC_GA)),
            pl.BlockSpec((tm, D_MODEL), lambda i: (i, ZC_GR)),
            tok,
            pl.BlockSpec((1, HEAD_W), fixed), pl.BlockSpec((1, HEAD_W), fixed),
            wspec, wspec, wspec,
            pl.BlockSpec((1, D_MODEL), fixed),
            pl.BlockSpec((D_MODEL, N_EXPERTS), fixed),
            pl.BlockSpec((1, N_EXPERTS), fixed),
            pl.BlockSpec((1, N_EXPERTS), fixed),
        ],
        out_specs=[tok, pl.BlockSpec((tm * ROW_SUB, HEAD_W), row), narrow, narrow, narrow,
                   pl.BlockSpec((1, N_EXPERTS), fixed)],
        out_shape=[
            jax.ShapeDtypeStruct((t, D_MODEL), F32),
            jax.ShapeDtypeStruct((t * ROW_SUB, HEAD_W), F32),
            jax.ShapeDtypeStruct((t, TOP_K), jnp.int32),
            jax.ShapeDtypeStruct((t, TOP_K), F32),
            jax.ShapeDtypeStruct((t, TOP_K), jnp.int32),
            jax.ShapeDtypeStruct((1, N_EXPERTS), F32),
        ],
        scratch_shapes=[pltpu.VMEM((1, N_EXPERTS), F32)],
        compiler_params=_cparams("arbitrary"),
        name="merge",
    )(ao, ro, z, z, z, x, g_subln, g_rec, woa, wor, wout, g_ffn, w_router, b_router, cnt_in)


def _row_copy(src, dst, sem):
    return pltpu.make_async_copy(src, dst, sem)


def _dispatch_kernel(n_first, n_steps, pos_ref, xa_hbm, xb_hbm, xs_hbm, stage, load_sem, copy_sem):
    i = pl.program_id(0)
    tm = ROW_TM
    rows = tm * ROW_SUB
    slot = i % DISPATCH_SLOTS

    def row_tile(off):
        return pl.ds(pl.multiple_of(off, ROW_SUB), ROW_SUB)

    def load(t, sl):
        @pl.when(t < n_first)
        def _():
            _row_copy(xa_hbm.at[pl.ds(pl.multiple_of(t * rows, rows), rows)], stage.at[sl], load_sem.at[sl]).start()

        @pl.when(t >= n_first)
        def _():
            _row_copy(xb_hbm.at[pl.ds(pl.multiple_of((t - n_first) * rows, rows), rows)], stage.at[sl],
                      load_sem.at[sl]).start()

    def drain(parity):
        for k in range(TOP_K):
            _row_copy(stage.at[0], xs_hbm.at[pl.ds(0, rows)], copy_sem.at[parity]).wait()

    @pl.when(i == 0)
    def _():
        load(i, slot)

    @pl.when(i + 1 < n_steps)
    def _():
        load(i + 1, (i + 1) % DISPATCH_SLOTS)

    _row_copy(xa_hbm.at[pl.ds(0, rows)], stage.at[slot], load_sem.at[slot]).wait()

    def start(g, carry):
        r0 = g * ROW_COPY_UNROLL
        n = ROW_COPY_UNROLL * TOP_K
        dst = [pos_ref[0, 0, r0 * TOP_K + j] for j in range(n)]
        for j in range(n):
            _row_copy(stage.at[slot, row_tile((r0 + j // TOP_K) * ROW_SUB)],
                      xs_hbm.at[row_tile(dst[j])], copy_sem.at[i % 2]).start(priority=j % 2)
        return carry

    lax.fori_loop(0, tm // ROW_COPY_UNROLL, start, 0)

    @pl.when(i > 0)
    def _():
        drain((i - 1) % 2)

    @pl.when(i == n_steps - 1)
    def _():
        drain(i % 2)


def _dispatch(pos, xn_a, xn_b):
    tm = ROW_TM
    n_a, n_b = xn_a.shape[0] // (tm * ROW_SUB), xn_b.shape[0] // (tm * ROW_SUB)
    pos3 = pos.reshape(n_a + n_b, 1, tm * TOP_K)
    return pl.pallas_call(
        functools.partial(_dispatch_kernel, n_a, n_a + n_b),
        grid=(n_a + n_b,),
        in_specs=[
            pl.BlockSpec((1, 1, tm * TOP_K), lambda i: (i, 0, 0), memory_space=pltpu.SMEM),
            pl.BlockSpec(memory_space=pl.ANY),
            pl.BlockSpec(memory_space=pl.ANY),
        ],
        out_specs=pl.BlockSpec(memory_space=pl.ANY),
        out_shape=jax.ShapeDtypeStruct((pos.size * ROW_SUB, HEAD_W), F32),
        scratch_shapes=[
            pltpu.VMEM((DISPATCH_SLOTS, tm * ROW_SUB, HEAD_W), F32),
            pltpu.SemaphoreType.DMA((DISPATCH_SLOTS,)),
            pltpu.SemaphoreType.DMA((2,)),
        ],
        compiler_params=_cparams("arbitrary"),
        name="dispatch",
    )(pos3, xn_a, xn_b)


def _experts_kernel(tile_ref, exp_ref, lo_ref, hi_ref, fresh_ref, par_ref, nxt_ref, more_ref,
                    xs_ref, wg_hbm, bg_ref, wu_hbm, bu_ref, wd_hbm, bd_ref, o_ref,
                    wg_s, wu_s, wd_s, x_s, acc_s, wf_s, wsem):
    w = pl.program_id(0)
    lo = lo_ref[w]
    hi = hi_ref[w]
    tm = x_s.shape[0]

    def lanes(s):
        return slice(s * HEAD_W, (s + 1) * HEAD_W)

    def sub_rows(s):
        return pl.ds(s, tm, stride=ROW_SUB)

    def weight_copies(e, sl):
        return [pltpu.make_async_copy(src.at[e], wf_s.at[sl, j], wsem.at[sl])
                for j, src in enumerate((wg_hbm, wu_hbm, wd_hbm))]

    @pl.when(w == 0)
    def _():
        for c in weight_copies(exp_ref[0], par_ref[0]):
            c.start()
        acc_s[...] = jnp.zeros(acc_s.shape, F32)

    @pl.when(fresh_ref[w] == 1)
    def _():
        sl = par_ref[w]
        for c in weight_copies(exp_ref[w], sl):
            c.wait()

        @pl.when(more_ref[w] == 1)
        def _():
            for c in weight_copies(nxt_ref[w], 1 - sl):
                c.start()

        wg_s[...] = wf_s[sl, 0].astype(BF16)
        wu_s[...] = wf_s[sl, 1].astype(BF16)
        wd_s[...] = wf_s[sl, 2].astype(BF16)

    @pl.when(hi > lo)
    def _():
        for s in range(ROW_SUB):
            x_s[:, lanes(s)] = xs_ref[sub_rows(s), :].astype(BF16)
        x = x_s[...]
        out = bd_ref[0]
        def gate_up(c):
            ff = slice(c, c + MOE_FF_SLICE)
            return _dot(x, wg_s[:, ff]) + bg_ref[0, :, ff], _dot(x, wu_s[:, ff]) + bu_ref[0, :, ff]

        slices = range(0, D_MODEL, MOE_FF_SLICE)
        gus = [gate_up(c) for c in slices]
        acts = []
        for gate, up in gus:
            gate = jnp.minimum(gate, SWIGLU_LIMIT)
            up = jnp.clip(up, -SWIGLU_LIMIT, SWIGLU_LIMIT)
            acts.append((gate * jax.nn.sigmoid(SWIGLU_ALPHA * gate) * (up + 1.0)).astype(BF16))
        for c, act in zip(slices, acts):
            out = out + _dot(act, wd_s[c:c + MOE_FF_SLICE, :])
        row = lax.broadcasted_iota(jnp.int32, (tm, 1), 0)
        merged = jnp.where((row >= lo) & (row < hi), out, acc_s[...])
        acc_s[...] = merged
        for s in range(ROW_SUB):
            o_ref[sub_rows(s), :] = merged[:, lanes(s)]


def _experts(plan, xs, w_gate, b_gate, w_up, b_up, w_down, b_down):
    tm = MOE_TM
    n_items = plan[0].shape[0]
    rows = lambda w, tile, *_: (tile[w], 0)
    wspec = pl.BlockSpec(memory_space=pl.ANY)
    bspec = pl.BlockSpec((1, 1, D_MODEL), lambda w, tile, exp, *_: (exp[w], 0, 0))
    grid_spec = pltpu.PrefetchScalarGridSpec(
        num_scalar_prefetch=len(plan),
        grid=(n_items,),
        in_specs=[pl.BlockSpec((tm * ROW_SUB, HEAD_W), rows), wspec, bspec, wspec, bspec, wspec, bspec],
        out_specs=pl.BlockSpec((tm * ROW_SUB, HEAD_W), rows),
        scratch_shapes=[pltpu.VMEM((D_MODEL, D_MODEL), BF16)] * 3 + [
            pltpu.VMEM((tm, D_MODEL), BF16),
            pltpu.VMEM((tm, D_MODEL), F32),
            pltpu.VMEM((2, 3, D_MODEL, D_MODEL), F32),
            pltpu.SemaphoreType.DMA((2,)),
        ],
    )
    return pl.pallas_call(
        _experts_kernel,
        grid_spec=grid_spec,
        out_shape=jax.ShapeDtypeStruct(xs.shape, F32),
        compiler_params=_cparams("arbitrary"),
        name="experts",
    )(*plan, xs, w_gate, b_gate.reshape(N_EXPERTS, 1, D_MODEL), w_up, b_up.reshape(N_EXPERTS, 1, D_MODEL),
      w_down, b_down.reshape(N_EXPERTS, 1, D_MODEL))


def _expert_plan(counts, n_rows):
    tm = MOE_TM
    n_tiles = n_rows // tm
    n_items = n_tiles + N_EXPERTS - 1
    ends = jnp.cumsum(counts)
    starts = ends - counts
    first_tile = starts // tm
    last_tile = jnp.maximum(ends - 1, 0) // tm
    items = jnp.where(counts > 0, last_tile - first_tile + 1, 0)
    item_end = jnp.cumsum(items)
    item_start = item_end - items
    total = item_end[-1]
    w = jnp.arange(n_items, dtype=jnp.int32)
    wc = jnp.minimum(w, total - 1)
    exp = jnp.sum((item_end[None, :] <= wc[:, None]).astype(jnp.int32), axis=1)
    ids = jnp.arange(N_EXPERTS, dtype=jnp.int32)
    mine = exp[:, None] == ids[None, :]

    def of_item(per_expert):
        return jnp.sum(jnp.where(mine, per_expert[None, :], 0), axis=1).astype(jnp.int32)

    tile = of_item(first_tile) + wc - of_item(item_start)
    lo = jnp.maximum(of_item(starts), tile * tm) - tile * tm
    hi = jnp.minimum(of_item(ends), (tile + 1) * tm) - tile * tm
    valid = w < total
    lo = jnp.where(valid, lo, 0).astype(jnp.int32)
    hi = jnp.where(valid, hi, 0).astype(jnp.int32)
    prev_exp = jnp.concatenate([jnp.full((1,), -1, jnp.int32), exp[:-1]])
    fresh = (valid & (exp != prev_exp)).astype(jnp.int32)
    later = jnp.where((counts > 0)[None, :] & (ids[None, :] > ids[:, None]), ids[None, :], N_EXPERTS)
    next_exp = jnp.min(later, axis=1)
    parity = ((jnp.cumsum((counts > 0).astype(jnp.int32)) - 1) % 2).astype(jnp.int32)
    next_of_item = of_item(next_exp)
    more = (next_of_item < N_EXPERTS).astype(jnp.int32)
    nxt = jnp.where(more == 1, next_of_item, exp).astype(jnp.int32)
    return tile, exp, lo, hi, fresh, of_item(parity), nxt, more


def _combine_kernel(n_steps, pos_ref, pos_next_ref, x1_ref, prob_ref, gfin_ref, ys_ref, y_ref, buf_ref, sem):
    i = pl.program_id(0)
    tm = x1_ref.shape[0]
    slot = i % 2

    def row_tile(off):
        return pl.ds(pl.multiple_of(off, ROW_SUB), ROW_SUB)

    def gather(p_ref, sl):
        def start(g, carry):
            r0 = g * ROW_COPY_UNROLL
            n = ROW_COPY_UNROLL * TOP_K
            src = [p_ref[0, 0, r0 * TOP_K + j] for j in range(n)]
            for j in range(n):
                _row_copy(ys_ref.at[row_tile(src[j])],
                          buf_ref.at[sl, j % TOP_K, row_tile((r0 + j // TOP_K) * ROW_SUB)],
                          sem.at[sl]).start(priority=j % 2)
            return carry

        lax.fori_loop(0, tm // ROW_COPY_UNROLL, start, 0)

    @pl.when(i == 0)
    def _():
        gather(pos_ref, slot)

    @pl.when(i + 1 < n_steps)
    def _():
        gather(pos_next_ref, 1 - slot)

    for k in range(TOP_K):
        _row_copy(ys_ref.at[pl.ds(0, tm * ROW_SUB)], buf_ref.at[slot, k], sem.at[slot]).wait()

    prob = prob_ref[...]
    parts = []
    for s in range(ROW_SUB):
        part = x1_ref[:, s * HEAD_W:(s + 1) * HEAD_W]
        for k in range(TOP_K):
            part = part + prob[:, k:k + 1] * buf_ref[slot, k, pl.ds(s, tm, stride=ROW_SUB), :]
        parts.append(part)
    x2 = jnp.concatenate(parts, axis=-1)
    ms = jnp.mean(x2 * x2, axis=-1, keepdims=True)
    y_ref[...] = x2 * lax.rsqrt(ms + NORM_EPS) * gfin_ref[...]


def _combine(pos, x1, prob, g_final, ys):
    t = x1.shape[0]
    tm = ROW_TM
    n = t // tm
    pos3 = pos.reshape(n, 1, tm * TOP_K)
    return pl.pallas_call(
        functools.partial(_combine_kernel, n),
        grid=(n,),
        in_specs=[
            pl.BlockSpec((1, 1, tm * TOP_K), lambda i: (i, 0, 0), memory_space=pltpu.SMEM),
            pl.BlockSpec((1, 1, tm * TOP_K), lambda i: (jnp.minimum(i + 1, n - 1), 0, 0), memory_space=pltpu.SMEM),
            pl.BlockSpec((tm, D_MODEL), lambda i: (i, 0)),
            pl.BlockSpec((tm, TOP_K), lambda i: (i, 0)),
            pl.BlockSpec((1, D_MODEL), lambda i: (0, 0)),
            pl.BlockSpec(memory_space=pl.ANY),
        ],
        out_specs=pl.BlockSpec((tm, D_MODEL), lambda i: (i, 0)),
        out_shape=jax.ShapeDtypeStruct((t, D_MODEL), F32),
        scratch_shapes=[pltpu.VMEM((2, TOP_K, tm * ROW_SUB, HEAD_W), F32), pltpu.SemaphoreType.DMA((2,))],
        compiler_params=_cparams("arbitrary"),
        name="combine",
    )(pos3, pos3, x1, prob, g_final, ys)


def _lambda_init(layer):
    return 0.8 - 0.6 * math.exp(-0.3 * layer)


def kernel(x_prompt, x_sample, cache_k, cache_v, state_rec, g_mix, w_in, lambda_q1, lambda_k1, lambda_q2, lambda_k2, g_subln, w_o_attn, lb_logits, g_rec_norm, w_o_rec, w_out, g_ffn, w_router, b_router, w_gate, b_gate, w_up, b_up, w_down, b_down, g_final):
    nb, seq, _ = x_prompt.shape
    nbd, ld, _ = x_sample.shape
    past = cache_k.shape[2]
    tp, ts = nb * seq, nbd * ld
    lam_init = _lambda_init(0)
    lam = (jnp.exp(jnp.sum(lambda_q1[0].astype(F32) * lambda_k1[0].astype(F32)))
           - jnp.exp(jnp.sum(lambda_q2[0].astype(F32) * lambda_k2[0].astype(F32))) + lam_init).reshape(1)

    w_in_b = w_in[0].astype(BF16)
    woa, wor, wout = w_o_attn[0].astype(BF16), w_o_rec[0].astype(BF16), w_out[0].astype(BF16)
    wr = w_router[0].astype(BF16)
    g_mix2, g_ffn2, g_fin2 = g_mix[0].reshape(1, -1), g_ffn[0].reshape(1, -1), g_final.reshape(1, -1)
    gs2, grn2 = g_subln[0].reshape(1, -1), g_rec_norm[0].reshape(1, -1)
    br2 = b_router[0].reshape(1, -1)

    cos_p, sin_p = _rope_tables(jnp.arange(seq, dtype=jnp.int32))
    pos_s = past + (jnp.arange(PROJ_TM, dtype=jnp.int32) % ld)
    cos_s, sin_s = _rope_tables(pos_s)

    xp = x_prompt.reshape(tp, D_MODEL)
    xs_in = x_sample.reshape(ts, D_MODEL)
    zp, k_p, v_p, kb_p, vb_p = _project(xp, g_mix2, w_in_b, cos_p, sin_p, lb_logits)
    zs, k_s, v_s, kb_s, vb_s = _project(xs_in, g_mix2, w_in_b, cos_s, sin_s, lb_logits)

    ao_p = _attn_prompt(lam, zp, kb_p, vb_p, nb, seq)
    ro_p, st_p, ao_s = _hgrn(zp, jnp.zeros((nb, HEADS, HEAD_W, HEAD_W), F32), nb, seq // CHUNK, HGRN_GROUP_PROMPT,
                             sample=(lam, zs, kb_s, vb_s, cache_k[0], cache_v[0], nbd, ld))
    ro_s, st_s = _hgrn(zs, state_rec[0], nbd, ld // CHUNK, HGRN_GROUP_SAMPLE)

    merge_w = (gs2, grn2, woa, wor, wout, g_ffn2, wr, br2)
    zero_cnt = jnp.zeros((1, N_EXPERTS), F32)
    x1_p, xn_p, idx_p, prob_p, rank_p, cnt_p = _merge(lam_init, ao_p, ro_p, zp, xp, *merge_w, zero_cnt)
    x1_s, xn_s, idx_s, prob_s, rank_s, cnt = _merge(lam_init, ao_s, ro_s, zs, xs_in, *merge_w, cnt_p)

    counts = cnt[0].astype(jnp.int32)
    starts = jnp.cumsum(counts) - counts
    experts = jnp.arange(N_EXPERTS, dtype=jnp.int32)

    def positions(idx, rank):
        return (jnp.sum(jnp.where(idx[..., None] == experts, starts, 0), axis=-1) + rank) * ROW_SUB

    pos_p = positions(idx_p, rank_p)
    pos_s2 = positions(idx_s, rank_s)
    n_rows = (tp + ts) * TOP_K
    plan = _expert_plan(counts, n_rows)

    xs_sorted = _dispatch(jnp.concatenate([pos_p, pos_s2], axis=0), xn_p, xn_s)
    ys = _experts(plan, xs_sorted, w_gate[0], b_gate[0], w_up[0], b_up[0], w_down[0], b_down[0])
    y_p = _combine(pos_p, x1_p, prob_p, g_fin2, ys)
    y_s = _combine(pos_s2, x1_s, prob_s, g_fin2, ys)

    return (y_p.reshape(nb, seq, D_MODEL), y_s.reshape(nbd, ld, D_MODEL),
            k_p.reshape(1, nb, seq, HEADS, HEAD_W), v_p.reshape(1, nb, seq, HEADS, HEAD_W), st_p[None],
            k_s.reshape(1, nbd, ld, HEADS, HEAD_W), v_s.reshape(1, nbd, ld, HEADS, HEAD_W), st_s[None])
```
